```python
import jax
import jax.numpy as jnp
from jax import lax
import numpy as np

D_MODEL = 1024
BATCH = 2
SEQ = 8192
DEPTH = 4
DEC_BATCH = 32
DEC_SEQ = 1
PAST_LEN = 8192
PAGE_SIZE = 128

N_A_LAYERS = (DEPTH + 1) // 2
N_C_LAYERS = DEPTH // 2
RW_HEAD = 64
RW_WIDTH = D_MODEL // 2
RW_HEADS = RW_WIDTH // RW_HEAD
D_DECAY_LORA = 64
D_AAA_LORA = 64
D_MV_LORA = 32
D_GATE_LORA = 128
RW_SPLITS = (RW_WIDTH, 2 * RW_WIDTH, 3 * RW_WIDTH, 3 * RW_WIDTH + D_DECAY_LORA,
             3 * RW_WIDTH + D_DECAY_LORA + D_AAA_LORA)
RW_IN = 3 * RW_WIDTH + D_DECAY_LORA + D_AAA_LORA + D_GATE_LORA
MB_HEAD = 64
MB_WIDTH = D_MODEL - RW_WIDTH
MB_HEADS = MB_WIDTH // MB_HEAD
MOBA_BLOCK = 256
MOBA_TOPK = 3
MOBA_Q_CHUNK = 64
EVEN_IN = RW_IN + 3 * MB_WIDTH
HG_EXPAND = 128
HG_HEADS = D_MODEL // HG_EXPAND
HG_DK = HG_EXPAND
HG_DV = D_MODEL // HG_HEADS
HG_CHUNK = 64
HG_SPLITS = (HG_HEADS * HG_DK, 2 * HG_HEADS * HG_DK, 2 * HG_HEADS * HG_DK + HG_HEADS * HG_DV)
ODD_IN = 2 * HG_HEADS * HG_DK + 2 * HG_HEADS * HG_DV
D_FF = -(-8 * D_MODEL // (3 * 256)) * 256
RMS_EPS = 1e-6
GN_EPS = 64e-5
MASK_NEG = -1e30

kernel_name = 'rwkv7_moba_hgrn2_decode_step'


def rms_norm(x, gain):
    xf = x.astype(jnp.float32)
    y = xf * lax.rsqrt(jnp.mean(xf * xf, axis=-1, keepdims=True) + RMS_EPS)
    return (y * gain.astype(jnp.float32)).astype(x.dtype)


def swiglu(h, w_gate, w_up, w_down):
    return (jax.nn.silu(h @ w_gate) * (h @ w_up)) @ w_down


def rwkv7_scan(r, w, k, v, kk, a, s0):
    def step(s, inp):
        r_t, w_t, k_t, v_t, kk_t, a_t = inp
        s_kk = jnp.einsum('bhvk,bhk->bhv', s, kk_t)
        s = (s * w_t[:, :, None, :] - s_kk[..., None] * (kk_t * a_t)[:, :, None, :]
             + v_t[..., None] * k_t[:, :, None, :])
        return s, jnp.einsum('bhvk,bhk->bhv', s, r_t)
    xs = tuple(jnp.moveaxis(t, 1, 0) for t in (r, w, k, v, kk, a))
    s, o = lax.scan(step, s0, xs)
    return jnp.moveaxis(o, 0, 1), s


def rwkv7_mix(p, shift_prev, s0, v_first, rw, vres):
    b, L, _ = p.shape
    f32 = jnp.float32
    mu, w0, w2, a0, a2, g2, k_k, k_a, r_k, ln_w, ln_b = rw
    pf = p.astype(f32)
    prev = jnp.concatenate([shift_prev.astype(f32)[:, None], pf[:, :-1]], axis=1)
    xs = pf + (prev - pf) * mu
    r, k, v, w_lo, a_lo, g_lo = jnp.split(xs, RW_SPLITS, axis=-1)
    w_log = -jax.nn.softplus(-(w0 + jnp.tanh(w_lo) @ w2)) - 0.5
    decay = jnp.exp(-jnp.exp(w_log))
    a = jax.nn.sigmoid(a0 + a_lo @ a2)
    g = jax.nn.sigmoid(g_lo) @ g2
    if vres is None:
        v_first = v
    else:
        v0, v1, v2 = vres
        v = v + (v_first - v) * jax.nn.sigmoid(v0 + (v @ v1) @ v2)
    heads = lambda t: t.reshape(b, L, RW_HEADS, RW_HEAD)
    kk = heads(k * k_k)
    kk = kk * lax.rsqrt(jnp.maximum(jnp.sum(kk * kk, axis=-1, keepdims=True), 1e-24))
    k = k * (1.0 + (a - 1.0) * k_a)
    o, s = rwkv7_scan(heads(r), heads(decay), heads(k), heads(v), kk, heads(a), s0.astype(f32))
    mean = jnp.mean(o, axis=-1, keepdims=True)
    var = jnp.mean(jnp.square(o - mean), axis=-1, keepdims=True)
    o = ((o - mean) * lax.rsqrt(var + GN_EPS)).reshape(b, L, RW_WIDTH) * ln_w + ln_b
    bonus = jnp.sum(heads(r) * heads(k) * r_k, axis=-1, keepdims=True) * heads(v)
    o = (o + bonus.reshape(b, L, RW_WIDTH)) * g
    return o.astype(p.dtype), v_first, p[:, -1], s


def _gather_page_sums(k_pool, k_extra, table):
    n_pool = k_pool.shape[0]
    s = jnp.sum(k_pool.astype(jnp.float32), axis=1)[jnp.clip(table, 0, n_pool - 1)]
    if k_extra is None:
        return s
    e = jnp.sum(k_extra.astype(jnp.float32), axis=1)[jnp.clip(table - n_pool, 0, k_extra.shape[0] - 1)]
    return jnp.where((table < n_pool)[..., None, None], s, e)


def _gather_rows(pool, extra, ids, heads):
    n_pool = pool.shape[0]
    rows = pool[jnp.clip(ids, 0, n_pool - 1), :, heads, :]
    if extra is None:
        return rows
    e = extra[jnp.clip(ids - n_pool, 0, extra.shape[0] - 1), :, heads, :]
    return jnp.where((ids < n_pool)[..., None, None], rows, e)


def moba_attention(q, q_pos, k_pool, v_pool, k_extra, v_extra, table):
    bq, nq, nh, hd = q.shape
    f32 = jnp.float32
    ppb = MOBA_BLOCK // PAGE_SIZE
    nb = table.shape[1] // ppb
    topk = min(MOBA_TOPK, nb)
    k1 = topk + 1
    bmean = _gather_page_sums(k_pool, k_extra, table).reshape(bq, nb, ppb, nh, hd).sum(2) / MOBA_BLOCK
    own = q_pos // MOBA_BLOCK
    gate = jnp.einsum('bqhd,bnhd->bhqn', q.astype(f32), bmean)
    gate = jnp.where(jnp.arange(nb)[None, :] < own[:, None], gate, MASK_NEG)
    _, sel = lax.top_k(gate, topk)
    blocks = jnp.concatenate([sel.astype(jnp.int32),
                              jnp.broadcast_to(own.astype(jnp.int32)[:, None], (bq, nh, nq, 1))], axis=-1)
    valid = jnp.concatenate([jnp.arange(topk)[None, :] < own[:, None], jnp.ones((nq, 1), bool)], axis=-1)
    qc = MOBA_Q_CHUNK if nq % MOBA_Q_CHUNK == 0 else nq
    nc = nq // qc
    q_items = q.reshape(bq * nc, qc, nh, hd)
    blk_items = blocks.reshape(bq, nh, nc, qc, k1).transpose(0, 2, 1, 3, 4).reshape(bq * nc, nh, qc, k1)
    item = jnp.arange(bq * nc, dtype=jnp.int32)
    b_idx, c_idx = item // nc, item % nc
    pos_c = q_pos.reshape(nc, qc)
    valid_c = valid.reshape(nc, qc, k1)
    heads = jnp.arange(nh)[:, None, None, None]
    scale = hd ** -0.5

    def attend(it):
        b, c, q_i, blk_i = it
        ids = table[b][blk_i[..., None] * ppb + jnp.arange(ppb)]
        k_rows = _gather_rows(k_pool, k_extra, ids, heads).reshape(nh, qc, k1 * MOBA_BLOCK, hd)
        v_rows = _gather_rows(v_pool, v_extra, ids, heads).reshape(nh, qc, k1 * MOBA_BLOCK, hd)
        k_pos = (blk_i[..., None] * MOBA_BLOCK + jnp.arange(MOBA_BLOCK)).reshape(nh, qc, k1 * MOBA_BLOCK)
        mask = jnp.repeat(valid_c[c], MOBA_BLOCK, axis=-1)[None] & (k_pos <= pos_c[c][None, :, None])
        s = jnp.einsum('qhd,hqkd->hqk', q_i.astype(f32), k_rows.astype(f32)) * scale
        prob = jax.nn.softmax(jnp.where(mask, s, MASK_NEG), axis=-1)
        return jnp.einsum('hqk,hqkd->qhd', prob, v_rows.astype(f32)).astype(q.dtype)

    out = lax.map(attend, (b_idx, c_idx, q_items, blk_items))
    return out.reshape(bq, nq, nh, hd)


def moba_prompt(q, k, v):
    b, L, nh, hd = q.shape
    lp = -(-L // MOBA_BLOCK) * MOBA_BLOCK
    def pages(t):
        return jnp.pad(t, ((0, 0), (0, lp - L), (0, 0), (0, 0))).reshape(b * lp // PAGE_SIZE, PAGE_SIZE, nh, hd)
    table = jnp.arange(b * lp // PAGE_SIZE, dtype=jnp.int32).reshape(b, lp // PAGE_SIZE)
    return moba_attention(q, jnp.arange(L, dtype=jnp.int32), pages(k), pages(v), None, None, table)


def moba_sample(q, k, v, k_cache, v_cache, page_table):
    b, L, nh, hd = q.shape
    n_pool = k_cache.shape[0]
    past = page_table.shape[1] * PAGE_SIZE
    n_new = -(-L // PAGE_SIZE)
    ppb = MOBA_BLOCK // PAGE_SIZE
    def pages(t):
        t = jnp.pad(t, ((0, 0), (0, n_new * PAGE_SIZE - L), (0, 0), (0, 0))).reshape(b * n_new, PAGE_SIZE, nh, hd)
        return jnp.concatenate([t, jnp.zeros((1, PAGE_SIZE, nh, hd), t.dtype)], axis=0)
    table = jnp.concatenate([page_table.astype(jnp.int32),
                             n_pool + jnp.arange(b * n_new, dtype=jnp.int32).reshape(b, n_new)], axis=1)
    pad = (-table.shape[1]) % ppb
    table = jnp.concatenate([table, jnp.full((b, pad), n_pool + b * n_new, jnp.int32)], axis=1)
    return moba_attention(q, past + jnp.arange(L, dtype=jnp.int32), k_cache, v_cache, pages(k), pages(v), table)


def even_mixer(h, shift0, wkv0, v_first, moba_fn, w_in, w_out, rw, vres, q_gain, k_gain):
    b, L, _ = h.shape
    p = h @ w_in
    o_rw, v_first, shift, wkv = rwkv7_mix(p[..., :RW_IN], shift0, wkv0, v_first, rw, vres)
    q, k, v = jnp.split(p[..., RW_IN:], 3, axis=-1)
    q = rms_norm(q.reshape(b, L, MB_HEADS, MB_HEAD), q_gain)
    k = rms_norm(k.reshape(b, L, MB_HEADS, MB_HEAD), k_gain)
    v = v.reshape(b, L, MB_HEADS, MB_HEAD)
    o_mb = moba_fn(q, k, v).reshape(b, L, MB_WIDTH)
    return jnp.concatenate([o_rw, o_mb], axis=-1) @ w_out, v_first, shift, wkv, k, v


def hgrn2_chunked(q, k, v, log_f, s0):
    b, L, nh, _ = q.shape
    c = min(HG_CHUNK, L)
    lp = -(-L // c) * c
    n = lp // c
    def blocks(t):
        t = jnp.pad(t, ((0, 0), (0, lp - L), (0, 0), (0, 0)))
        return t.reshape(b, n, c, nh, t.shape[-1]).transpose(1, 0, 3, 2, 4)
    causal = jnp.tril(jnp.ones((c, c), bool))[None, None, :, :, None]
    def step(s, inp):
        qc, kc, vc, gc = inp
        cum = jnp.cumsum(gc, axis=2)
        inter = jnp.einsum('bhtk,bhkv->bhtv', qc * jnp.exp(cum), s)
        diff = cum[:, :, :, None, :] - cum[:, :, None, :, :]
        rel = jnp.where(causal, jnp.exp(jnp.minimum(diff, 0.0)), 0.0)
        att = jnp.einsum('bhtk,bhsk,bhtsk->bhts', qc, kc, rel)
        o = inter + jnp.einsum('bhts,bhsv->bhtv', att, vc)
        last = cum[:, :, -1]
        s = jnp.exp(last)[..., None] * s + jnp.einsum('bhsk,bhsv->bhkv', kc * jnp.exp(last[:, :, None] - cum), vc)
        return s, o
    s, o = lax.scan(step, s0, (blocks(q), blocks(k), blocks(v), blocks(log_f)))
    o = o.transpose(1, 0, 3, 2, 4).reshape(b, lp, nh, -1)[:, :L]
    return o, s


def odd_mixer(h, s0, w_in, w_out, lb, out_gain):
    b, L, _ = h.shape
    f32 = jnp.float32
    q, f, i, g = jnp.split(h @ w_in, HG_SPLITS, axis=-1)
    q = jax.nn.silu(q.astype(f32))
    lb = lb.astype(f32)
    forget = lb + (1.0 - lb) * jax.nn.sigmoid(f.astype(f32))
    log_f = jnp.log(jnp.maximum(forget, 1e-30))
    k = 1.0 - forget
    hh = lambda t: t.reshape(b, L, HG_HEADS, -1)
    o, s = hgrn2_chunked(hh(q), hh(k), hh(i.astype(f32)), hh(log_f), s0.astype(f32))
    o = rms_norm(o, out_gain) * jax.nn.silu(hh(g.astype(f32)))
    return o.reshape(b, L, HG_HEADS * HG_DV).astype(h.dtype) @ w_out, s


def setup_inputs(seed: int = 0) -> dict:
    key = jax.random.key(seed)
    keys = iter(jax.random.split(key, 48))
    nrm = lambda shape, scale: scale * jax.random.normal(next(keys), shape, jnp.float32)
    n_pages = PAST_LEN // PAGE_SIZE
    n_used = DEC_BATCH * n_pages
    n_phys = n_used + max(1, n_used // 4)
    page_table = jax.random.permutation(next(keys), n_phys)[:n_used].reshape(DEC_BATCH, n_pages).astype(jnp.int32)
    d = D_MODEL
    return {
        'x_prompt': nrm((BATCH, SEQ, d), 1.0),
        'x_sample': nrm((DEC_BATCH, DEC_SEQ, d), 1.0),
        'cache_moba_k': nrm((N_A_LAYERS, n_phys, PAGE_SIZE, MB_HEADS, MB_HEAD), 1.0),
        'cache_moba_v': nrm((N_A_LAYERS, n_phys, PAGE_SIZE, MB_HEADS, MB_HEAD), 1.0),
        'page_table': page_table,
        'state_rwkv': nrm((N_A_LAYERS, DEC_BATCH, RW_HEADS, RW_HEAD, RW_HEAD), 0.3),
        'state_rwkv_shift': nrm((N_A_LAYERS, DEC_BATCH, RW_IN), 1.0),
        'state_hgrn': nrm((N_C_LAYERS, DEC_BATCH, HG_HEADS, HG_DK, HG_DV), 0.3),
        'norm_mix': 1.0 + nrm((DEPTH, d), 0.02),
        'norm_ffn': 1.0 + nrm((DEPTH, d), 0.02),
        'w_in_even': nrm((N_A_LAYERS, d, EVEN_IN), d ** -0.5),
        'w_out_even': nrm((N_A_LAYERS, RW_WIDTH + MB_WIDTH, d), (RW_WIDTH + MB_WIDTH) ** -0.5),
        'rw_mu': jax.random.uniform(next(keys), (N_A_LAYERS, RW_IN), jnp.float32),
        'rw_w0': jax.random.uniform(next(keys), (N_A_LAYERS, RW_WIDTH), jnp.float32, -6.0, 1.0),
        'rw_w2': nrm((N_A_LAYERS, D_DECAY_LORA, RW_WIDTH), 0.5 * D_DECAY_LORA ** -0.5),
        'rw_a0': nrm((N_A_LAYERS, RW_WIDTH), 0.1),
        'rw_a2': nrm((N_A_LAYERS, D_AAA_LORA, RW_WIDTH), D_AAA_LORA ** -0.5),
        'rw_g2': nrm((N_A_LAYERS, D_GATE_LORA, RW_WIDTH), D_GATE_LORA ** -0.5),
        'rw_k_k': 0.85 + nrm((N_A_LAYERS, RW_WIDTH), 0.02),
        'rw_k_a': 1.0 + nrm((N_A_LAYERS, RW_WIDTH), 0.02),
        'rw_r_k': nrm((N_A_LAYERS, RW_HEADS, RW_HEAD), 0.1),
        'rw_ln_w': 1.0 + nrm((N_A_LAYERS, RW_WIDTH), 0.02),
        'rw_ln_b': nrm((N_A_LAYERS, RW_WIDTH), 0.02),
        'rw_v0': 1.0 + nrm((N_A_LAYERS - 1, RW_WIDTH), 0.1),
        'rw_v1': nrm((N_A_LAYERS - 1, RW_WIDTH, D_MV_LORA), RW_WIDTH ** -0.5),
        'rw_v2': nrm((N_A_LAYERS - 1, D_MV_LORA, RW_WIDTH), D_MV_LORA ** -0.5),
        'mb_q_norm': 1.0 + nrm((N_A_LAYERS, MB_HEAD), 0.02),
        'mb_k_norm': 1.0 + nrm((N_A_LAYERS, MB_HEAD), 0.02),
        'w_in_odd': nrm((N_C_LAYERS, d, ODD_IN), d ** -0.5),
        'w_out_odd': nrm((N_C_LAYERS, HG_HEADS * HG_DV, d), (HG_HEADS * HG_DV) ** -0.5),
        'hg_lb_logits': nrm((N_C_LAYERS, HG_HEADS * HG_DK), 0.5),
        'hg_out_norm': 1.0 + nrm((N_C_LAYERS, HG_DV), 0.02),
        'ffn_w_gate': nrm((DEPTH, d, D_FF), d ** -0.5),
        'ffn_w_up': nrm((DEPTH, d, D_FF), d ** -0.5),
        'ffn_w_down': nrm((DEPTH, D_FF, d), D_FF ** -0.5),
    }


def reference(x_prompt, x_sample, cache_moba_k, cache_moba_v, page_table, state_rwkv, state_rwkv_shift,
              state_hgrn, norm_mix, norm_ffn, w_in_even, w_out_even, rw_mu, rw_w0, rw_w2, rw_a0, rw_a2, rw_g2,
              rw_k_k, rw_k_a, rw_r_k, rw_ln_w, rw_ln_b, rw_v0, rw_v1, rw_v2, mb_q_norm, mb_k_norm, w_in_odd,
              w_out_odd, hg_lb_logits, hg_out_norm, ffn_w_gate, ffn_w_up, ffn_w_down):
    f32 = jnp.float32
    nb_p = x_prompt.shape[0]
    lb_prob = jax.nn.softmax(hg_lb_logits.astype(f32), axis=0)
    lower_bounds = jnp.cumsum(lb_prob, axis=0) - lb_prob[0]
    xp, xs = x_prompt, x_sample
    vf_p = vf_s = None
    mk_p, mv_p, mk_s, mv_s = [], [], [], []
    wkv_p, wkv_s, sh_p, sh_s, hg_p, hg_s = [], [], [], [], [], []
    for layer in range(DEPTH):
        il = layer // 2
        hp = rms_norm(xp, norm_mix[layer])
        hs = rms_norm(xs, norm_mix[layer])
        if layer % 2 == 0:
            rw = (rw_mu[il], rw_w0[il], rw_w2[il], rw_a0[il], rw_a2[il], rw_g2[il], rw_k_k[il], rw_k_a[il],
                  rw_r_k[il], rw_ln_w[il], rw_ln_b[il])
            vres = None if il == 0 else (rw_v0[il - 1], rw_v1[il - 1], rw_v2[il - 1])
            args = (w_in_even[il], w_out_even[il], rw, vres, mb_q_norm[il], mb_k_norm[il])
            mp, vf_p, s_sh, s_wkv, kp, vp = even_mixer(
                hp, jnp.zeros((nb_p, RW_IN), xp.dtype), jnp.zeros((nb_p, RW_HEADS, RW_HEAD, RW_HEAD), f32),
                vf_p, moba_prompt, *args)
            sample_attn = lambda q, k, v, il=il: moba_sample(q, k, v, cache_moba_k[il], cache_moba_v[il], page_table)
            ms, vf_s, d_sh, d_wkv, ks, vs = even_mixer(
                hs, state_rwkv_shift[il], state_rwkv[il], vf_s, sample_attn, *args)
            mk_p.append(kp); mv_p.append(vp); mk_s.append(ks); mv_s.append(vs)
            wkv_p.append(s_wkv); wkv_s.append(d_wkv); sh_p.append(s_sh); sh_s.append(d_sh)
        else:
            mp, s_hg = odd_mixer(hp, jnp.zeros((nb_p, HG_HEADS, HG_DK, HG_DV), f32), w_in_odd[il],
                                 w_out_odd[il], lower_bounds[il], hg_out_norm[il])
            ms, d_hg = odd_mixer(hs, state_hgrn[il], w_in_odd[il], w_out_odd[il], lower_bounds[il], hg_out_norm[il])
            hg_p.append(s_hg); hg_s.append(d_hg)
        xp = xp + mp
        xs = xs + ms
        xp = xp + swiglu(rms_norm(xp, norm_ffn[layer]), ffn_w_gate[layer], ffn_w_up[layer], ffn_w_down[layer])
        xs = xs + swiglu(rms_norm(xs, norm_ffn[layer]), ffn_w_gate[layer], ffn_w_up[layer], ffn_w_down[layer])
    return (xp, xs,
            jnp.stack(mk_p), jnp.stack(mv_p), jnp.stack(mk_s), jnp.stack(mv_s),
            jnp.stack(wkv_p).astype(state_rwkv.dtype), jnp.stack(wkv_s).astype(state_rwkv.dtype),
            jnp.stack(sh_p).astype(state_rwkv_shift.dtype), jnp.stack(sh_s).astype(state_rwkv_shift.dtype),
            jnp.stack(hg_p).astype(state_hgrn.dtype), jnp.stack(hg_s).astype(state_hgrn.dtype))
```

```python
import functools

import numpy as np
import jax
import jax.numpy as jnp
from jax import lax
from jax.experimental import pallas as pl
from jax.experimental.pallas import tpu as pltpu

F32 = jnp.float32
BF16 = jnp.bfloat16
HI = lax.Precision.HIGHEST

D_MODEL = 1024
PAGE_SIZE = 128
RW_HEAD = 64
RW_WIDTH = 512
RW_HEADS = 8
RW_IN = 1792
MB_HEAD = 64
MB_WIDTH = 512
MB_HEADS = 8
MOBA_BLOCK = 256
MOBA_TOPK = 3
HG_HEADS = 8
HG_D = 128
D_FF = 2816
RMS_EPS = 1e-6
GN_EPS = 64e-5
MASK_NEG = -1e30

LANES = 128
VMEM_LIMIT = 56 * 1024 * 1024
RW_CHUNK = 64
HG_CHUNK = 64


def _cparams(*sem):
    return pltpu.CompilerParams(dimension_semantics=sem, vmem_limit_bytes=VMEM_LIMIT)


def _dot(a, b, precision=None):
    return jnp.dot(a, b, precision=precision, preferred_element_type=F32)


def _dot_nt(a, b, precision=None):
    return lax.dot_general(a, b, (((1,), (1,)), ((), ())), precision=precision,
                           preferred_element_type=F32)


def _dot_tn(a, b, precision=None):
    return lax.dot_general(a, b, (((0,), (0,)), ((), ())), precision=precision,
                           preferred_element_type=F32)


def _bmm(a, b):
    return lax.dot_general(a, b, (((2,), (1,)), ((0,), (0,))), precision=HI,
                           preferred_element_type=F32)


def _bmm_nt(a, b):
    return lax.dot_general(a, b, (((2,), (2,)), ((0,), (0,))), precision=HI,
                           preferred_element_type=F32)


def _bmm_tn(a, b):
    return lax.dot_general(a, b, (((1,), (1,)), ((0,), (0,))), precision=HI,
                           preferred_element_type=F32)


def _sigmoid(x):
    return 1.0 / (1.0 + jnp.exp(-x))


def _group_matrices(width, group):
    g = np.zeros((width, LANES), np.float32)
    g[np.arange(width), np.arange(width) // group] = 1.0
    return jnp.asarray(g), jnp.asarray(g.T.copy())


def _norm_matmul_kernel(x_ref, gain_ref, w_ref, *out_refs, widths, tn):
    x = x_ref[...]
    ms = jnp.mean(x * x, axis=-1, keepdims=True)
    h = (x * lax.rsqrt(ms + RMS_EPS) * gain_ref[...]).astype(BF16)
    col = 0
    for o_ref, width in zip(out_refs, widths):
        for c0 in range(0, width, tn):
            sz = min(tn, width - c0)
            o_ref[:, c0:c0 + sz] = _dot(h, w_ref[:, col + c0:col + c0 + sz])
        col += width


def norm_matmul(x, gain, w_bf16, widths, tm):
    m, d = x.shape
    n = w_bf16.shape[1]
    assert sum(widths) == n and m % tm == 0
    kern = functools.partial(_norm_matmul_kernel, widths=tuple(widths), tn=512)
    return pl.pallas_call(
        kern,
        grid=(m // tm,),
        in_specs=[pl.BlockSpec((tm, d), lambda i: (i, 0)),
                  pl.BlockSpec((1, d), lambda i: (0, 0)),
                  pl.BlockSpec((d, n), lambda i: (0, 0))],
        out_specs=[pl.BlockSpec((tm, wd), lambda i: (i, 0)) for wd in widths],
        out_shape=[jax.ShapeDtypeStruct((m, wd), F32) for wd in widths],
        compiler_params=_cparams("parallel"),
    )(x, gain.reshape(1, d), w_bf16)


def _mix_ffn_kernel(*refs, n_act):
    x_ref = refs[0]
    a_refs = refs[1:1 + n_act]
    wo_refs = refs[1 + n_act:1 + 2 * n_act]
    gain_ref, wg_ref, wu_ref, wd_ref, y_ref, x1_sc, h_sc, acc_sc = refs[1 + 2 * n_act:]
    j = pl.program_id(1)

    @pl.when(j == 0)
    def _():
        x1 = x_ref[...]
        for a_ref, wo_ref in zip(a_refs, wo_refs):
            x1 = x1 + _dot(a_ref[...].astype(BF16), wo_ref[...])
        x1_sc[...] = x1
        ms = jnp.mean(x1 * x1, axis=-1, keepdims=True)
        h_sc[...] = (x1 * lax.rsqrt(ms + RMS_EPS) * gain_ref[...]).astype(BF16)
        acc_sc[...] = jnp.zeros_like(acc_sc)

    h = h_sc[...]
    g = _dot(h, wg_ref[...])
    u = _dot(h, wu_ref[...])
    act = (g * _sigmoid(g) * u).astype(BF16)
    acc_sc[...] += _dot(act, wd_ref[...])

    @pl.when(j == pl.num_programs(1) - 1)
    def _():
        y_ref[...] = x1_sc[...] + acc_sc[...]


def mix_ffn(x, acts, wos_bf16, gain, wg, wu, wd, tm, tf):
    m, d = x.shape
    f = wg.shape[1]
    assert m % tm == 0 and f % tf == 0
    n_act = len(acts)
    in_specs = [pl.BlockSpec((tm, d), lambda i, j: (i, 0))]
    in_specs += [pl.BlockSpec((tm, a.shape[1]), lambda i, j: (i, 0)) for a in acts]
    in_specs += [pl.BlockSpec(w.shape, lambda i, j: (0, 0)) for w in wos_bf16]
    in_specs += [pl.BlockSpec((1, d), lambda i, j: (0, 0)),
                 pl.BlockSpec((d, tf), lambda i, j: (0, j)),
                 pl.BlockSpec((d, tf), lambda i, j: (0, j)),
                 pl.BlockSpec((tf, d), lambda i, j: (j, 0))]
    return pl.pallas_call(
        functools.partial(_mix_ffn_kernel, n_act=n_act),
        grid=(m // tm, f // tf),
        in_specs=in_specs,
        out_specs=pl.BlockSpec((tm, d), lambda i, j: (i, 0)),
        out_shape=jax.ShapeDtypeStruct((m, d), F32),
        scratch_shapes=[pltpu.VMEM((tm, d), F32), pltpu.VMEM((tm, d), BF16), pltpu.VMEM((tm, d), F32)],
        compiler_params=_cparams("parallel", "arbitrary"),
    )(x, *acts, *wos_bf16, gain.reshape(1, d), wg, wu, wd)


def _rwkv_prep_kernel(*refs, shift_mode, has_vres, tiles_per_seq):
    it = iter(refs)
    p_ref = next(it)
    prev_ref = next(it)
    mu_ref, w0_ref, a0_ref, kk_ref, ka_ref, w2_ref, a2_ref, g2_ref, grp_ref, grpt_ref = (
        next(it) for _ in range(10))
    if has_vres:
        vf_ref, v0_ref, v1_ref, v2_ref = (next(it) for _ in range(4))
    r_o, lw_o, k_o, v_o, kkn_o, b_o, g_o = (next(it) for _ in range(7))
    if not has_vres:
        vf_o = next(it)

    pf = p_ref[...]
    if shift_mode:
        first = (pl.program_id(0) % tiles_per_seq) == 0
        prev_row = jnp.where(first, 0.0, prev_ref[7:8, :])
        rows = lax.broadcasted_iota(jnp.int32, pf.shape, 0)
        prev = jnp.where(rows == 0, prev_row, pltpu.roll(pf, 1, axis=0))
    else:
        prev = prev_ref[...]
    xs = pf + (prev - pf) * mu_ref[...]
    r = xs[:, 0:RW_WIDTH]
    k = xs[:, RW_WIDTH:2 * RW_WIDTH]
    v = xs[:, 2 * RW_WIDTH:3 * RW_WIDTH]
    wa_lo = xs[:, 3 * RW_WIDTH:3 * RW_WIDTH + LANES]
    g_lo = xs[:, 3 * RW_WIDTH + LANES:RW_IN]
    z = w0_ref[...] + _dot(jnp.tanh(wa_lo), w2_ref[...], HI)
    sp = jnp.maximum(-z, 0.0) + jnp.log(1.0 + jnp.exp(-jnp.abs(z)))
    logw = -jnp.exp(-sp - 0.5)
    a = _sigmoid(a0_ref[...] + _dot(wa_lo, a2_ref[...], HI))
    g = _dot(_sigmoid(g_lo), g2_ref[...], HI)
    if has_vres:
        gate = _sigmoid(v0_ref[...] + _dot(_dot(v, v1_ref[...], HI), v2_ref[...], HI))
        v = v + (vf_ref[...] - v) * gate
    else:
        vf_o[...] = v
    kk = k * kk_ref[...]
    ss = _dot(kk * kk, grp_ref[...], HI)
    inv = _dot(lax.rsqrt(jnp.maximum(ss, 1e-24)), grpt_ref[...], HI)
    kk = kk * inv
    k = k * (1.0 + (a - 1.0) * ka_ref[...])
    r_o[...] = r
    lw_o[...] = logw
    k_o[...] = k
    v_o[...] = v
    kkn_o[...] = kk
    b_o[...] = kk * a
    g_o[...] = g


def rwkv_prep(p_rw, prev, params, vres, shift_mode, seq_len, tm):
    t = p_rw.shape[0]
    assert t % tm == 0
    mu, w0, w2p, a0, a2p, g2, k_k, k_a = params
    grp, grpt = _group_matrices(RW_WIDTH, RW_HEAD)
    row = lambda a: a.reshape(1, -1)
    tile = lambda wd: pl.BlockSpec((tm, wd), lambda i: (i, 0))
    full = lambda a: pl.BlockSpec(a.shape, lambda i: (0,) * a.ndim)
    args = [p_rw]
    specs = [tile(RW_IN)]
    if shift_mode:
        r8 = tm // 8
        args.append(p_rw)
        specs.append(pl.BlockSpec((8, RW_IN), lambda i: (jnp.maximum(i * r8 - 1, 0), 0)))
    else:
        args.append(prev)
        specs.append(tile(RW_IN))
    small = [row(mu), row(w0), row(a0), row(k_k), row(k_a), w2p, a2p, g2, grp, grpt]
    args += small
    specs += [full(a) for a in small]
    has_vres = vres is not None
    if has_vres:
        vf, v0, v1, v2 = vres
        extra = [row(v0), v1, v2]
        args += [vf] + extra
        specs += [tile(RW_WIDTH)] + [full(a) for a in extra]
    n_out = 7 if has_vres else 8
    kern = functools.partial(_rwkv_prep_kernel, shift_mode=shift_mode, has_vres=has_vres,
                             tiles_per_seq=max(seq_len // tm, 1))
    outs = pl.pallas_call(
        kern,
        grid=(t // tm,),
        in_specs=specs,
        out_specs=[tile(RW_WIDTH)] * n_out,
        out_shape=[jax.ShapeDtypeStruct((t, RW_WIDTH), F32)] * n_out,
        compiler_params=_cparams("parallel"),
    )(*args)
    return outs


def _unit_lower_inverse(n, c):
    ri = lax.broadcasted_iota(jnp.int32, (c, c), 0)
    ci = lax.broadcasted_iota(jnp.int32, (c, c), 1)
    eye = (ri == ci).astype(F32)[None]
    blk = min(16, c)
    same = ((ri // blk) == (ci // blk))[None]
    nd = jnp.where(same, n, 0.0)
    x = eye - nd
    pw = nd
    span = 2
    while span < blk:
        pw = _bmm(pw, pw)
        x = _bmm(x, eye + pw)
        span *= 2
    if c > blk:
        p = _bmm(x, jnp.where(same, 0.0, n))
        y = eye - p
        pw = p
        span = 2
        while span < c // blk:
            pw = _bmm(pw, pw)
            y = _bmm(y, eye + pw)
            span *= 2
        x = _bmm(y, x)
    return x


def _rwkv_scan_kernel(r_ref, lw_ref, k_ref, v_ref, kk_ref, b_ref, g_ref, lnw_ref, lnb_ref, rk_ref,
                      s0_ref, o_ref, sout_ref, s_sc, *, chunk):
    l = pl.program_id(2)
    tl = r_ref.shape[1]
    c = chunk
    nc = tl // c
    hd = RW_HEAD

    @pl.when(l == 0)
    def _():
        s_sc[...] = s0_ref[0]

    ri = lax.broadcasted_iota(jnp.int32, (c, c), 0)
    ci = lax.broadcasted_iota(jnp.int32, (c, c), 1)
    tri_incl = (ri >= ci)[None]
    tri_strict = (ri > ci)[None]
    tri_f = jnp.broadcast_to((ri >= ci).astype(F32)[None], (nc, c, c))

    for h in range(LANES // hd):
        sl = slice(h * hd, (h + 1) * hd)
        ch = lambda ref: ref[0][:, sl].reshape(nc, c, hd)
        r, lw, k, v, kk, b = ch(r_ref), ch(lw_ref), ch(k_ref), ch(v_ref), ch(kk_ref), ch(b_ref)
        lc = _bmm(tri_f, lw)
        tot = lc[:, c - 1:c, :]
        e_pos = jnp.exp(lc)
        e_neg = jnp.exp(-lc)
        e_end = jnp.exp(tot - lc)
        kk_t = kk * jnp.exp(lc - lw)
        r_t = r * e_pos
        b_t = b * e_neg
        k_t = k * e_neg
        a_ab = jnp.where(tri_strict, _bmm_nt(kk_t, b_t), 0.0)
        a_ak = jnp.where(tri_strict, _bmm_nt(kk_t, k_t), 0.0)
        a_rb = jnp.where(tri_incl, _bmm_nt(r_t, b_t), 0.0)
        a_rk = jnp.where(tri_incl, _bmm_nt(r_t, k_t), 0.0)
        tinv = _unit_lower_inverse(a_ab, c)
        wm = _bmm(tinv, kk_t)
        u0 = -_bmm(tinv, _bmm(a_ak, v))
        q_eff = r_t - _bmm(a_rb, wm)
        o0 = _bmm(a_rk, v) + _bmm(a_rb, u0)
        b_g = b * e_end
        k_g = k * e_end
        kr = lax.broadcasted_iota(jnp.int32, (hd, hd), 0)
        kc = lax.broadcasted_iota(jnp.int32, (hd, hd), 1)
        eye = (kr == kc).astype(F32)[None]
        m_mat = eye * jnp.exp(tot) - _bmm_tn(wm, b_g)
        d_mat = _bmm_tn(u0, b_g) + _bmm_tn(v, k_g)
        s = s_sc[h]
        states = []
        for i in range(nc):
            states.append(s)
            s = _dot(s, m_mat[i], HI) + d_mat[i]
        s_sc[h] = s
        o = o0 + _bmm_nt(q_eff, jnp.stack(states))
        o = o.reshape(tl, hd)
        mean = jnp.mean(o, axis=-1, keepdims=True)
        var = jnp.mean(jnp.square(o - mean), axis=-1, keepdims=True)
        o = (o - mean) * lax.rsqrt(var + GN_EPS) * lnw_ref[:, sl] + lnb_ref[:, sl]
        r2 = r_ref[0][:, sl]
        k2 = k_ref[0][:, sl]
        bonus = jnp.sum(r2 * k2 * rk_ref[:, sl], axis=-1, keepdims=True) * v_ref[0][:, sl]
        o_ref[0, :, sl] = (o + bonus) * g_ref[0][:, sl]

    @pl.when(l == pl.num_programs(2) - 1)
    def _():
        sout_ref[0] = s_sc[...]


def rwkv_scan(r, lw, k, v, kk, b, g, ln_w, ln_b, r_k, s0, tl, chunk):
    bsz, seq, _ = r.shape
    assert seq % tl == 0 and tl % chunk == 0
    hp = RW_WIDTH // LANES
    per = LANES // RW_HEAD
    seq_spec = pl.BlockSpec((1, tl, LANES), lambda bi, hi, li: (bi, li, hi))
    par_spec = pl.BlockSpec((1, LANES), lambda bi, hi, li: (0, hi))
    st_spec = pl.BlockSpec((1, per, RW_HEAD, RW_HEAD), lambda bi, hi, li: (bi, hi, 0, 0))
    row = lambda a: a.reshape(1, RW_WIDTH)
    return pl.pallas_call(
        functools.partial(_rwkv_scan_kernel, chunk=chunk),
        grid=(bsz, hp, seq // tl),
        in_specs=[seq_spec] * 7 + [par_spec] * 3 + [st_spec],
        out_specs=[seq_spec, st_spec],
        out_shape=[jax.ShapeDtypeStruct((bsz, seq, RW_WIDTH), F32),
                   jax.ShapeDtypeStruct((bsz, RW_HEADS, RW_HEAD, RW_HEAD), F32)],
        scratch_shapes=[pltpu.VMEM((per, RW_HEAD, RW_HEAD), F32)],
        compiler_params=_cparams("parallel", "parallel", "arbitrary"),
    )(r, lw, k, v, kk, b, g, row(ln_w), row(ln_b), row(r_k), s0)


def _moba_prep_kernel(p_ref, qg_ref, kg_ref, grp_ref, grpt_ref, q_o, k_o, kb_o, vb_o, ks_o):
    p = p_ref[...]
    q = p[:, 0:MB_WIDTH]
    k = p[:, MB_WIDTH:2 * MB_WIDTH]
    v = p[:, 2 * MB_WIDTH:3 * MB_WIDTH]

    def head_norm(x, gain_ref):
        ms = _dot(x * x, grp_ref[...], HI) * (1.0 / MB_HEAD)
        inv = _dot(lax.rsqrt(ms + RMS_EPS), grpt_ref[...], HI)
        return x * inv * gain_ref[...]

    qn = head_norm(q, qg_ref)
    kn = head_norm(k, kg_ref)
    q_o[...] = qn
    k_o[...] = kn
    kb_o[...] = kn.astype(BF16)
    vb_o[...] = v.astype(BF16)
    ks_o[0] = jnp.sum(kn, axis=0, keepdims=True)


def moba_prep(p_mb, q_gain, k_gain, tm):
    t = p_mb.shape[0]
    assert t % tm == 0
    grp, grpt = _group_matrices(MB_WIDTH, MB_HEAD)
    tile = lambda i: (i, 0)
    full = lambda a: pl.BlockSpec(a.shape, lambda i: (0,) * a.ndim)
    qg = jnp.tile(q_gain, MB_HEADS).reshape(1, MB_WIDTH)
    kg = jnp.tile(k_gain, MB_HEADS).reshape(1, MB_WIDTH)
    return pl.pallas_call(
        _moba_prep_kernel,
        grid=(t // tm,),
        in_specs=[pl.BlockSpec((tm, 3 * MB_WIDTH), tile), full(qg), full(kg), full(grp), full(grpt)],
        out_specs=[pl.BlockSpec((tm, MB_WIDTH), tile)] * 4
                  + [pl.BlockSpec((1, 1, MB_WIDTH), lambda i: (i, 0, 0))],
        out_shape=[jax.ShapeDtypeStruct((t, MB_WIDTH), F32), jax.ShapeDtypeStruct((t, MB_WIDTH), F32),
                   jax.ShapeDtypeStruct((t, MB_WIDTH), BF16), jax.ShapeDtypeStruct((t, MB_WIDTH), BF16),
                   jax.ShapeDtypeStruct((t // tm, 1, MB_WIDTH), F32)],
        compiler_params=_cparams("parallel"),
    )(p_mb, qg, kg, grp, grpt)


def _top_blocks(gate, idx, axis):
    sel = jnp.zeros_like(gate)
    rem = gate
    for _ in range(MOBA_TOPK):
        mx = jnp.max(rem, axis=axis, keepdims=True)
        live = jnp.logical_and(rem == mx, rem > 0.5 * MASK_NEG)
        first = jnp.min(jnp.where(live, idx, 1e9), axis=axis, keepdims=True)
        pick = idx == first
        sel = jnp.where(pick, 1.0, sel)
        rem = jnp.where(pick, MASK_NEG, rem)
    return sel


def _moba_attn_kernel(q_ref, k_ref, v_ref, ks_ref, o_ref):
    qi = pl.program_id(2)
    tq = q_ref.shape[1]
    nb = ks_ref.shape[1]
    scale = MB_HEAD ** -0.5
    q = q_ref[0]
    lane = lax.broadcasted_iota(jnp.int32, (1, LANES), 1)
    bmean = ks_ref[0] * (1.0 / MOBA_BLOCK)
    blk_f = lax.broadcasted_iota(jnp.int32, (tq, nb), 1).astype(F32)
    qi_f = qi.astype(F32)
    ri = lax.broadcasted_iota(jnp.int32, (tq, tq), 0)
    ci = lax.broadcasted_iota(jnp.int32, (tq, tq), 1)
    causal = ri >= ci
    row0 = pl.multiple_of(qi * tq, tq)
    k_d = k_ref[0, pl.ds(row0, tq), :]
    v_d = v_ref[0, pl.ds(row0, tq), :]

    n_heads = LANES // MB_HEAD
    qh, sel, carry = [], [], []
    for h in range(n_heads):
        q_h = jnp.where((lane // MB_HEAD) == h, q, 0.0)
        gate = _dot_nt(q_h, bmean, HI)
        gate = jnp.where(blk_f < qi_f, gate, MASK_NEG)
        sel.append(_top_blocks(gate, blk_f, 1))
        q_b = q_h.astype(BF16)
        qh.append(q_b)
        s = jnp.where(causal, _dot_nt(q_b, k_d) * scale, MASK_NEG)
        m = jnp.max(s, axis=1, keepdims=True)
        p = jnp.exp(s - m)
        carry += [m, jnp.sum(p, axis=1, keepdims=True), _dot(p.astype(BF16), v_d)]

    def body(n, carry):
        r0 = pl.multiple_of(n * tq, tq)
        k_n = k_ref[0, pl.ds(r0, tq), :]
        v_n = v_ref[0, pl.ds(r0, tq), :]
        n_f = n.astype(F32)
        out = []
        for h in range(n_heads):
            m, l, acc = carry[3 * h:3 * h + 3]
            picked = jnp.sum(jnp.where(blk_f == n_f, sel[h], 0.0), axis=1, keepdims=True) > 0.0
            s = jnp.where(picked, _dot_nt(qh[h], k_n) * scale, MASK_NEG)
            m_new = jnp.maximum(m, jnp.max(s, axis=1, keepdims=True))
            alpha = jnp.exp(m - m_new)
            p = jnp.exp(s - m_new)
            out += [m_new, alpha * l + jnp.sum(p, axis=1, keepdims=True),
                    alpha * acc + _dot(p.astype(BF16), v_n)]
        return tuple(out)

    carry = lax.fori_loop(0, qi, body, tuple(carry))
    res = carry[2] / carry[1]
    for h in range(1, n_heads):
        res = jnp.where((lane // MB_HEAD) == h, carry[3 * h + 2] / carry[3 * h + 1], res)
    o_ref[0] = res


def moba_attn(qn, kb, vb, ksum):
    bsz, seq, _ = qn.shape
    tq = MOBA_BLOCK
    nb = seq // tq
    hp = MB_WIDTH // LANES
    return pl.pallas_call(
        _moba_attn_kernel,
        grid=(bsz, hp, nb),
        in_specs=[pl.BlockSpec((1, tq, LANES), lambda b, h, i: (b, i, h)),
                  pl.BlockSpec((1, seq, LANES), lambda b, h, i: (b, 0, h)),
                  pl.BlockSpec((1, seq, LANES), lambda b, h, i: (b, 0, h)),
                  pl.BlockSpec((1, nb, LANES), lambda b, h, i: (b, 0, h))],
        out_specs=pl.BlockSpec((1, tq, LANES), lambda b, h, i: (b, i, h)),
        out_shape=jax.ShapeDtypeStruct((bsz, seq, MB_WIDTH), F32),
        compiler_params=_cparams("parallel", "parallel", "arbitrary"),
    )(qn, kb, vb, ksum)


def _moba_pages_kernel(pt_ref, q_ref, grp_ref, grpt_ref, *refs, n_pg):
    k_refs = refs[:n_pg]
    v_refs = refs[n_pg:2 * n_pg]
    ks_o, m_o, l_o, acc_o = refs[2 * n_pg:]
    scale = MB_HEAD ** -0.5
    q = q_ref[0]
    for j in range(n_pg):
        k = k_refs[j][0, 0]
        v = v_refs[j][0, 0]
        ks_o[0, j:j + 1, :] = jnp.sum(k, axis=0, keepdims=True)
        s = _dot(k * q, grp_ref[...], HI) * scale
        m = jnp.max(s, axis=0, keepdims=True)
        p = jnp.exp(s - m)
        m_o[0, j:j + 1, :] = m
        l_o[0, j:j + 1, :] = jnp.sum(p, axis=0, keepdims=True)
        pe = _dot(p, grpt_ref[...], HI)
        acc_o[0, j:j + 1, :] = jnp.sum(pe * v, axis=0, keepdims=True)


def _moba_merge_kernel(q_ref, kn_ref, vn_ref, ks_ref, m_ref, l_ref, acc_ref, grp_ref, grpt_ref,
                       pair_ref, pairt_ref, o_ref):
    scale = MB_HEAD ** -0.5
    q = q_ref[0]
    bsum = _dot(pair_ref[...], ks_ref[0], HI)
    nblk = bsum.shape[0]
    gate = _dot(bsum * (1.0 / MOBA_BLOCK) * q, grp_ref[...], HI)
    idx = lax.broadcasted_iota(jnp.int32, (nblk, LANES), 0).astype(F32)
    sel = _top_blocks(gate, idx, 0)
    sel_pg = _dot(pairt_ref[...], sel, HI) > 0.5
    s_own = _dot(kn_ref[0] * q, grp_ref[...], HI) * scale
    m_pg = jnp.where(sel_pg, m_ref[0], MASK_NEG)
    m_all = jnp.maximum(jnp.max(m_pg, axis=0, keepdims=True), s_own)
    w_pg = jnp.where(sel_pg, jnp.exp(m_pg - m_all), 0.0)
    w_own = jnp.exp(s_own - m_all)
    l_all = jnp.sum(w_pg * l_ref[0], axis=0, keepdims=True) + w_own
    acc = jnp.sum(_dot(w_pg, grpt_ref[...], HI) * acc_ref[0], axis=0, keepdims=True)
    acc = acc + _dot(w_own, grpt_ref[...], HI) * vn_ref[0]
    o_ref[0] = acc / _dot(l_all, grpt_ref[...], HI)


def moba_decode(qn, kn, vn, k_cache, v_cache, page_table, layer):
    bsz = qn.shape[0]
    n_lp = page_table.shape[1]
    ppb = MOBA_BLOCK // PAGE_SIZE
    assert n_lp % ppb == 0 and n_lp // ppb >= MOBA_TOPK
    n_pg = 8
    assert n_lp % n_pg == 0
    grp, grpt = _group_matrices(MB_WIDTH, MB_HEAD)
    q3 = qn.reshape(bsz, 1, MB_WIDTH)
    full = lambda a: pl.BlockSpec(a.shape, lambda b, g, pt: (0,) * a.ndim)

    def page_spec(j):
        return pl.BlockSpec((1, 1, PAGE_SIZE, MB_WIDTH),
                            lambda b, g, pt: (layer, pt[b, g * n_pg + j], 0, 0))

    part = lambda wd: pl.BlockSpec((1, n_pg, wd), lambda b, g, pt: (b, g, 0))
    ks, m, l, acc = pl.pallas_call(
        functools.partial(_moba_pages_kernel, n_pg=n_pg),
        grid_spec=pltpu.PrefetchScalarGridSpec(
            num_scalar_prefetch=1,
            grid=(bsz, n_lp // n_pg),
            in_specs=[pl.BlockSpec((1, 1, MB_WIDTH), lambda b, g, pt: (b, 0, 0)), full(grp), full(grpt)]
                     + [page_spec(j) for j in range(n_pg)] * 2,
            out_specs=[part(MB_WIDTH), part(LANES), part(LANES), part(MB_WIDTH)]),
        out_shape=[jax.ShapeDtypeStruct((bsz, n_lp, MB_WIDTH), F32),
                   jax.ShapeDtypeStruct((bsz, n_lp, LANES), F32),
                   jax.ShapeDtypeStruct((bsz, n_lp, LANES), F32),
                   jax.ShapeDtypeStruct((bsz, n_lp, MB_WIDTH), F32)],
        compiler_params=_cparams("parallel", "arbitrary"),
    )(page_table, q3, grp, grpt, *([k_cache] * n_pg), *([v_cache] * n_pg))

    nblk = n_lp // ppb
    pair = np.zeros((nblk, n_lp), np.float32)
    pair[np.arange(n_lp) // ppb, np.arange(n_lp)] = 1.0
    pair_t = jnp.asarray(pair.T.copy())
    pair = jnp.asarray(pair)
    vec = pl.BlockSpec((1, 1, MB_WIDTH), lambda b: (b, 0, 0))
    pg = lambda wd: pl.BlockSpec((1, n_lp, wd), lambda b: (b, 0, 0))
    full1 = lambda a: pl.BlockSpec(a.shape, lambda b: (0,) * a.ndim)
    out = pl.pallas_call(
        _moba_merge_kernel,
        grid=(bsz,),
        in_specs=[vec, vec, vec, pg(MB_WIDTH), pg(LANES), pg(LANES), pg(MB_WIDTH),
                  full1(grp), full1(grpt), full1(pair), full1(pair_t)],
        out_specs=vec,
        out_shape=jax.ShapeDtypeStruct((bsz, 1, MB_WIDTH), F32),
        compiler_params=_cparams("parallel"),
    )(q3, kn.reshape(bsz, 1, MB_WIDTH), vn.reshape(bsz, 1, MB_WIDTH), ks, m, l, acc,
      grp, grpt, pair, pair_t)
    return out.reshape(bsz, MB_WIDTH)


def _hgrn_constants(c):
    rows = np.arange(c)
    dmats, ups, los, masks = [], [], [], []
    z = c
    while z >= 2:
        seg = rows // z
        mid = seg * z + z // 2
        up = rows >= mid
        d = np.zeros((c, c), np.float32)
        for i in rows:
            if up[i]:
                d[i, mid[i]:i + 1] = 1.0
            else:
                d[i, i + 1:mid[i]] = 1.0
        dmats.append(d)
        ups.append(np.repeat(up[:, None], HG_D, 1).astype(np.float32))
        los.append(np.repeat(~up[:, None], HG_D, 1).astype(np.float32))
        masks.append(((seg[:, None] == seg[None, :]) & up[:, None] & (~up)[None, :]).astype(np.float32))
        z //= 2
    tri = np.tril(np.ones((c, c), np.float32))
    dmats += [tri, 1.0 - tri]
    return (jnp.asarray(np.concatenate(dmats, 0)), jnp.asarray(np.stack(ups)),
            jnp.asarray(np.stack(los)), jnp.asarray(np.stack(masks)))


def _hgrn_kernel(pq_ref, pf_ref, pi_ref, pg_ref, lbl_ref, gain_ref, s0_ref, dd_ref, up_ref, lo_ref,
                 mk_ref, o_ref, sout_ref, st_sc, *, chunk, layer, seq_valid):
    l = pl.program_id(2)
    tl = pq_ref.shape[1]
    c = chunk
    nc = tl // c
    nlev = up_ref.shape[0]
    er = lax.broadcasted_iota(jnp.int32, (HG_D, HG_D), 0)
    ec = lax.broadcasted_iota(jnp.int32, (HG_D, HG_D), 1)
    eye_d = (er == ec).astype(F32)

    @pl.when(l == 0)
    def _():
        st_sc[...] = _dot_nt(eye_d, s0_ref[0, 0], HI)

    logits = lbl_ref[...]
    ex = jnp.exp(logits - jnp.max(logits, axis=0, keepdims=True))
    prob = ex / jnp.sum(ex, axis=0, keepdims=True)
    lb = jnp.sum(prob[0:layer + 1], axis=0, keepdims=True) - prob[0:1]

    qr = pq_ref[0]
    q = qr * _sigmoid(qr)
    forget = lb + (1.0 - lb) * _sigmoid(pf_ref[0])
    logf = jnp.log(jnp.maximum(forget, 1e-30))
    kg = 1.0 - forget
    v = pi_ref[0]
    if seq_valid is not None:
        rows = lax.broadcasted_iota(jnp.int32, (tl, HG_D), 0) + l * tl
        ok = rows < seq_valid
        logf = jnp.where(ok, logf, 0.0)
        kg = jnp.where(ok, kg, 0.0)
        v = jnp.where(ok, v, 0.0)

    ri = lax.broadcasted_iota(jnp.int32, (c, c), 0)
    ci = lax.broadcasted_iota(jnp.int32, (c, c), 1)
    diag = ri == ci
    st = st_sc[...]
    outs = []
    for i in range(nc):
        rs = slice(i * c, (i + 1) * c)
        qc, kc, vc = q[rs], kg[rs], v[rs]
        e = jnp.exp(_dot(dd_ref[...], logf[rs], HI))
        a = jnp.where(diag, jnp.sum(qc * kc, axis=1, keepdims=True), 0.0)
        for lev in range(nlev):
            el = e[lev * c:(lev + 1) * c]
            a = a + mk_ref[lev] * _dot_nt(qc * el * up_ref[lev], kc * el * lo_ref[lev], HI)
        e_in = e[nlev * c:(nlev + 1) * c]
        e_out = e[(nlev + 1) * c:(nlev + 2) * c]
        outs.append(_dot_nt(qc * e_in, st, HI) + _dot(a, vc, HI))
        st = st * e_in[c - 1:c, :] + _dot_tn(vc, kc * e_out, HI)
    st_sc[...] = st
    o = jnp.concatenate(outs, axis=0) if nc > 1 else outs[0]
    ms = jnp.mean(o * o, axis=-1, keepdims=True)
    gr = pg_ref[0]
    o_ref[0] = o * lax.rsqrt(ms + RMS_EPS) * gain_ref[...] * (gr * _sigmoid(gr))

    @pl.when(l == pl.num_programs(2) - 1)
    def _():
        sout_ref[0, 0] = _dot_nt(eye_d, st, HI)


def hgrn_scan(p, lb_logits, out_gain, s0, layer, tl, chunk, seq_valid):
    bsz, seq, _ = p.shape
    assert seq % tl == 0 and tl % chunk == 0
    dd, up, lo, mk = _hgrn_constants(chunk)
    col = lambda off: pl.BlockSpec((1, tl, HG_D), lambda b, h, li: (b, li, off + h))
    full = lambda a: pl.BlockSpec(a.shape, lambda b, h, li: (0,) * a.ndim)
    st_spec = pl.BlockSpec((1, 1, HG_D, HG_D), lambda b, h, li: (b, h, 0, 0))
    kern = functools.partial(_hgrn_kernel, chunk=chunk, layer=layer,
                             seq_valid=None if seq_valid == seq else seq_valid)
    return pl.pallas_call(
        kern,
        grid=(bsz, HG_HEADS, seq // tl),
        in_specs=[col(0), col(HG_HEADS), col(2 * HG_HEADS), col(3 * HG_HEADS),
                  pl.BlockSpec((lb_logits.shape[0], HG_D), lambda b, h, li: (0, h)),
                  pl.BlockSpec((1, HG_D), lambda b, h, li: (0, 0)),
                  st_spec, full(dd), full(up), full(lo), full(mk)],
        out_specs=[pl.BlockSpec((1, tl, HG_D), lambda b, h, li: (b, li, h)), st_spec],
        out_shape=[jax.ShapeDtypeStruct((bsz, seq, D_MODEL), F32),
                   jax.ShapeDtypeStruct((bsz, HG_HEADS, HG_D, HG_D), F32)],
        scratch_shapes=[pltpu.VMEM((HG_D, HG_D), F32)],
        compiler_params=_cparams("parallel", "parallel", "arbitrary"),
    )(p, p, p, p, lb_logits, out_gain.reshape(1, HG_D), s0, dd, up, lo, mk)


SAMPLE_PAD = 8


def _pad_seq(a, n):
    return jnp.pad(a[:, None, :], ((0, 0), (0, n - 1), (0, 0)))


def _even_layer(xp, xs, il, vf_p, vf_s, w, cache_k, cache_v, page_table, state_rwkv, state_shift):
    nb_p, seq, d = xp.shape
    nb_s = xs.shape[0]
    tp = nb_p * seq
    rw_params = (w['rw_mu'][il], w['rw_w0'][il], w['w2p'][il], w['rw_a0'][il], w['a2p'][il],
                 w['rw_g2'][il], w['rw_k_k'][il], w['rw_k_a'][il])
    has_vres = il > 0
    vres_w = (w['rw_v0'][il - 1], w['rw_v1'][il - 1], w['rw_v2'][il - 1]) if has_vres else None
    ln = (w['rw_ln_w'][il], w['rw_ln_b'][il], w['rw_r_k'][il].reshape(-1))

    p_rw, p_mb = norm_matmul(xp.reshape(tp, d), w['norm_mix'][2 * il], w['w_in_even'][il],
                             (RW_IN, 3 * MB_WIDTH), 512)
    vres = (vf_p,) + vres_w if has_vres else None
    prep = rwkv_prep(p_rw, None, rw_params, vres, True, seq, 512)
    if not has_vres:
        vf_p = prep[7]
    seq3 = lambda a: a.reshape(nb_p, seq, -1)
    o_rw, wkv_p = rwkv_scan(*[seq3(a) for a in prep[:7]], *ln,
                            jnp.zeros((nb_p, RW_HEADS, RW_HEAD, RW_HEAD), F32), 512, RW_CHUNK)
    qn, kn, kb, vb, ksum = moba_prep(p_mb, w['mb_q_norm'][il], w['mb_k_norm'][il], MOBA_BLOCK)
    o_mb = moba_attn(seq3(qn), seq3(kb), seq3(vb), ksum.reshape(nb_p, seq // MOBA_BLOCK, MB_WIDTH))
    lyr = 2 * il
    yp = mix_ffn(xp.reshape(tp, d), [o_rw.reshape(tp, RW_WIDTH), o_mb.reshape(tp, MB_WIDTH)],
                 [w['w_out_even'][il][:RW_WIDTH], w['w_out_even'][il][RW_WIDTH:]],
                 w['norm_ffn'][lyr], w['ffn_w_gate'][lyr], w['ffn_w_up'][lyr], w['ffn_w_down'][lyr],
                 512, D_FF // 2).reshape(nb_p, seq, d)
    mk_p = kn.reshape(nb_p, seq, MB_HEADS, MB_HEAD)
    mv_p = p_mb[:, 2 * MB_WIDTH:].reshape(nb_p, seq, MB_HEADS, MB_HEAD)
    sh_p = p_rw.reshape(nb_p, seq, RW_IN)[:, -1]

    ps_rw, ps_mb = norm_matmul(xs.reshape(nb_s, d), w['norm_mix'][2 * il], w['w_in_even'][il],
                               (RW_IN, 3 * MB_WIDTH), nb_s)
    vres = (vf_s,) + vres_w if has_vres else None
    prep = rwkv_prep(ps_rw, state_shift[il], rw_params, vres, False, 1, nb_s)
    if not has_vres:
        vf_s = prep[7]
    os_rw, wkv_s = rwkv_scan(*[_pad_seq(a, SAMPLE_PAD) for a in prep[:7]], *ln, state_rwkv[il],
                             SAMPLE_PAD, SAMPLE_PAD)
    qs, ks, _, _, _ = moba_prep(ps_mb, w['mb_q_norm'][il], w['mb_k_norm'][il], nb_s)
    vs = ps_mb[:, 2 * MB_WIDTH:]
    os_mb = moba_decode(qs, ks, vs, cache_k, cache_v, page_table, il)
    ys = mix_ffn(xs.reshape(nb_s, d), [os_rw[:, 0], os_mb],
                 [w['w_out_even'][il][:RW_WIDTH], w['w_out_even'][il][RW_WIDTH:]],
                 w['norm_ffn'][lyr], w['ffn_w_gate'][lyr], w['ffn_w_up'][lyr], w['ffn_w_down'][lyr],
                 nb_s, D_FF // 2).reshape(nb_s, 1, d)
    mk_s = ks.reshape(nb_s, 1, MB_HEADS, MB_HEAD)
    mv_s = vs.reshape(nb_s, 1, MB_HEADS, MB_HEAD)
    return yp, ys, vf_p, vf_s, (mk_p, mv_p, mk_s, mv_s, wkv_p, wkv_s, sh_p, ps_rw)


def _odd_layer(xp, xs, il, w, state_hgrn):
    nb_p, seq, d = xp.shape
    nb_s = xs.shape[0]
    tp = nb_p * seq
    lyr = 2 * il + 1
    (pp,) = norm_matmul(xp.reshape(tp, d), w['norm_mix'][lyr], w['w_in_odd'][il], (4 * d,), 512)
    o_p, hg_p = hgrn_scan(pp.reshape(nb_p, seq, 4 * d), w['hg_lb_logits'], w['hg_out_norm'][il],
                          jnp.zeros((nb_p, HG_HEADS, HG_D, HG_D), F32), il, 256, HG_CHUNK, seq)
    yp = mix_ffn(xp.reshape(tp, d), [o_p.reshape(tp, d)], [w['w_out_odd'][il]],
                 w['norm_ffn'][lyr], w['ffn_w_gate'][lyr], w['ffn_w_up'][lyr], w['ffn_w_down'][lyr],
                 512, D_FF // 2).reshape(nb_p, seq, d)
    (ps,) = norm_matmul(xs.reshape(nb_s, d), w['norm_mix'][lyr], w['w_in_odd'][il], (4 * d,), nb_s)
    o_s, hg_s = hgrn_scan(_pad_seq(ps, SAMPLE_PAD), w['hg_lb_logits'], w['hg_out_norm'][il],
                          state_hgrn[il], il, SAMPLE_PAD, SAMPLE_PAD, 1)
    ys = mix_ffn(xs.reshape(nb_s, d), [o_s[:, 0]], [w['w_out_odd'][il]],
                 w['norm_ffn'][lyr], w['ffn_w_gate'][lyr], w['ffn_w_up'][lyr], w['ffn_w_down'][lyr],
                 nb_s, D_FF // 2).reshape(nb_s, 1, d)
    return yp, ys, hg_p, hg_s


def kernel(x_prompt, x_sample, cache_moba_k, cache_moba_v, page_table, state_rwkv, state_rwkv_shift, state_hgrn, norm_mix, norm_ffn, w_in_even, w_out_even, rw_mu, rw_w0, rw_w2, rw_a0, rw_a2, rw_g2, rw_k_k, rw_k_a, rw_r_k, rw_ln_w, rw_ln_b, rw_v0, rw_v1, rw_v2, mb_q_norm, mb_k_norm, w_in_odd, w_out_odd, hg_lb_logits, hg_out_norm, ffn_w_gate, ffn_w_up, ffn_w_down):
    depth = norm_mix.shape[0]
    n_a = w_in_even.shape[0]
    zeros_lora = jnp.zeros_like(rw_w2)
    w = dict(
        norm_mix=norm_mix, norm_ffn=norm_ffn,
        w_in_even=w_in_even.astype(BF16), w_out_even=w_out_even.astype(BF16),
        w_in_odd=w_in_odd.astype(BF16), w_out_odd=w_out_odd.astype(BF16),
        ffn_w_gate=ffn_w_gate.astype(BF16), ffn_w_up=ffn_w_up.astype(BF16),
        ffn_w_down=ffn_w_down.astype(BF16),
        rw_mu=rw_mu, rw_w0=rw_w0, rw_a0=rw_a0, rw_g2=rw_g2, rw_k_k=rw_k_k, rw_k_a=rw_k_a,
        rw_r_k=rw_r_k, rw_ln_w=rw_ln_w, rw_ln_b=rw_ln_b, rw_v0=rw_v0, rw_v1=rw_v1, rw_v2=rw_v2,
        w2p=jnp.concatenate([rw_w2, zeros_lora], axis=1), a2p=jnp.concatenate([zeros_lora, rw_a2], axis=1),
        mb_q_norm=mb_q_norm, mb_k_norm=mb_k_norm, hg_lb_logits=hg_lb_logits, hg_out_norm=hg_out_norm)
    n_phys = cache_moba_k.shape[1]
    cache_k = cache_moba_k.reshape(n_a, n_phys, PAGE_SIZE, MB_WIDTH)
    cache_v = cache_moba_v.reshape(n_a, n_phys, PAGE_SIZE, MB_WIDTH)
    page_table = page_table.astype(jnp.int32)

    xp, xs = x_prompt, x_sample
    vf_p = vf_s = None
    even_out, hg_out = [], []
    for layer in range(depth):
        il = layer // 2
        if layer % 2 == 0:
            xp, xs, vf_p, vf_s, outs = _even_layer(xp, xs, il, vf_p, vf_s, w, cache_k, cache_v,
                                                   page_table, state_rwkv, state_rwkv_shift)
            even_out.append(outs)
        else:
            xp, xs, hg_p, hg_s = _odd_layer(xp, xs, il, w, state_hgrn)
            hg_out.append((hg_p, hg_s))
    stack = lambda i: jnp.stack([o[i] for o in even_out])
    return (xp, xs, stack(0), stack(1), stack(2), stack(3), stack(4), stack(5), stack(6), stack(7),
            jnp.stack([o[0] for o in hg_out]), jnp.stack([o[1] for o in hg_out]))
```

```python
import functools

import numpy as np
import jax
import jax.numpy as jnp
from jax import lax
from jax.experimental import pallas as pl
from jax.experimental.pallas import tpu as pltpu

F32 = jnp.float32
BF16 = jnp.bfloat16
HI = lax.Precision.HIGHEST

D_MODEL = 1024
PAGE_SIZE = 128
RW_HEAD = 64
RW_WIDTH = 512
RW_HEADS = 8
RW_IN = 1792
MB_HEAD = 64
MB_WIDTH = 512
MB_HEADS = 8
MOBA_BLOCK = 256
MOBA_TOPK = 3
HG_HEADS = 8
HG_D = 128
D_FF = 2816
RMS_EPS = 1e-6
GN_EPS = 64e-5
MASK_NEG = -1e30
LOG2E = 1.4426950408889634

LANES = 128
VMEM_LIMIT = 56 * 1024 * 1024
RW_CHUNK = 64
HG_CHUNK = 64


def _cparams(*sem):
    return pltpu.CompilerParams(dimension_semantics=sem, vmem_limit_bytes=VMEM_LIMIT)


def _bf(x):
    return x.astype(BF16)


def _dot(a, b, precision=None):
    return jnp.dot(a, b, precision=precision, preferred_element_type=F32)


def _dot_nt(a, b, precision=None):
    return lax.dot_general(a, b, (((1,), (1,)), ((), ())), precision=precision,
                           preferred_element_type=F32)


def _dot_tn(a, b, precision=None):
    return lax.dot_general(a, b, (((0,), (0,)), ((), ())), precision=precision,
                           preferred_element_type=F32)


def _bmm(a, b):
    return lax.dot_general(_bf(a), _bf(b), (((2,), (1,)), ((0,), (0,))), preferred_element_type=F32)


def _bmm_nt(a, b):
    return lax.dot_general(_bf(a), _bf(b), (((2,), (2,)), ((0,), (0,))), preferred_element_type=F32)


def _bmm_tn(a, b):
    return lax.dot_general(_bf(a), _bf(b), (((1,), (1,)), ((0,), (0,))), preferred_element_type=F32)


def _split_dot(a_bf16, b):
    n = b.shape[1]
    hi = b.astype(BF16)
    lo = (b - hi.astype(F32)).astype(BF16)
    out = _dot(a_bf16, jnp.concatenate([hi, lo], axis=1))
    return out[:, :n] + out[:, n:]


def _sigmoid(x):
    return 1.0 / (1.0 + jnp.exp(-x))


def _group_matrices(width, group):
    g = np.zeros((width, LANES), np.float32)
    g[np.arange(width), np.arange(width) // group] = 1.0
    return jnp.asarray(g), jnp.asarray(g.T.copy())


def _norm_matmul_kernel(x_ref, gain_ref, w_ref, *out_refs, widths, tn):
    x = x_ref[...]
    ms = jnp.mean(x * x, axis=-1, keepdims=True)
    h = (x * lax.rsqrt(ms + RMS_EPS) * gain_ref[...]).astype(BF16)
    col = 0
    for o_ref, width in zip(out_refs, widths):
        for c0 in range(0, width, tn):
            sz = min(tn, width - c0)
            o_ref[:, c0:c0 + sz] = _dot(h, w_ref[:, col + c0:col + c0 + sz])
        col += width


def norm_matmul(x, gain, w_bf16, widths, tm):
    m, d = x.shape
    n = w_bf16.shape[1]
    assert sum(widths) == n and m % tm == 0
    kern = functools.partial(_norm_matmul_kernel, widths=tuple(widths), tn=512)
    return pl.pallas_call(
        kern,
        grid=(m // tm,),
        in_specs=[pl.BlockSpec((tm, d), lambda i: (i, 0)),
                  pl.BlockSpec((1, d), lambda i: (0, 0)),
                  pl.BlockSpec((d, n), lambda i: (0, 0))],
        out_specs=[pl.BlockSpec((tm, wd), lambda i: (i, 0)) for wd in widths],
        out_shape=[jax.ShapeDtypeStruct((m, wd), F32) for wd in widths],
        compiler_params=_cparams("parallel"),
    )(x, gain.reshape(1, d), w_bf16)


def _mix_ffn_kernel(*refs, n_act):
    x_ref = refs[0]
    a_refs = refs[1:1 + n_act]
    wo_refs = refs[1 + n_act:1 + 2 * n_act]
    gain_ref, wg_ref, wu_ref, wd_ref, y_ref, x1_sc, h_sc, acc_sc = refs[1 + 2 * n_act:]
    j = pl.program_id(1)

    @pl.when(j == 0)
    def _():
        x1 = x_ref[...]
        for a_ref, wo_ref in zip(a_refs, wo_refs):
            x1 = x1 + _dot(a_ref[...].astype(BF16), wo_ref[...])
        x1_sc[...] = x1
        ms = jnp.mean(x1 * x1, axis=-1, keepdims=True)
        h_sc[...] = (x1 * lax.rsqrt(ms + RMS_EPS) * gain_ref[...]).astype(BF16)
        acc_sc[...] = jnp.zeros_like(acc_sc)

    h = h_sc[...]
    g = _dot(h, wg_ref[...])
    u = _dot(h, wu_ref[...])
    act = (g * _sigmoid(g) * u).astype(BF16)
    acc_sc[...] += _dot(act, wd_ref[...])

    @pl.when(j == pl.num_programs(1) - 1)
    def _():
        y_ref[...] = x1_sc[...] + acc_sc[...]


def mix_ffn(x, acts, wos_bf16, gain, wg, wu, wd, tm, tf):
    m, d = x.shape
    f = wg.shape[1]
    assert m % tm == 0 and f % tf == 0
    n_act = len(acts)
    in_specs = [pl.BlockSpec((tm, d), lambda i, j: (i, 0))]
    in_specs += [pl.BlockSpec((tm, a.shape[1]), lambda i, j: (i, 0)) for a in acts]
    in_specs += [pl.BlockSpec(w.shape, lambda i, j: (0, 0)) for w in wos_bf16]
    in_specs += [pl.BlockSpec((1, d), lambda i, j: (0, 0)),
                 pl.BlockSpec((d, tf), lambda i, j: (0, j)),
                 pl.BlockSpec((d, tf), lambda i, j: (0, j)),
                 pl.BlockSpec((tf, d), lambda i, j: (j, 0))]
    return pl.pallas_call(
        functools.partial(_mix_ffn_kernel, n_act=n_act),
        grid=(m // tm, f // tf),
        in_specs=in_specs,
        out_specs=pl.BlockSpec((tm, d), lambda i, j: (i, 0)),
        out_shape=jax.ShapeDtypeStruct((m, d), F32),
        scratch_shapes=[pltpu.VMEM((tm, d), F32), pltpu.VMEM((tm, d), BF16), pltpu.VMEM((tm, d), F32)],
        compiler_params=_cparams("parallel", "arbitrary"),
    )(x, *acts, *wos_bf16, gain.reshape(1, d), wg, wu, wd)


def _rwkv_prep_kernel(*refs, shift_mode, has_vres, tiles_per_seq):
    it = iter(refs)
    p_ref = next(it)
    prev_ref = next(it)
    mu_ref, w0_ref, a0_ref, kk_ref, ka_ref, w2_ref, a2_ref, g2_ref, grp_ref, grpt_ref = (
        next(it) for _ in range(10))
    if has_vres:
        vf_ref, v0_ref, v1_ref, v2_ref = (next(it) for _ in range(4))
    r_o, lw_o, k_o, v_o, kkn_o, b_o, g_o = (next(it) for _ in range(7))
    if not has_vres:
        vf_o = next(it)

    pf = p_ref[...]
    if shift_mode:
        first = (pl.program_id(0) % tiles_per_seq) == 0
        prev_row = jnp.where(first, 0.0, prev_ref[7:8, :])
        rows = lax.broadcasted_iota(jnp.int32, pf.shape, 0)
        prev = jnp.where(rows == 0, prev_row, pltpu.roll(pf, 1, axis=0))
    else:
        prev = prev_ref[...]
    xs = pf + (prev - pf) * mu_ref[...]
    r = xs[:, 0:RW_WIDTH]
    k = xs[:, RW_WIDTH:2 * RW_WIDTH]
    v = xs[:, 2 * RW_WIDTH:3 * RW_WIDTH]
    wa_lo = xs[:, 3 * RW_WIDTH:3 * RW_WIDTH + LANES]
    g_lo = xs[:, 3 * RW_WIDTH + LANES:RW_IN]
    z = w0_ref[...] + _dot(jnp.tanh(wa_lo), w2_ref[...], HI)
    sp = jnp.maximum(-z, 0.0) + jnp.log(1.0 + jnp.exp(-jnp.abs(z)))
    logw = -jnp.exp(-sp - 0.5)
    a = _sigmoid(a0_ref[...] + _dot(wa_lo, a2_ref[...], HI))
    g = _dot(_sigmoid(g_lo), g2_ref[...], HI)
    if has_vres:
        gate = _sigmoid(v0_ref[...] + _dot(_dot(v, v1_ref[...], HI), v2_ref[...], HI))
        v = v + (vf_ref[...] - v) * gate
    else:
        vf_o[...] = v
    kk = k * kk_ref[...]
    ss = _dot(kk * kk, grp_ref[...], HI)
    inv = _dot(lax.rsqrt(jnp.maximum(ss, 1e-24)), grpt_ref[...], HI)
    kk = kk * inv
    k = k * (1.0 + (a - 1.0) * ka_ref[...])
    r_o[...] = r
    lw_o[...] = logw
    k_o[...] = k
    v_o[...] = v
    kkn_o[...] = kk
    b_o[...] = kk * a
    g_o[...] = g


def rwkv_prep(p_rw, prev, params, vres, shift_mode, seq_len, tm):
    t = p_rw.shape[0]
    assert t % tm == 0
    mu, w0, w2p, a0, a2p, g2, k_k, k_a = params
    grp, grpt = _group_matrices(RW_WIDTH, RW_HEAD)
    row = lambda a: a.reshape(1, -1)
    tile = lambda wd: pl.BlockSpec((tm, wd), lambda i: (i, 0))
    full = lambda a: pl.BlockSpec(a.shape, lambda i: (0,) * a.ndim)
    args = [p_rw]
    specs = [tile(RW_IN)]
    if shift_mode:
        r8 = tm // 8
        args.append(p_rw)
        specs.append(pl.BlockSpec((8, RW_IN), lambda i: (jnp.maximum(i * r8 - 1, 0), 0)))
    else:
        args.append(prev)
        specs.append(tile(RW_IN))
    small = [row(mu), row(w0), row(a0), row(k_k), row(k_a), w2p, a2p, g2, grp, grpt]
    args += small
    specs += [full(a) for a in small]
    has_vres = vres is not None
    if has_vres:
        vf, v0, v1, v2 = vres
        extra = [row(v0), v1, v2]
        args += [vf] + extra
        specs += [tile(RW_WIDTH)] + [full(a) for a in extra]
    n_out = 7 if has_vres else 8
    kern = functools.partial(_rwkv_prep_kernel, shift_mode=shift_mode, has_vres=has_vres,
                             tiles_per_seq=max(seq_len // tm, 1))
    outs = pl.pallas_call(
        kern,
        grid=(t // tm,),
        in_specs=specs,
        out_specs=[tile(RW_WIDTH)] * n_out,
        out_shape=[jax.ShapeDtypeStruct((t, RW_WIDTH), F32)] * n_out,
        compiler_params=_cparams("parallel"),
    )(*args)
    return outs


def _unit_lower_inverse(n, c):
    ri = lax.broadcasted_iota(jnp.int32, (c, c), 0)
    ci = lax.broadcasted_iota(jnp.int32, (c, c), 1)
    eye = (ri == ci).astype(F32)[None]
    blk = min(16, c)
    same = ((ri // blk) == (ci // blk))[None]
    nd = jnp.where(same, n, 0.0)
    x = eye - nd
    pw = nd
    span = 2
    while span < blk:
        pw = _bmm(pw, pw)
        x = _bmm(x, eye + pw)
        span *= 2
    if c > blk:
        p = _bmm(x, jnp.where(same, 0.0, n))
        y = eye - p
        pw = p
        span = 2
        while span < c // blk:
            pw = _bmm(pw, pw)
            y = _bmm(y, eye + pw)
            span *= 2
        x = _bmm(y, x)
    return x


def _rwkv_scan_kernel(r_ref, lw_ref, k_ref, v_ref, kk_ref, b_ref, g_ref, lnw_ref, lnb_ref, rk_ref,
                      s0_ref, tri_ref, o_ref, sout_ref, s_sc, *, chunk):
    l = pl.program_id(2)
    tl = r_ref.shape[1]
    c = chunk
    nc = tl // c
    hd = RW_HEAD

    @pl.when(l == 0)
    def _():
        s_sc[...] = s0_ref[0]

    ri = lax.broadcasted_iota(jnp.int32, (c, c), 0)
    ci = lax.broadcasted_iota(jnp.int32, (c, c), 1)
    tri_incl = (ri >= ci)[None]
    tri_strict = (ri > ci)[None]
    lw_all = lw_ref[0]
    lc_all = _split_dot(tri_ref[...], lw_all)

    for h in range(LANES // hd):
        sl = slice(h * hd, (h + 1) * hd)
        ch = lambda ref: ref[0][:, sl].reshape(nc, c, hd)
        r, k, v, kk, b = ch(r_ref), ch(k_ref), ch(v_ref), ch(kk_ref), ch(b_ref)
        lw = lw_all[:, sl].reshape(nc, c, hd)
        lc = lc_all[:, sl].reshape(nc, c, hd)
        tot = lc[:, c - 1:c, :]
        e_pos = jnp.exp(lc)
        e_neg = jnp.exp(-lc)
        e_end = jnp.exp(tot - lc)
        kk_t = kk * jnp.exp(lc - lw)
        r_t = r * e_pos
        b_t = b * e_neg
        k_t = k * e_neg
        a_ab = jnp.where(tri_strict, _bmm_nt(kk_t, b_t), 0.0)
        a_ak = jnp.where(tri_strict, _bmm_nt(kk_t, k_t), 0.0)
        a_rb = jnp.where(tri_incl, _bmm_nt(r_t, b_t), 0.0)
        a_rk = jnp.where(tri_incl, _bmm_nt(r_t, k_t), 0.0)
        tinv = _unit_lower_inverse(a_ab, c)
        wm = _bmm(tinv, kk_t)
        u0 = -_bmm(tinv, _bmm(a_ak, v))
        q_eff = r_t - _bmm(a_rb, wm)
        o0 = _bmm(a_rk, v) + _bmm(a_rb, u0)
        b_g = b * e_end
        k_g = k * e_end
        kr = lax.broadcasted_iota(jnp.int32, (hd, hd), 0)
        kc = lax.broadcasted_iota(jnp.int32, (hd, hd), 1)
        eye = (kr == kc).astype(F32)[None]
        m_mat = eye * jnp.exp(tot) - _bmm_tn(wm, b_g)
        d_mat = _bmm_tn(u0, b_g) + _bmm_tn(v, k_g)
        s = s_sc[h]
        states = []
        for i in range(nc):
            states.append(s)
            s = _dot(_bf(s), _bf(m_mat[i])) + d_mat[i]
        s_sc[h] = s
        o = o0 + _bmm_nt(q_eff, jnp.stack(states))
        o = o.reshape(tl, hd)
        mean = jnp.mean(o, axis=-1, keepdims=True)
        var = jnp.mean(jnp.square(o - mean), axis=-1, keepdims=True)
        o = (o - mean) * lax.rsqrt(var + GN_EPS) * lnw_ref[:, sl] + lnb_ref[:, sl]
        r2 = r_ref[0][:, sl]
        k2 = k_ref[0][:, sl]
        bonus = jnp.sum(r2 * k2 * rk_ref[:, sl], axis=-1, keepdims=True) * v_ref[0][:, sl]
        o_ref[0, :, sl] = (o + bonus) * g_ref[0][:, sl]

    @pl.when(l == pl.num_programs(2) - 1)
    def _():
        sout_ref[0] = s_sc[...]


def rwkv_scan(r, lw, k, v, kk, b, g, ln_w, ln_b, r_k, s0, tl, chunk):
    bsz, seq, _ = r.shape
    assert seq % tl == 0 and tl % chunk == 0
    hp = RW_WIDTH // LANES
    per = LANES // RW_HEAD
    seq_spec = pl.BlockSpec((1, tl, LANES), lambda bi, hi, li: (bi, li, hi))
    par_spec = pl.BlockSpec((1, LANES), lambda bi, hi, li: (0, hi))
    st_spec = pl.BlockSpec((1, per, RW_HEAD, RW_HEAD), lambda bi, hi, li: (bi, hi, 0, 0))
    row = lambda a: a.reshape(1, RW_WIDTH)
    idx = np.arange(tl)
    tri = jnp.asarray(((idx[:, None] >= idx[None, :]) & (idx[:, None] // chunk == idx[None, :] // chunk)),
                      dtype=BF16)
    return pl.pallas_call(
        functools.partial(_rwkv_scan_kernel, chunk=chunk),
        grid=(bsz, hp, seq // tl),
        in_specs=[seq_spec] * 7 + [par_spec] * 3 + [st_spec, pl.BlockSpec((tl, tl), lambda bi, hi, li: (0, 0))],
        out_specs=[seq_spec, st_spec],
        out_shape=[jax.ShapeDtypeStruct((bsz, seq, RW_WIDTH), F32),
                   jax.ShapeDtypeStruct((bsz, RW_HEADS, RW_HEAD, RW_HEAD), F32)],
        scratch_shapes=[pltpu.VMEM((per, RW_HEAD, RW_HEAD), F32)],
        compiler_params=_cparams("parallel", "parallel", "arbitrary"),
    )(r, lw, k, v, kk, b, g, row(ln_w), row(ln_b), row(r_k), s0, tri)


def _moba_prep_kernel(p_ref, qg_ref, kg_ref, grp_ref, grpt_ref, q_o, k_o, kb_o, vt_o, ks_o):
    p = p_ref[...]
    q = p[:, 0:MB_WIDTH]
    k = p[:, MB_WIDTH:2 * MB_WIDTH]
    v = p[:, 2 * MB_WIDTH:3 * MB_WIDTH]

    def head_norm(x, gain_ref):
        ms = _dot(x * x, grp_ref[...], HI) * (1.0 / MB_HEAD)
        inv = _dot(lax.rsqrt(ms + RMS_EPS), grpt_ref[...], HI)
        return x * inv * gain_ref[...]

    qn = head_norm(q, qg_ref)
    kn = head_norm(k, kg_ref)
    q_o[...] = qn
    k_o[...] = kn
    kb_o[...] = kn.astype(BF16)
    vt_o[...] = v.T.astype(BF16)
    ks_o[0] = jnp.sum(kn, axis=0, keepdims=True)


def moba_prep(p_mb, q_gain, k_gain, tm):
    t = p_mb.shape[0]
    assert t % tm == 0
    grp, grpt = _group_matrices(MB_WIDTH, MB_HEAD)
    tile = lambda i: (i, 0)
    full = lambda a: pl.BlockSpec(a.shape, lambda i: (0,) * a.ndim)
    qg = jnp.tile(q_gain, MB_HEADS).reshape(1, MB_WIDTH)
    kg = jnp.tile(k_gain, MB_HEADS).reshape(1, MB_WIDTH)
    return pl.pallas_call(
        _moba_prep_kernel,
        grid=(t // tm,),
        in_specs=[pl.BlockSpec((tm, 3 * MB_WIDTH), tile), full(qg), full(kg), full(grp), full(grpt)],
        out_specs=[pl.BlockSpec((tm, MB_WIDTH), tile)] * 3
                  + [pl.BlockSpec((MB_WIDTH, tm), lambda i: (0, i)),
                     pl.BlockSpec((1, 1, MB_WIDTH), lambda i: (i, 0, 0))],
        out_shape=[jax.ShapeDtypeStruct((t, MB_WIDTH), F32), jax.ShapeDtypeStruct((t, MB_WIDTH), F32),
                   jax.ShapeDtypeStruct((t, MB_WIDTH), BF16), jax.ShapeDtypeStruct((MB_WIDTH, t), BF16),
                   jax.ShapeDtypeStruct((t // tm, 1, MB_WIDTH), F32)],
        compiler_params=_cparams("parallel"),
    )(p_mb, qg, kg, grp, grpt)


def _top_blocks(gate, idx, axis):
    sel = jnp.zeros_like(gate)
    rem = gate
    for _ in range(MOBA_TOPK):
        mx = jnp.max(rem, axis=axis, keepdims=True)
        live = jnp.logical_and(rem == mx, rem > 0.5 * MASK_NEG)
        first = jnp.min(jnp.where(live, idx, 1e9), axis=axis, keepdims=True)
        pick = idx == first
        sel = jnp.where(pick, 1.0, sel)
        rem = jnp.where(pick, MASK_NEG, rem)
    return sel


def _moba_attn_kernel(q_ref, k_ref, vt_ref, ks_ref, o_ref, sel_sc):
    qi = pl.program_id(2)
    tq = q_ref.shape[1]
    nb = ks_ref.shape[1]
    n_heads = LANES // MB_HEAD
    q = q_ref[0]
    lane = lax.broadcasted_iota(jnp.int32, (1, LANES), 1)
    bmean = ks_ref[0] * (1.0 / MOBA_BLOCK)
    blk_f = lax.broadcasted_iota(jnp.int32, (nb, tq), 0).astype(F32)
    qi_f = qi.astype(F32)
    ri = lax.broadcasted_iota(jnp.int32, (tq, tq), 0)
    ci = lax.broadcasted_iota(jnp.int32, (tq, tq), 1)
    causal_t = ri <= ci
    row0 = pl.multiple_of(qi * tq, tq)
    k_d = k_ref[0, pl.ds(row0, tq), :]
    vt_d = vt_ref[:, pl.ds(row0, tq)]

    qs, carry = [], []
    for h in range(n_heads):
        q_h = jnp.where((lane // MB_HEAD) == h, q, 0.0)
        gate_t = _dot_nt(bmean, q_h, HI)
        gate_t = jnp.where(blk_f < qi_f, gate_t, MASK_NEG)
        sel_sc[h] = _top_blocks(gate_t, blk_f, 0)
        q_b = (q_h * (LOG2E * MB_HEAD ** -0.5)).astype(BF16)
        qs.append(q_b)
        s = jnp.where(causal_t, _dot_nt(k_d, q_b), MASK_NEG)
        m = jnp.max(s, axis=0, keepdims=True)
        p = jnp.exp2(s - m)
        carry += [m, jnp.sum(p, axis=0, keepdims=True),
                  _dot(vt_d[h * MB_HEAD:(h + 1) * MB_HEAD], p.astype(BF16))]

    last = jnp.maximum(qi - 1, 0)

    def body(j, carry):
        blocks = []
        for u in range(2):
            n = 2 * j + u
            nn = jnp.minimum(n, last)
            r0 = pl.multiple_of(nn * tq, tq)
            blocks.append((n < qi, nn, k_ref[0, pl.ds(r0, tq), :], vt_ref[:, pl.ds(r0, tq)]))
        out = []
        for h in range(n_heads):
            m, l, acc = carry[3 * h:3 * h + 3]
            ss = []
            m_new = m
            for valid, nn, k_n, _ in blocks:
                picked = jnp.logical_and(sel_sc[h, pl.ds(nn, 1), :] > 0.0, valid)
                s = jnp.where(picked, _dot_nt(k_n, qs[h]), MASK_NEG)
                ss.append(s)
                m_new = jnp.maximum(m_new, jnp.max(s, axis=0, keepdims=True))
            alpha = jnp.exp2(m - m_new)
            l = alpha * l
            acc = alpha * acc
            for s, (_, _, _, vt_n) in zip(ss, blocks):
                p = jnp.exp2(s - m_new)
                l = l + jnp.sum(p, axis=0, keepdims=True)
                acc = acc + _dot(vt_n[h * MB_HEAD:(h + 1) * MB_HEAD], p.astype(BF16))
            out += [m_new, l, acc]
        return tuple(out)

    carry = lax.fori_loop(0, (qi + 1) // 2, body, tuple(carry))
    o_t = jnp.concatenate([carry[3 * h + 2] / carry[3 * h + 1] for h in range(n_heads)], axis=0)
    o_ref[0] = o_t.T


def moba_attn(qn, kb, vt, ksum):
    bsz, seq, _ = qn.shape
    tq = MOBA_BLOCK
    nb = seq // tq
    hp = MB_WIDTH // LANES
    return pl.pallas_call(
        _moba_attn_kernel,
        grid=(bsz, hp, nb),
        in_specs=[pl.BlockSpec((1, tq, LANES), lambda b, h, i: (b, i, h)),
                  pl.BlockSpec((1, seq, LANES), lambda b, h, i: (b, 0, h)),
                  pl.BlockSpec((LANES, seq), lambda b, h, i: (h, b)),
                  pl.BlockSpec((1, nb, LANES), lambda b, h, i: (b, 0, h))],
        out_specs=pl.BlockSpec((1, tq, LANES), lambda b, h, i: (b, i, h)),
        out_shape=jax.ShapeDtypeStruct((bsz, seq, MB_WIDTH), F32),
        scratch_shapes=[pltpu.VMEM((LANES // MB_HEAD, nb, tq), F32)],
        compiler_params=_cparams("parallel", "parallel", "arbitrary"),
    )(qn, kb, vt, ksum)


def _moba_pages_kernel(pt_ref, q_ref, *refs, n_pg):
    k_refs = refs[:n_pg]
    v_refs = refs[n_pg:2 * n_pg]
    ks_o, m_o, l_o, acc_o = refs[2 * n_pg:]
    scale = MB_HEAD ** -0.5
    q = q_ref[0]
    for j in range(n_pg):
        k = k_refs[j][0, 0]
        v = v_refs[j][0, 0]
        ks_o[0, j] = jnp.sum(k, axis=0)
        s = jnp.sum(k * q[None], axis=-1, keepdims=True) * scale
        m = jnp.max(s, axis=0)
        p = jnp.exp(s - m[None])
        m_o[0, j] = jnp.broadcast_to(m, (MB_HEADS, MB_HEAD))
        l_o[0, j] = jnp.broadcast_to(jnp.sum(p, axis=0), (MB_HEADS, MB_HEAD))
        acc_o[0, j] = jnp.sum(p * v, axis=0)


def _moba_merge_kernel(q_ref, kn_ref, vn_ref, ks_ref, m_ref, l_ref, acc_ref, o_ref, *, ppb):
    scale = MB_HEAD ** -0.5
    q = q_ref[0]
    n_lp = ks_ref.shape[1]
    nblk = n_lp // ppb
    bsum = jnp.sum(ks_ref[0].reshape(nblk, ppb, MB_HEADS, MB_HEAD), axis=1)
    gate = jnp.sum(bsum * (1.0 / MOBA_BLOCK) * q[None], axis=-1, keepdims=True)
    idx = lax.broadcasted_iota(jnp.int32, gate.shape, 0).astype(F32)
    sel = _top_blocks(gate, idx, 0)
    sel_pg = jnp.broadcast_to(sel[:, None], (nblk, ppb, MB_HEADS, 1)).reshape(n_lp, MB_HEADS, 1) > 0.5
    s_own = jnp.sum(kn_ref[0] * q, axis=-1, keepdims=True) * scale
    m_pg = jnp.where(sel_pg, m_ref[0][:, :, 0:1], MASK_NEG)
    m_all = jnp.maximum(jnp.max(m_pg, axis=0), s_own)
    w_pg = jnp.where(sel_pg, jnp.exp(m_pg - m_all[None]), 0.0)
    w_own = jnp.exp(s_own - m_all)
    l_all = jnp.sum(w_pg * l_ref[0][:, :, 0:1], axis=0) + w_own
    acc = jnp.sum(w_pg * acc_ref[0], axis=0) + w_own * vn_ref[0]
    o_ref[0] = acc / l_all


def moba_decode(qn, kn, vn, k_cache, v_cache, page_table, layer):
    bsz = qn.shape[0]
    n_lp = page_table.shape[1]
    ppb = MOBA_BLOCK // PAGE_SIZE
    assert n_lp % ppb == 0 and n_lp // ppb >= MOBA_TOPK
    n_pg = 8
    assert n_lp % n_pg == 0
    heads = lambda a: a.reshape(bsz, MB_HEADS, MB_HEAD)
    hd_blk = (MB_HEADS, MB_HEAD)

    def page_spec(j):
        return pl.BlockSpec((1, 1, PAGE_SIZE) + hd_blk,
                            lambda b, g, pt: (layer, pt[b, g * n_pg + j], 0, 0, 0))

    part = pl.BlockSpec((1, n_pg) + hd_blk, lambda b, g, pt: (b, g, 0, 0))
    part_shape = jax.ShapeDtypeStruct((bsz, n_lp) + hd_blk, F32)
    ks, m, l, acc = pl.pallas_call(
        functools.partial(_moba_pages_kernel, n_pg=n_pg),
        grid_spec=pltpu.PrefetchScalarGridSpec(
            num_scalar_prefetch=1,
            grid=(bsz, n_lp // n_pg),
            in_specs=[pl.BlockSpec((1,) + hd_blk, lambda b, g, pt: (b, 0, 0))]
                     + [page_spec(j) for j in range(n_pg)] * 2,
            out_specs=[part] * 4),
        out_shape=[part_shape] * 4,
        compiler_params=_cparams("parallel", "arbitrary"),
    )(page_table, heads(qn), *([k_cache] * n_pg), *([v_cache] * n_pg))

    vec = pl.BlockSpec((1,) + hd_blk, lambda b: (b, 0, 0))
    pg = pl.BlockSpec((1, n_lp) + hd_blk, lambda b: (b, 0, 0, 0))
    out = pl.pallas_call(
        functools.partial(_moba_merge_kernel, ppb=ppb),
        grid=(bsz,),
        in_specs=[vec, vec, vec, pg, pg, pg, pg],
        out_specs=vec,
        out_shape=jax.ShapeDtypeStruct((bsz,) + hd_blk, F32),
        compiler_params=_cparams("parallel"),
    )(heads(qn), heads(kn), heads(vn), ks, m, l, acc)
    return out.reshape(bsz, MB_WIDTH)


def _hgrn_constants(c):
    rows = np.arange(c)
    dmats, ups, los, masks = [], [], [], []
    z = c
    while z >= 2:
        seg = rows // z
        mid = seg * z + z // 2
        up = rows >= mid
        d = np.zeros((c, c), np.float32)
        for i in rows:
            if up[i]:
                d[i, mid[i]:i + 1] = 1.0
            else:
                d[i, i + 1:mid[i]] = 1.0
        dmats.append(d)
        ups.append(np.repeat(up[:, None], HG_D, 1).astype(np.float32))
        los.append(np.repeat(~up[:, None], HG_D, 1).astype(np.float32))
        masks.append(((seg[:, None] == seg[None, :]) & up[:, None] & (~up)[None, :]).astype(np.float32))
        z //= 2
    tri = np.tril(np.ones((c, c), np.float32))
    dmats += [tri, 1.0 - tri]
    return (jnp.asarray(np.concatenate(dmats, 0), dtype=BF16), jnp.asarray(np.stack(ups)),
            jnp.asarray(np.stack(los)), jnp.asarray(np.stack(masks)))


def _hgrn_kernel(pq_ref, pf_ref, pi_ref, pg_ref, lbl_ref, gain_ref, s0_ref, dd_ref, up_ref, lo_ref,
                 mk_ref, o_ref, sout_ref, st_sc, *, chunk, layer):
    l = pl.program_id(2)
    tl = pq_ref.shape[1]
    c = chunk
    nc = tl // c
    nlev = up_ref.shape[0]
    er = lax.broadcasted_iota(jnp.int32, (HG_D, HG_D), 0)
    ec = lax.broadcasted_iota(jnp.int32, (HG_D, HG_D), 1)
    eye_d = (er == ec).astype(F32)

    @pl.when(l == 0)
    def _():
        st_sc[...] = _dot_nt(eye_d, s0_ref[0, 0], HI)

    logits = lbl_ref[...]
    ex = jnp.exp(logits - jnp.max(logits, axis=0, keepdims=True))
    prob = ex / jnp.sum(ex, axis=0, keepdims=True)
    lb = jnp.sum(prob[0:layer + 1], axis=0, keepdims=True) - prob[0:1]

    qr = pq_ref[0]
    q = qr * _sigmoid(qr)
    forget = lb + (1.0 - lb) * _sigmoid(pf_ref[0])
    logf = jnp.log(jnp.maximum(forget, 1e-30))
    kg = 1.0 - forget
    v = pi_ref[0]

    ri = lax.broadcasted_iota(jnp.int32, (c, c), 0)
    ci = lax.broadcasted_iota(jnp.int32, (c, c), 1)
    diag = ri == ci
    st = st_sc[...]
    outs = []
    for i in range(nc):
        rs = slice(i * c, (i + 1) * c)
        qc, kc, vc = q[rs], kg[rs], v[rs]
        e = jnp.exp(_split_dot(dd_ref[...], logf[rs]))
        a = jnp.where(diag, jnp.sum(qc * kc, axis=1, keepdims=True), 0.0)
        for lev in range(nlev):
            el = e[lev * c:(lev + 1) * c]
            a = a + mk_ref[lev] * _dot_nt(_bf(qc * el * up_ref[lev]), _bf(kc * el * lo_ref[lev]))
        e_in = e[nlev * c:(nlev + 1) * c]
        e_out = e[(nlev + 1) * c:(nlev + 2) * c]
        outs.append(_dot_nt(_bf(qc * e_in), _bf(st)) + _dot(_bf(a), _bf(vc)))
        st = st * e_in[c - 1:c, :] + _dot_tn(_bf(vc), _bf(kc * e_out))
    st_sc[...] = st
    o = jnp.concatenate(outs, axis=0) if nc > 1 else outs[0]
    ms = jnp.mean(o * o, axis=-1, keepdims=True)
    gr = pg_ref[0]
    o_ref[0] = o * lax.rsqrt(ms + RMS_EPS) * gain_ref[...] * (gr * _sigmoid(gr))

    @pl.when(l == pl.num_programs(2) - 1)
    def _():
        sout_ref[0, 0] = _dot_nt(eye_d, st, HI)


def hgrn_scan(p, lb_logits, out_gain, s0, layer, tl, chunk):
    bsz, seq, _ = p.shape
    assert seq % tl == 0 and tl % chunk == 0
    dd, up, lo, mk = _hgrn_constants(chunk)
    col = lambda off: pl.BlockSpec((1, tl, HG_D), lambda b, h, li: (b, li, off + h))
    full = lambda a: pl.BlockSpec(a.shape, lambda b, h, li: (0,) * a.ndim)
    st_spec = pl.BlockSpec((1, 1, HG_D, HG_D), lambda b, h, li: (b, h, 0, 0))
    return pl.pallas_call(
        functools.partial(_hgrn_kernel, chunk=chunk, layer=layer),
        grid=(bsz, HG_HEADS, seq // tl),
        in_specs=[col(0), col(HG_HEADS), col(2 * HG_HEADS), col(3 * HG_HEADS),
                  pl.BlockSpec((lb_logits.shape[0], HG_D), lambda b, h, li: (0, h)),
                  pl.BlockSpec((1, HG_D), lambda b, h, li: (0, 0)),
                  st_spec, full(dd), full(up), full(lo), full(mk)],
        out_specs=[pl.BlockSpec((1, tl, HG_D), lambda b, h, li: (b, li, h)), st_spec],
        out_shape=[jax.ShapeDtypeStruct((bsz, seq, D_MODEL), F32),
                   jax.ShapeDtypeStruct((bsz, HG_HEADS, HG_D, HG_D), F32)],
        scratch_shapes=[pltpu.VMEM((HG_D, HG_D), F32)],
        compiler_params=_cparams("parallel", "parallel", "arbitrary"),
    )(p, p, p, p, lb_logits, out_gain.reshape(1, HG_D), s0, dd, up, lo, mk)


def _row_to_col(row, eye):
    return jnp.sum(jnp.where(eye, row, 0.0), axis=1, keepdims=True)


def _col_to_row(col, eye):
    return jnp.sum(jnp.where(eye, col, 0.0), axis=0, keepdims=True)


def _rwkv_step_kernel(r_ref, lw_ref, k_ref, v_ref, kk_ref, b_ref, g_ref, lnw_ref, lnb_ref, rk_ref,
                      s0_ref, o_ref, sout_ref):
    hd = RW_HEAD
    ri = lax.broadcasted_iota(jnp.int32, (hd, hd), 0)
    ci = lax.broadcasted_iota(jnp.int32, (hd, hd), 1)
    eye = ri == ci
    for h in range(RW_HEADS):
        sl = slice(h * hd, (h + 1) * hd)
        row = lambda ref: ref[0][:, sl]
        r, k, v, kk, b = row(r_ref), row(k_ref), row(v_ref), row(kk_ref), row(b_ref)
        s = s0_ref[0, h]
        s_kk = jnp.sum(s * kk, axis=1, keepdims=True)
        s = s * jnp.exp(row(lw_ref)) - s_kk * b + _row_to_col(v, eye) * k
        sout_ref[0, h] = s
        o = _col_to_row(jnp.sum(s * r, axis=1, keepdims=True), eye)
        mean = jnp.mean(o, axis=-1, keepdims=True)
        var = jnp.mean(jnp.square(o - mean), axis=-1, keepdims=True)
        o = (o - mean) * lax.rsqrt(var + GN_EPS) * lnw_ref[:, sl] + lnb_ref[:, sl]
        bonus = jnp.sum(r * k * rk_ref[:, sl], axis=-1, keepdims=True) * v
        o_ref[0, :, sl] = (o + bonus) * row(g_ref)


def rwkv_step(r, lw, k, v, kk, b, g, ln_w, ln_b, r_k, s0):
    bsz = r.shape[0]
    seq_spec = pl.BlockSpec((1, 1, RW_WIDTH), lambda i: (i, 0, 0))
    par_spec = pl.BlockSpec((1, RW_WIDTH), lambda i: (0, 0))
    st_spec = pl.BlockSpec((1, RW_HEADS, RW_HEAD, RW_HEAD), lambda i: (i, 0, 0, 0))
    row = lambda a: a.reshape(1, RW_WIDTH)
    tok = lambda a: a.reshape(bsz, 1, RW_WIDTH)
    o, s = pl.pallas_call(
        _rwkv_step_kernel,
        grid=(bsz,),
        in_specs=[seq_spec] * 7 + [par_spec] * 3 + [st_spec],
        out_specs=[seq_spec, st_spec],
        out_shape=[jax.ShapeDtypeStruct((bsz, 1, RW_WIDTH), F32),
                   jax.ShapeDtypeStruct((bsz, RW_HEADS, RW_HEAD, RW_HEAD), F32)],
        compiler_params=_cparams("parallel"),
    )(*[tok(a) for a in (r, lw, k, v, kk, b, g)], row(ln_w), row(ln_b), row(r_k), s0)
    return o.reshape(bsz, RW_WIDTH), s


def _hgrn_step_kernel(p_ref, lbl_ref, gain_ref, s0_ref, o_ref, sout_ref, *, layer):
    d = HG_D
    ri = lax.broadcasted_iota(jnp.int32, (d, d), 0)
    ci = lax.broadcasted_iota(jnp.int32, (d, d), 1)
    eye = ri == ci
    logits = lbl_ref[...]
    ex = jnp.exp(logits - jnp.max(logits, axis=0, keepdims=True))
    prob = ex / jnp.sum(ex, axis=0, keepdims=True)
    lb_all = jnp.sum(prob[0:layer + 1], axis=0, keepdims=True) - prob[0:1]
    width = HG_HEADS * d
    for h in range(HG_HEADS):
        col = lambda part: p_ref[0][:, part * width + h * d:part * width + (h + 1) * d]
        qr, fr, v, gr = col(0), col(1), col(2), col(3)
        lb = lb_all[:, h * d:(h + 1) * d]
        q = qr * _sigmoid(qr)
        forget = lb + (1.0 - lb) * _sigmoid(fr)
        decay = jnp.maximum(forget, 1e-30)
        s = _row_to_col(decay, eye) * s0_ref[0, h] + _row_to_col(1.0 - forget, eye) * v
        sout_ref[0, h] = s
        o = jnp.sum(_row_to_col(q, eye) * s, axis=0, keepdims=True)
        ms = jnp.mean(o * o, axis=-1, keepdims=True)
        o_ref[0, :, h * d:(h + 1) * d] = o * lax.rsqrt(ms + RMS_EPS) * gain_ref[...] * (gr * _sigmoid(gr))


def hgrn_step(p, lb_logits, out_gain, s0, layer):
    bsz = p.shape[0]
    st_spec = pl.BlockSpec((1, HG_HEADS, HG_D, HG_D), lambda i: (i, 0, 0, 0))
    o, s = pl.pallas_call(
        functools.partial(_hgrn_step_kernel, layer=layer),
        grid=(bsz,),
        in_specs=[pl.BlockSpec((1, 1, p.shape[1]), lambda i: (i, 0, 0)),
                  pl.BlockSpec(lb_logits.shape, lambda i: (0, 0)),
                  pl.BlockSpec((1, HG_D), lambda i: (0, 0)), st_spec],
        out_specs=[pl.BlockSpec((1, 1, D_MODEL), lambda i: (i, 0, 0)), st_spec],
        out_shape=[jax.ShapeDtypeStruct((bsz, 1, D_MODEL), F32),
                   jax.ShapeDtypeStruct((bsz, HG_HEADS, HG_D, HG_D), F32)],
        compiler_params=_cparams("parallel"),
    )(p.reshape(bsz, 1, -1), lb_logits, out_gain.reshape(1, HG_D), s0)
    return o.reshape(bsz, D_MODEL), s


def _even_layer(xp, xs, il, vf_p, vf_s, w, cache_k, cache_v, page_table, state_rwkv, state_shift):
    nb_p, seq, d = xp.shape
    nb_s = xs.shape[0]
    tp = nb_p * seq
    rw_params = (w['rw_mu'][il], w['rw_w0'][il], w['w2p'][il], w['rw_a0'][il], w['a2p'][il],
                 w['rw_g2'][il], w['rw_k_k'][il], w['rw_k_a'][il])
    has_vres = il > 0
    vres_w = (w['rw_v0'][il - 1], w['rw_v1'][il - 1], w['rw_v2'][il - 1]) if has_vres else None
    ln = (w['rw_ln_w'][il], w['rw_ln_b'][il], w['rw_r_k'][il].reshape(-1))

    p_rw, p_mb = norm_matmul(xp.reshape(tp, d), w['norm_mix'][2 * il], w['w_in_even'][il],
                             (RW_IN, 3 * MB_WIDTH), 512)
    vres = (vf_p,) + vres_w if has_vres else None
    prep = rwkv_prep(p_rw, None, rw_params, vres, True, seq, 512)
    if not has_vres:
        vf_p = prep[7]
    seq3 = lambda a: a.reshape(nb_p, seq, -1)
    o_rw, wkv_p = rwkv_scan(*[seq3(a) for a in prep[:7]], *ln,
                            jnp.zeros((nb_p, RW_HEADS, RW_HEAD, RW_HEAD), F32), 512, RW_CHUNK)
    qn, kn, kb, vt, ksum = moba_prep(p_mb, w['mb_q_norm'][il], w['mb_k_norm'][il], MOBA_BLOCK)
    o_mb = moba_attn(seq3(qn), seq3(kb), vt, ksum.reshape(nb_p, seq // MOBA_BLOCK, MB_WIDTH))
    lyr = 2 * il
    yp = mix_ffn(xp.reshape(tp, d), [o_rw.reshape(tp, RW_WIDTH), o_mb.reshape(tp, MB_WIDTH)],
                 [w['w_out_even'][il][:RW_WIDTH], w['w_out_even'][il][RW_WIDTH:]],
                 w['norm_ffn'][lyr], w['ffn_w_gate'][lyr], w['ffn_w_up'][lyr], w['ffn_w_down'][lyr],
                 512, D_FF // 2).reshape(nb_p, seq, d)
    mk_p = kn.reshape(nb_p, seq, MB_HEADS, MB_HEAD)
    mv_p = p_mb[:, 2 * MB_WIDTH:].reshape(nb_p, seq, MB_HEADS, MB_HEAD)
    sh_p = p_rw.reshape(nb_p, seq, RW_IN)[:, -1]

    ps_rw, ps_mb = norm_matmul(xs.reshape(nb_s, d), w['norm_mix'][2 * il], w['w_in_even'][il],
                               (RW_IN, 3 * MB_WIDTH), nb_s)
    vres = (vf_s,) + vres_w if has_vres else None
    prep = rwkv_prep(ps_rw, state_shift[il], rw_params, vres, False, 1, nb_s)
    if not has_vres:
        vf_s = prep[7]
    os_rw, wkv_s = rwkv_step(*prep[:7], *ln, state_rwkv[il])
    qs, ks, _, _, _ = moba_prep(ps_mb, w['mb_q_norm'][il], w['mb_k_norm'][il], nb_s)
    vs = ps_mb[:, 2 * MB_WIDTH:]
    os_mb = moba_decode(qs, ks, vs, cache_k, cache_v, page_table, il)
    ys = mix_ffn(xs.reshape(nb_s, d), [os_rw, os_mb],
                 [w['w_out_even'][il][:RW_WIDTH], w['w_out_even'][il][RW_WIDTH:]],
                 w['norm_ffn'][lyr], w['ffn_w_gate'][lyr], w['ffn_w_up'][lyr], w['ffn_w_down'][lyr],
                 nb_s, D_FF // 2).reshape(nb_s, 1, d)
    mk_s = ks.reshape(nb_s, 1, MB_HEADS, MB_HEAD)
    mv_s = vs.reshape(nb_s, 1, MB_HEADS, MB_HEAD)
    return yp, ys, vf_p, vf_s, (mk_p, mv_p, mk_s, mv_s, wkv_p, wkv_s, sh_p, ps_rw)


def _odd_layer(xp, xs, il, w, state_hgrn):
    nb_p, seq, d = xp.shape
    nb_s = xs.shape[0]
    tp = nb_p * seq
    lyr = 2 * il + 1
    (pp,) = norm_matmul(xp.reshape(tp, d), w['norm_mix'][lyr], w['w_in_odd'][il], (4 * d,), 512)
    o_p, hg_p = hgrn_scan(pp.reshape(nb_p, seq, 4 * d), w['hg_lb_logits'], w['hg_out_norm'][il],
                          jnp.zeros((nb_p, HG_HEADS, HG_D, HG_D), F32), il, 256, HG_CHUNK)
    yp = mix_ffn(xp.reshape(tp, d), [o_p.reshape(tp, d)], [w['w_out_odd'][il]],
                 w['norm_ffn'][lyr], w['ffn_w_gate'][lyr], w['ffn_w_up'][lyr], w['ffn_w_down'][lyr],
                 512, D_FF // 2).reshape(nb_p, seq, d)
    (ps,) = norm_matmul(xs.reshape(nb_s, d), w['norm_mix'][lyr], w['w_in_odd'][il], (4 * d,), nb_s)
    o_s, hg_s = hgrn_step(ps, w['hg_lb_logits'], w['hg_out_norm'][il], state_hgrn[il], il)
    ys = mix_ffn(xs.reshape(nb_s, d), [o_s], [w['w_out_odd'][il]],
                 w['norm_ffn'][lyr], w['ffn_w_gate'][lyr], w['ffn_w_up'][lyr], w['ffn_w_down'][lyr],
                 nb_s, D_FF // 2).reshape(nb_s, 1, d)
    return yp, ys, hg_p, hg_s


def kernel(x_prompt, x_sample, cache_moba_k, cache_moba_v, page_table, state_rwkv, state_rwkv_shift, state_hgrn, norm_mix, norm_ffn, w_in_even, w_out_even, rw_mu, rw_w0, rw_w2, rw_a0, rw_a2, rw_g2, rw_k_k, rw_k_a, rw_r_k, rw_ln_w, rw_ln_b, rw_v0, rw_v1, rw_v2, mb_q_norm, mb_k_norm, w_in_odd, w_out_odd, hg_lb_logits, hg_out_norm, ffn_w_gate, ffn_w_up, ffn_w_down):
    depth = norm_mix.shape[0]
    zeros_lora = jnp.zeros_like(rw_w2)
    w = dict(
        norm_mix=norm_mix, norm_ffn=norm_ffn,
        w_in_even=w_in_even.astype(BF16), w_out_even=w_out_even.astype(BF16),
        w_in_odd=w_in_odd.astype(BF16), w_out_odd=w_out_odd.astype(BF16),
        ffn_w_gate=ffn_w_gate.astype(BF16), ffn_w_up=ffn_w_up.astype(BF16),
        ffn_w_down=ffn_w_down.astype(BF16),
        rw_mu=rw_mu, rw_w0=rw_w0, rw_a0=rw_a0, rw_g2=rw_g2, rw_k_k=rw_k_k, rw_k_a=rw_k_a,
        rw_r_k=rw_r_k, rw_ln_w=rw_ln_w, rw_ln_b=rw_ln_b, rw_v0=rw_v0, rw_v1=rw_v1, rw_v2=rw_v2,
        w2p=jnp.concatenate([rw_w2, zeros_lora], axis=1), a2p=jnp.concatenate([zeros_lora, rw_a2], axis=1),
        mb_q_norm=mb_q_norm, mb_k_norm=mb_k_norm, hg_lb_logits=hg_lb_logits, hg_out_norm=hg_out_norm)
    page_table = page_table.astype(jnp.int32)

    xp, xs = x_prompt, x_sample
    vf_p = vf_s = None
    even_out, hg_out = [], []
    for layer in range(depth):
        il = layer // 2
        if layer % 2 == 0:
            xp, xs, vf_p, vf_s, outs = _even_layer(xp, xs, il, vf_p, vf_s, w, cache_moba_k, cache_moba_v,
                                                   page_table, state_rwkv, state_rwkv_shift)
            even_out.append(outs)
        else:
            xp, xs, hg_p, hg_s = _odd_layer(xp, xs, il, w, state_hgrn)
            hg_out.append((hg_p, hg_s))
    stack = lambda i: jnp.stack([o[i] for o in even_out])
    return (xp, xs, stack(0), stack(1), stack(2), stack(3), stack(4), stack(5), stack(6), stack(7),
            jnp.stack([o[0] for o in hg_out]), jnp.stack([o[1] for o in hg_out]))
```

```python
import functools

import numpy as np
import jax
import jax.numpy as jnp
from jax import lax
from jax.experimental import pallas as pl
from jax.experimental.pallas import tpu as pltpu

F32 = jnp.float32
BF16 = jnp.bfloat16
HI = lax.Precision.HIGHEST

D_MODEL = 1024
PAGE_SIZE = 128
RW_HEAD = 64
RW_WIDTH = 512
RW_HEADS = 8
RW_IN = 1792
MB_HEAD = 64
MB_WIDTH = 512
MB_HEADS = 8
MOBA_BLOCK = 256
MOBA_TOPK = 3
HG_HEADS = 8
HG_D = 128
D_FF = 2816
RMS_EPS = 1e-6
GN_EPS = 64e-5
MASK_NEG = -1e30
LOG2E = 1.4426950408889634

LANES = 128
VMEM_LIMIT = 56 * 1024 * 1024
RW_CHUNK = 64
HG_CHUNK = 64


def _cparams(*sem):
    return pltpu.CompilerParams(dimension_semantics=sem, vmem_limit_bytes=VMEM_LIMIT)


def _bf(x):
    return x.astype(BF16)


def _dot(a, b, precision=None):
    return jnp.dot(a, b, precision=precision, preferred_element_type=F32)


def _dot_nt(a, b, precision=None):
    return lax.dot_general(a, b, (((1,), (1,)), ((), ())), precision=precision,
                           preferred_element_type=F32)


def _dot_tn(a, b, precision=None):
    return lax.dot_general(a, b, (((0,), (0,)), ((), ())), precision=precision,
                           preferred_element_type=F32)


def _bmm(a, b):
    return lax.dot_general(_bf(a), _bf(b), (((2,), (1,)), ((0,), (0,))), preferred_element_type=F32)


def _bmm_nt(a, b):
    return lax.dot_general(_bf(a), _bf(b), (((2,), (2,)), ((0,), (0,))), preferred_element_type=F32)


def _bmm_tn(a, b):
    return lax.dot_general(_bf(a), _bf(b), (((1,), (1,)), ((0,), (0,))), preferred_element_type=F32)


def _split_dot(a_bf16, b):
    n = b.shape[1]
    hi = b.astype(BF16)
    lo = (b - hi.astype(F32)).astype(BF16)
    out = _dot(a_bf16, jnp.concatenate([hi, lo], axis=1))
    return out[:, :n] + out[:, n:]


def _sigmoid(x):
    return 1.0 / (1.0 + jnp.exp(-x))


def _group_matrices(width, group):
    g = np.zeros((width, LANES), np.float32)
    g[np.arange(width), np.arange(width) // group] = 1.0
    return jnp.asarray(g), jnp.asarray(g.T.copy())


def _norm_matmul_kernel(x_ref, gain_ref, w_ref, *out_refs, widths, tn):
    x = x_ref[...]
    ms = jnp.mean(x * x, axis=-1, keepdims=True)
    h = (x * lax.rsqrt(ms + RMS_EPS) * gain_ref[...]).astype(BF16)
    col = 0
    for o_ref, width in zip(out_refs, widths):
        for c0 in range(0, width, tn):
            sz = min(tn, width - c0)
            o_ref[:, c0:c0 + sz] = _dot(h, w_ref[:, col + c0:col + c0 + sz])
        col += width


def norm_matmul(x, gain, w_bf16, widths, tm):
    m, d = x.shape
    n = w_bf16.shape[1]
    assert sum(widths) == n and m % tm == 0
    kern = functools.partial(_norm_matmul_kernel, widths=tuple(widths), tn=512)
    return pl.pallas_call(
        kern,
        grid=(m // tm,),
        in_specs=[pl.BlockSpec((tm, d), lambda i: (i, 0)),
                  pl.BlockSpec((1, d), lambda i: (0, 0)),
                  pl.BlockSpec((d, n), lambda i: (0, 0))],
        out_specs=[pl.BlockSpec((tm, wd), lambda i: (i, 0)) for wd in widths],
        out_shape=[jax.ShapeDtypeStruct((m, wd), F32) for wd in widths],
        compiler_params=_cparams("parallel"),
    )(x, gain.reshape(1, d), w_bf16)


def _mix_ffn_kernel(*refs, n_act):
    x_ref = refs[0]
    a_refs = refs[1:1 + n_act]
    wo_refs = refs[1 + n_act:1 + 2 * n_act]
    gain_ref, wg_ref, wu_ref, wd_ref, y_ref, x1_sc, h_sc, acc_sc = refs[1 + 2 * n_act:]
    j = pl.program_id(1)

    @pl.when(j == 0)
    def _():
        x1 = x_ref[...]
        for a_ref, wo_ref in zip(a_refs, wo_refs):
            x1 = x1 + _dot(a_ref[...].astype(BF16), wo_ref[...])
        x1_sc[...] = x1
        ms = jnp.mean(x1 * x1, axis=-1, keepdims=True)
        h_sc[...] = (x1 * lax.rsqrt(ms + RMS_EPS) * gain_ref[...]).astype(BF16)
        acc_sc[...] = jnp.zeros_like(acc_sc)

    h = h_sc[...]
    g = _dot(h, wg_ref[...])
    u = _dot(h, wu_ref[...])
    act = (g * _sigmoid(g) * u).astype(BF16)
    acc_sc[...] += _dot(act, wd_ref[...])

    @pl.when(j == pl.num_programs(1) - 1)
    def _():
        y_ref[...] = x1_sc[...] + acc_sc[...]


def mix_ffn(x, acts, wos_bf16, gain, wg, wu, wd, tm, tf):
    m, d = x.shape
    f = wg.shape[1]
    assert m % tm == 0 and f % tf == 0
    n_act = len(acts)
    in_specs = [pl.BlockSpec((tm, d), lambda i, j: (i, 0))]
    in_specs += [pl.BlockSpec((tm, a.shape[1]), lambda i, j: (i, 0)) for a in acts]
    in_specs += [pl.BlockSpec(w.shape, lambda i, j: (0, 0)) for w in wos_bf16]
    in_specs += [pl.BlockSpec((1, d), lambda i, j: (0, 0)),
                 pl.BlockSpec((d, tf), lambda i, j: (0, j)),
                 pl.BlockSpec((d, tf), lambda i, j: (0, j)),
                 pl.BlockSpec((tf, d), lambda i, j: (j, 0))]
    return pl.pallas_call(
        functools.partial(_mix_ffn_kernel, n_act=n_act),
        grid=(m // tm, f // tf),
        in_specs=in_specs,
        out_specs=pl.BlockSpec((tm, d), lambda i, j: (i, 0)),
        out_shape=jax.ShapeDtypeStruct((m, d), F32),
        scratch_shapes=[pltpu.VMEM((tm, d), F32), pltpu.VMEM((tm, d), BF16), pltpu.VMEM((tm, d), F32)],
        compiler_params=_cparams("parallel", "arbitrary"),
    )(x, *acts, *wos_bf16, gain.reshape(1, d), wg, wu, wd)


def _rwkv_prep_kernel(*refs, shift_mode, has_vres, tiles_per_seq):
    it = iter(refs)
    p_ref = next(it)
    prev_ref = next(it)
    mu_ref, w0_ref, a0_ref, kk_ref, ka_ref, w2_ref, a2_ref, g2_ref, grp_ref, grpt_ref = (
        next(it) for _ in range(10))
    if has_vres:
        vf_ref, v0_ref, v1_ref, v2_ref = (next(it) for _ in range(4))
    r_o, lw_o, k_o, v_o, kkn_o, b_o, g_o = (next(it) for _ in range(7))
    if not has_vres:
        vf_o = next(it)

    pf = p_ref[...]
    if shift_mode:
        first = (pl.program_id(0) % tiles_per_seq) == 0
        prev_row = jnp.where(first, 0.0, prev_ref[7:8, :])
        rows = lax.broadcasted_iota(jnp.int32, pf.shape, 0)
        prev = jnp.where(rows == 0, prev_row, pltpu.roll(pf, 1, axis=0))
    else:
        prev = prev_ref[...]
    xs = pf + (prev - pf) * mu_ref[...]
    r = xs[:, 0:RW_WIDTH]
    k = xs[:, RW_WIDTH:2 * RW_WIDTH]
    v = xs[:, 2 * RW_WIDTH:3 * RW_WIDTH]
    wa_lo = xs[:, 3 * RW_WIDTH:3 * RW_WIDTH + LANES]
    g_lo = xs[:, 3 * RW_WIDTH + LANES:RW_IN]
    z = w0_ref[...] + _dot(jnp.tanh(wa_lo), w2_ref[...], HI)
    sp = jnp.maximum(-z, 0.0) + jnp.log(1.0 + jnp.exp(-jnp.abs(z)))
    logw = -jnp.exp(-sp - 0.5)
    a = _sigmoid(a0_ref[...] + _dot(wa_lo, a2_ref[...], HI))
    g = _dot(_sigmoid(g_lo), g2_ref[...], HI)
    if has_vres:
        gate = _sigmoid(v0_ref[...] + _dot(_dot(v, v1_ref[...], HI), v2_ref[...], HI))
        v = v + (vf_ref[...] - v) * gate
    else:
        vf_o[...] = v
    kk = k * kk_ref[...]
    ss = _dot(kk * kk, grp_ref[...], HI)
    inv = _dot(lax.rsqrt(jnp.maximum(ss, 1e-24)), grpt_ref[...], HI)
    kk = kk * inv
    k = k * (1.0 + (a - 1.0) * ka_ref[...])
    r_o[...] = r
    lw_o[...] = logw
    k_o[...] = k
    v_o[...] = v
    kkn_o[...] = kk
    b_o[...] = kk * a
    g_o[...] = g


def rwkv_prep(p_rw, prev, params, vres, shift_mode, seq_len, tm):
    t = p_rw.shape[0]
    assert t % tm == 0
    mu, w0, w2p, a0, a2p, g2, k_k, k_a = params
    grp, grpt = _group_matrices(RW_WIDTH, RW_HEAD)
    row = lambda a: a.reshape(1, -1)
    tile = lambda wd: pl.BlockSpec((tm, wd), lambda i: (i, 0))
    full = lambda a: pl.BlockSpec(a.shape, lambda i: (0,) * a.ndim)
    args = [p_rw]
    specs = [tile(RW_IN)]
    if shift_mode:
        r8 = tm // 8
        args.append(p_rw)
        specs.append(pl.BlockSpec((8, RW_IN), lambda i: (jnp.maximum(i * r8 - 1, 0), 0)))
    else:
        args.append(prev)
        specs.append(tile(RW_IN))
    small = [row(mu), row(w0), row(a0), row(k_k), row(k_a), w2p, a2p, g2, grp, grpt]
    args += small
    specs += [full(a) for a in small]
    has_vres = vres is not None
    if has_vres:
        vf, v0, v1, v2 = vres
        extra = [row(v0), v1, v2]
        args += [vf] + extra
        specs += [tile(RW_WIDTH)] + [full(a) for a in extra]
    n_out = 7 if has_vres else 8
    kern = functools.partial(_rwkv_prep_kernel, shift_mode=shift_mode, has_vres=has_vres,
                             tiles_per_seq=max(seq_len // tm, 1))
    outs = pl.pallas_call(
        kern,
        grid=(t // tm,),
        in_specs=specs,
        out_specs=[tile(RW_WIDTH)] * n_out,
        out_shape=[jax.ShapeDtypeStruct((t, RW_WIDTH), F32)] * n_out,
        compiler_params=_cparams("parallel"),
    )(*args)
    return outs


def _unit_lower_inverse(n, c):
    ri = lax.broadcasted_iota(jnp.int32, (c, c), 0)
    ci = lax.broadcasted_iota(jnp.int32, (c, c), 1)
    eye = (ri == ci).astype(F32)[None]
    blk = min(16, c)
    same = ((ri // blk) == (ci // blk))[None]
    nd = jnp.where(same, n, 0.0)
    x = eye - nd
    pw = nd
    span = 2
    while span < blk:
        pw = _bmm(pw, pw)
        x = _bmm(x, eye + pw)
        span *= 2
    if c > blk:
        p = _bmm(x, jnp.where(same, 0.0, n))
        y = eye - p
        pw = p
        span = 2
        while span < c // blk:
            pw = _bmm(pw, pw)
            y = _bmm(y, eye + pw)
            span *= 2
        x = _bmm(y, x)
    return x


def _rwkv_scan_kernel(r_ref, lw_ref, k_ref, v_ref, kk_ref, b_ref, g_ref, lnw_ref, lnb_ref, rk_ref,
                      s0_ref, tri_ref, o_ref, sout_ref, s_sc, *, chunk):
    l = pl.program_id(2)
    tl = r_ref.shape[1]
    c = chunk
    nc = tl // c
    hd = RW_HEAD

    @pl.when(l == 0)
    def _():
        s_sc[...] = s0_ref[0]

    ri = lax.broadcasted_iota(jnp.int32, (c, c), 0)
    ci = lax.broadcasted_iota(jnp.int32, (c, c), 1)
    tri_incl = (ri >= ci)[None]
    tri_strict = (ri > ci)[None]
    lw_all = lw_ref[0]
    lc_all = _split_dot(tri_ref[...], lw_all)

    for h in range(LANES // hd):
        sl = slice(h * hd, (h + 1) * hd)
        ch = lambda ref: ref[0][:, sl].reshape(nc, c, hd)
        r, k, v, kk, b = ch(r_ref), ch(k_ref), ch(v_ref), ch(kk_ref), ch(b_ref)
        lw = lw_all[:, sl].reshape(nc, c, hd)
        lc = lc_all[:, sl].reshape(nc, c, hd)
        tot = lc[:, c - 1:c, :]
        e_pos = jnp.exp(lc)
        e_neg = jnp.exp(-lc)
        e_end = jnp.exp(tot - lc)
        kk_t = kk * jnp.exp(lc - lw)
        r_t = r * e_pos
        b_t = b * e_neg
        k_t = k * e_neg
        a_ab = jnp.where(tri_strict, _bmm_nt(kk_t, b_t), 0.0)
        a_ak = jnp.where(tri_strict, _bmm_nt(kk_t, k_t), 0.0)
        a_rb = jnp.where(tri_incl, _bmm_nt(r_t, b_t), 0.0)
        a_rk = jnp.where(tri_incl, _bmm_nt(r_t, k_t), 0.0)
        tinv = _unit_lower_inverse(a_ab, c)
        wm = _bmm(tinv, kk_t)
        u0 = -_bmm(tinv, _bmm(a_ak, v))
        q_eff = r_t - _bmm(a_rb, wm)
        o0 = _bmm(a_rk, v) + _bmm(a_rb, u0)
        b_g = b * e_end
        k_g = k * e_end
        kr = lax.broadcasted_iota(jnp.int32, (hd, hd), 0)
        kc = lax.broadcasted_iota(jnp.int32, (hd, hd), 1)
        eye = (kr == kc).astype(F32)[None]
        m_mat = eye * jnp.exp(tot) - _bmm_tn(wm, b_g)
        d_mat = _bmm_tn(u0, b_g) + _bmm_tn(v, k_g)
        s = s_sc[h]
        states = []
        for i in range(nc):
            states.append(s)
            s = _dot(_bf(s), _bf(m_mat[i])) + d_mat[i]
        s_sc[h] = s
        o = o0 + _bmm_nt(q_eff, jnp.stack(states))
        o = o.reshape(tl, hd)
        mean = jnp.mean(o, axis=-1, keepdims=True)
        var = jnp.mean(jnp.square(o - mean), axis=-1, keepdims=True)
        o = (o - mean) * lax.rsqrt(var + GN_EPS) * lnw_ref[:, sl] + lnb_ref[:, sl]
        r2 = r_ref[0][:, sl]
        k2 = k_ref[0][:, sl]
        bonus = jnp.sum(r2 * k2 * rk_ref[:, sl], axis=-1, keepdims=True) * v_ref[0][:, sl]
        o_ref[0, :, sl] = (o + bonus) * g_ref[0][:, sl]

    @pl.when(l == pl.num_programs(2) - 1)
    def _():
        sout_ref[0] = s_sc[...]


def rwkv_scan(r, lw, k, v, kk, b, g, ln_w, ln_b, r_k, s0, tl, chunk):
    bsz, seq, _ = r.shape
    assert seq % tl == 0 and tl % chunk == 0
    hp = RW_WIDTH // LANES
    per = LANES // RW_HEAD
    seq_spec = pl.BlockSpec((1, tl, LANES), lambda bi, hi, li: (bi, li, hi))
    par_spec = pl.BlockSpec((1, LANES), lambda bi, hi, li: (0, hi))
    st_spec = pl.BlockSpec((1, per, RW_HEAD, RW_HEAD), lambda bi, hi, li: (bi, hi, 0, 0))
    row = lambda a: a.reshape(1, RW_WIDTH)
    idx = np.arange(tl)
    tri = jnp.asarray(((idx[:, None] >= idx[None, :]) & (idx[:, None] // chunk == idx[None, :] // chunk)),
                      dtype=BF16)
    return pl.pallas_call(
        functools.partial(_rwkv_scan_kernel, chunk=chunk),
        grid=(bsz, hp, seq // tl),
        in_specs=[seq_spec] * 7 + [par_spec] * 3 + [st_spec, pl.BlockSpec((tl, tl), lambda bi, hi, li: (0, 0))],
        out_specs=[seq_spec, st_spec],
        out_shape=[jax.ShapeDtypeStruct((bsz, seq, RW_WIDTH), F32),
                   jax.ShapeDtypeStruct((bsz, RW_HEADS, RW_HEAD, RW_HEAD), F32)],
        scratch_shapes=[pltpu.VMEM((per, RW_HEAD, RW_HEAD), F32)],
        compiler_params=_cparams("parallel", "parallel", "arbitrary"),
    )(r, lw, k, v, kk, b, g, row(ln_w), row(ln_b), row(r_k), s0, tri)


def _moba_prep_kernel(p_ref, qg_ref, kg_ref, grp_ref, grpt_ref, q_o, k_o, kb_o, vt_o, ks_o):
    p = p_ref[...]
    q = p[:, 0:MB_WIDTH]
    k = p[:, MB_WIDTH:2 * MB_WIDTH]
    v = p[:, 2 * MB_WIDTH:3 * MB_WIDTH]

    def head_norm(x, gain_ref):
        ms = _dot(x * x, grp_ref[...], HI) * (1.0 / MB_HEAD)
        inv = _dot(lax.rsqrt(ms + RMS_EPS), grpt_ref[...], HI)
        return x * inv * gain_ref[...]

    qn = head_norm(q, qg_ref)
    kn = head_norm(k, kg_ref)
    q_o[...] = qn
    k_o[...] = kn
    kb_o[...] = kn.astype(BF16)
    vt_o[...] = v.T.astype(BF16)
    ks_o[0] = jnp.sum(kn, axis=0, keepdims=True)


def moba_prep(p_mb, q_gain, k_gain, tm):
    t = p_mb.shape[0]
    assert t % tm == 0
    grp, grpt = _group_matrices(MB_WIDTH, MB_HEAD)
    tile = lambda i: (i, 0)
    full = lambda a: pl.BlockSpec(a.shape, lambda i: (0,) * a.ndim)
    qg = jnp.tile(q_gain, MB_HEADS).reshape(1, MB_WIDTH)
    kg = jnp.tile(k_gain, MB_HEADS).reshape(1, MB_WIDTH)
    return pl.pallas_call(
        _moba_prep_kernel,
        grid=(t // tm,),
        in_specs=[pl.BlockSpec((tm, 3 * MB_WIDTH), tile), full(qg), full(kg), full(grp), full(grpt)],
        out_specs=[pl.BlockSpec((tm, MB_WIDTH), tile)] * 3
                  + [pl.BlockSpec((MB_WIDTH, tm), lambda i: (0, i)),
                     pl.BlockSpec((1, 1, MB_WIDTH), lambda i: (i, 0, 0))],
        out_shape=[jax.ShapeDtypeStruct((t, MB_WIDTH), F32), jax.ShapeDtypeStruct((t, MB_WIDTH), F32),
                   jax.ShapeDtypeStruct((t, MB_WIDTH), BF16), jax.ShapeDtypeStruct((MB_WIDTH, t), BF16),
                   jax.ShapeDtypeStruct((t // tm, 1, MB_WIDTH), F32)],
        compiler_params=_cparams("parallel"),
    )(p_mb, qg, kg, grp, grpt)


def _top_blocks(gate, idx, axis):
    sel = jnp.zeros_like(gate)
    rem = gate
    for _ in range(MOBA_TOPK):
        mx = jnp.max(rem, axis=axis, keepdims=True)
        live = jnp.logical_and(rem == mx, rem > 0.5 * MASK_NEG)
        first = jnp.min(jnp.where(live, idx, 1e9), axis=axis, keepdims=True)
        pick = idx == first
        sel = jnp.where(pick, 1.0, sel)
        rem = jnp.where(pick, MASK_NEG, rem)
    return sel


def _moba_attn_kernel(q_ref, k_ref, vt_ref, ks_ref, o_ref, sel_sc, s_a, s_b, p_a, p_b):
    qi = pl.program_id(2)
    tq = q_ref.shape[1]
    nb = ks_ref.shape[1]
    n_heads = LANES // MB_HEAD
    q = q_ref[0]
    lane = lax.broadcasted_iota(jnp.int32, (1, LANES), 1)
    bmean = ks_ref[0] * (1.0 / MOBA_BLOCK)
    blk_f = lax.broadcasted_iota(jnp.int32, (nb, tq), 0).astype(F32)
    qi_f = qi.astype(F32)
    q_cols = []
    for h in range(n_heads):
        q_h = jnp.where((lane // MB_HEAD) == h, q, 0.0)
        gate_t = _dot_nt(bmean, q_h, HI)
        gate_t = jnp.where(blk_f < qi_f, gate_t, MASK_NEG)
        sel_sc[:, h * tq:(h + 1) * tq] = (1.0 - _top_blocks(gate_t, blk_f, 0)) * MASK_NEG
        q_cols.append((q_h * (LOG2E * MB_HEAD ** -0.5)).astype(BF16))
    q_all = jnp.concatenate(q_cols, axis=0)
    ncol = n_heads * tq

    row0 = pl.multiple_of(qi * tq, tq)
    ri = lax.broadcasted_iota(jnp.int32, (tq, ncol), 0)
    ci = lax.broadcasted_iota(jnp.int32, (tq, ncol), 1)
    causal_t = ri <= (ci % tq)
    s = jnp.where(causal_t, _dot_nt(k_ref[0, pl.ds(row0, tq), :], q_all), MASK_NEG)
    m = jnp.max(s, axis=0, keepdims=True)
    p = jnp.exp2(s - m)
    l = jnp.sum(p, axis=0, keepdims=True)
    acc = _dot(vt_ref[:, pl.ds(row0, tq)], p.astype(BF16))

    def pair_rows(i):
        return pl.multiple_of(jnp.minimum(2 * i, nb - 2) * tq, tq)

    def scores(i, s_ref):
        r0 = pair_rows(i)
        qk = _dot_nt(k_ref[0, pl.ds(r0, 2 * tq), :], q_all)
        for u in range(2):
            n = 2 * i + u
            bias = jnp.where(n < qi, sel_sc[pl.ds(jnp.minimum(n, nb - 1), 1), :], MASK_NEG)
            s_ref[u * tq:(u + 1) * tq, :] = qk[u * tq:(u + 1) * tq] + bias

    def values(i, p_ref):
        return _dot(vt_ref[:, pl.ds(pair_rows(i), 2 * tq)], p_ref[...])

    def softmax(s_ref, p_ref, m, l):
        s = s_ref[...]
        m_new = jnp.maximum(m, jnp.max(s, axis=0, keepdims=True))
        alpha = jnp.exp2(m - m_new)
        p = jnp.exp2(s - m_new)
        p_ref[...] = p.astype(BF16)
        return m_new, alpha, alpha * l + jnp.sum(p, axis=0, keepdims=True)

    scores(0, s_a)
    p_b[...] = jnp.zeros_like(p_b)

    def body(t, carry):
        m, l, acc = carry
        scores(2 * t + 1, s_b)
        pv = values(jnp.maximum(2 * t - 1, 0), p_b)
        m, alpha, l = softmax(s_a, p_a, m, l)
        acc = (acc + pv) * alpha
        scores(2 * t + 2, s_a)
        pv = values(2 * t, p_a)
        m, alpha, l = softmax(s_b, p_b, m, l)
        acc = (acc + pv) * alpha
        return m, l, acc

    trips = (qi + 3) // 4
    m, l, acc = lax.fori_loop(0, trips, body, (m, l, acc))
    acc = acc + values(jnp.maximum(2 * trips - 1, 0), p_b)
    o = acc / l
    o_t = jnp.concatenate([o[h * MB_HEAD:(h + 1) * MB_HEAD, h * tq:(h + 1) * tq] for h in range(n_heads)],
                          axis=0)
    o_ref[0] = o_t.T


def moba_attn(qn, kb, vt, ksum):
    bsz, seq, _ = qn.shape
    tq = MOBA_BLOCK
    nb = seq // tq
    assert nb >= 2
    hp = MB_WIDTH // LANES
    ncol = (LANES // MB_HEAD) * tq
    return pl.pallas_call(
        _moba_attn_kernel,
        grid=(bsz, hp, nb),
        in_specs=[pl.BlockSpec((1, tq, LANES), lambda b, h, i: (b, i, h)),
                  pl.BlockSpec((1, seq, LANES), lambda b, h, i: (b, 0, h)),
                  pl.BlockSpec((LANES, seq), lambda b, h, i: (h, b)),
                  pl.BlockSpec((1, nb, LANES), lambda b, h, i: (b, 0, h))],
        out_specs=pl.BlockSpec((1, tq, LANES), lambda b, h, i: (b, i, h)),
        out_shape=jax.ShapeDtypeStruct((bsz, seq, MB_WIDTH), F32),
        scratch_shapes=[pltpu.VMEM((nb, ncol), F32),
                        pltpu.VMEM((2 * tq, ncol), F32), pltpu.VMEM((2 * tq, ncol), F32),
                        pltpu.VMEM((2 * tq, ncol), BF16), pltpu.VMEM((2 * tq, ncol), BF16)],
        compiler_params=_cparams("parallel", "parallel", "arbitrary"),
    )(qn, kb, vt, ksum)


def _moba_scores_kernel(pt_ref, q_ref, *refs, n_pg):
    k_refs = refs[:n_pg]
    s_o = refs[n_pg]
    q = q_ref[0]
    for j in range(n_pg):
        s_o[0, j] = jnp.sum(k_refs[j][0, 0] * q, axis=1)


def _moba_select_kernel(s_ref, q_ref, kn_ref, p_o, w_o, id_o, *, ppb):
    scale = MB_HEAD ** -0.5
    s = s_ref[0]
    n_lp = s.shape[0]
    nblk = n_lp // ppb
    page_sum = jnp.sum(s, axis=-1, keepdims=True)
    gate = jnp.sum(page_sum.reshape(nblk, ppb, MB_HEADS, 1), axis=1) * (1.0 / MOBA_BLOCK)
    idx = lax.broadcasted_iota(jnp.int32, gate.shape, 0).astype(F32)
    lane = lax.broadcasted_iota(jnp.int32, (MB_HEADS, LANES), 1)
    sel = jnp.zeros_like(gate)
    rem = gate
    ids = jnp.zeros((MB_HEADS, LANES), F32)
    for t in range(MOBA_TOPK):
        mx = jnp.max(rem, axis=0, keepdims=True)
        first = jnp.min(jnp.where(rem == mx, idx, 1e9), axis=0, keepdims=True)
        pick = idx == first
        sel = jnp.where(pick, 1.0, sel)
        rem = jnp.where(pick, MASK_NEG, rem)
        for u in range(ppb):
            ids = jnp.where(lane == t * ppb + u, first[0] * ppb + u, ids)
    id_o[0] = ids.astype(jnp.int32)
    sel_pg = jnp.broadcast_to(sel[:, None], (nblk, ppb, MB_HEADS, 1)).reshape(n_lp, MB_HEADS, 1) > 0.5
    s_own = jnp.sum(kn_ref[0] * q_ref[0], axis=-1, keepdims=True) * scale
    sm = jnp.where(sel_pg, s * scale, MASK_NEG)
    m_all = jnp.maximum(jnp.max(jnp.max(sm, axis=0), axis=-1, keepdims=True), s_own)
    p = jnp.where(sel_pg, jnp.exp(sm - m_all[None]), 0.0)
    w_own = jnp.exp(s_own - m_all)
    inv = 1.0 / (jnp.sum(jnp.sum(p, axis=0), axis=-1, keepdims=True) + w_own)
    p_o[0] = p * inv[None]
    w_o[0] = jnp.broadcast_to(w_own * inv, (MB_HEADS, LANES))


def _moba_values_kernel(pt_ref, id_ref, p_ref, w_ref, vn_ref, *refs, n_sel):
    v_refs = refs[:n_sel]
    o_ref = refs[n_sel]
    b = pl.program_id(0)
    h = pl.program_id(1)
    acc = w_ref[0, pl.ds(h, 1), 0:1] * vn_ref[0, 0]
    for j in range(n_sel):
        page = id_ref[(b * MB_HEADS + h) * n_sel + j]
        prob = p_ref[0, page, pl.ds(h, 1), :]
        acc = acc + _dot_nt(prob, v_refs[j][0, 0, 0], HI)
    o_ref[0, 0] = acc


def moba_decode(qn, kn, vn, k_cache_t, v_cache_t, page_table, layer):
    bsz = qn.shape[0]
    n_lp = page_table.shape[1]
    ppb = MOBA_BLOCK // PAGE_SIZE
    assert n_lp % ppb == 0 and n_lp // ppb >= MOBA_TOPK
    n_pg = 8
    assert n_lp % n_pg == 0
    n_sel = MOBA_TOPK * ppb
    heads = lambda a: a.reshape(bsz, MB_HEADS, MB_HEAD)
    page_blk = (1, 1, MB_HEADS, MB_HEAD, PAGE_SIZE)

    scores = pl.pallas_call(
        functools.partial(_moba_scores_kernel, n_pg=n_pg),
        grid_spec=pltpu.PrefetchScalarGridSpec(
            num_scalar_prefetch=1,
            grid=(bsz, n_lp // n_pg),
            in_specs=[pl.BlockSpec((1, MB_HEADS, MB_HEAD, 1), lambda b, g, pt: (b, 0, 0, 0))]
                     + [pl.BlockSpec(page_blk, functools.partial(
                         lambda b, g, pt, j: (layer, pt[b, g * n_pg + j], 0, 0, 0), j=j)) for j in range(n_pg)],
            out_specs=pl.BlockSpec((1, n_pg, MB_HEADS, PAGE_SIZE), lambda b, g, pt: (b, g, 0, 0))),
        out_shape=jax.ShapeDtypeStruct((bsz, n_lp, MB_HEADS, PAGE_SIZE), F32),
        compiler_params=_cparams("parallel", "arbitrary"),
    )(page_table, qn.reshape(bsz, MB_HEADS, MB_HEAD, 1), *([k_cache_t] * n_pg))

    vec = pl.BlockSpec((1, MB_HEADS, MB_HEAD), lambda b: (b, 0, 0))
    probs, w_own, ids = pl.pallas_call(
        functools.partial(_moba_select_kernel, ppb=ppb),
        grid=(bsz,),
        in_specs=[pl.BlockSpec((1, n_lp, MB_HEADS, PAGE_SIZE), lambda b: (b, 0, 0, 0)), vec, vec],
        out_specs=[pl.BlockSpec((1, n_lp, MB_HEADS, PAGE_SIZE), lambda b: (b, 0, 0, 0)),
                   pl.BlockSpec((1, MB_HEADS, LANES), lambda b: (b, 0, 0)),
                   pl.BlockSpec((1, MB_HEADS, LANES), lambda b: (b, 0, 0))],
        out_shape=[jax.ShapeDtypeStruct((bsz, n_lp, MB_HEADS, PAGE_SIZE), F32),
                   jax.ShapeDtypeStruct((bsz, MB_HEADS, LANES), F32),
                   jax.ShapeDtypeStruct((bsz, MB_HEADS, LANES), jnp.int32)],
        compiler_params=_cparams("parallel"),
    )(scores, heads(qn), heads(kn))

    sel_ids = ids[:, :, :n_sel].reshape(-1)
    out = pl.pallas_call(
        functools.partial(_moba_values_kernel, n_sel=n_sel),
        grid_spec=pltpu.PrefetchScalarGridSpec(
            num_scalar_prefetch=2,
            grid=(bsz, MB_HEADS),
            in_specs=[pl.BlockSpec((1, n_lp, MB_HEADS, PAGE_SIZE), lambda b, h, pt, sid: (b, 0, 0, 0)),
                      pl.BlockSpec((1, MB_HEADS, LANES), lambda b, h, pt, sid: (b, 0, 0)),
                      pl.BlockSpec((1, 1, 1, MB_HEAD), lambda b, h, pt, sid: (b, h, 0, 0))]
                     + [pl.BlockSpec((1, 1, 1, MB_HEAD, PAGE_SIZE), functools.partial(
                         lambda b, h, pt, sid, j: (layer, pt[b, sid[(b * MB_HEADS + h) * n_sel + j]], h, 0, 0),
                         j=j)) for j in range(n_sel)],
            out_specs=pl.BlockSpec((1, 1, 1, MB_HEAD), lambda b, h, pt, sid: (b, h, 0, 0))),
        out_shape=jax.ShapeDtypeStruct((bsz, MB_HEADS, 1, MB_HEAD), F32),
        compiler_params=_cparams("parallel", "arbitrary"),
    )(page_table, sel_ids, probs, w_own, vn.reshape(bsz, MB_HEADS, 1, MB_HEAD), *([v_cache_t] * n_sel))
    return out.reshape(bsz, MB_WIDTH)


def _hgrn_constants(c):
    rows = np.arange(c)
    dmats, ups, los, masks = [], [], [], []
    z = c
    while z >= 2:
        seg = rows // z
        mid = seg * z + z // 2
        up = rows >= mid
        d = np.zeros((c, c), np.float32)
        for i in rows:
            if up[i]:
                d[i, mid[i]:i + 1] = 1.0
            else:
                d[i, i + 1:mid[i]] = 1.0
        dmats.append(d)
        ups.append(np.repeat(up[:, None], HG_D, 1).astype(np.float32))
        los.append(np.repeat(~up[:, None], HG_D, 1).astype(np.float32))
        masks.append(((seg[:, None] == seg[None, :]) & up[:, None] & (~up)[None, :]).astype(np.float32))
        z //= 2
    tri = np.tril(np.ones((c, c), np.float32))
    dmats += [tri, 1.0 - tri]
    return (jnp.asarray(np.concatenate(dmats, 0), dtype=BF16), jnp.asarray(np.stack(ups)),
            jnp.asarray(np.stack(los)), jnp.asarray(np.stack(masks)))


def _hgrn_kernel(pq_ref, pf_ref, pi_ref, pg_ref, lbl_ref, gain_ref, s0_ref, dd_ref, up_ref, lo_ref,
                 mk_ref, o_ref, sout_ref, st_sc, *, chunk, layer):
    l = pl.program_id(2)
    tl = pq_ref.shape[1]
    c = chunk
    nc = tl // c
    nlev = up_ref.shape[0]
    er = lax.broadcasted_iota(jnp.int32, (HG_D, HG_D), 0)
    ec = lax.broadcasted_iota(jnp.int32, (HG_D, HG_D), 1)
    eye_d = (er == ec).astype(F32)

    @pl.when(l == 0)
    def _():
        st_sc[...] = _dot_nt(eye_d, s0_ref[0, 0], HI)

    logits = lbl_ref[...]
    ex = jnp.exp(logits - jnp.max(logits, axis=0, keepdims=True))
    prob = ex / jnp.sum(ex, axis=0, keepdims=True)
    lb = jnp.sum(prob[0:layer + 1], axis=0, keepdims=True) - prob[0:1]

    qr = pq_ref[0]
    q = qr * _sigmoid(qr)
    forget = lb + (1.0 - lb) * _sigmoid(pf_ref[0])
    logf = jnp.log(jnp.maximum(forget, 1e-30))
    kg = 1.0 - forget
    v = pi_ref[0]

    ri = lax.broadcasted_iota(jnp.int32, (c, c), 0)
    ci = lax.broadcasted_iota(jnp.int32, (c, c), 1)
    diag = ri == ci
    rows = [slice(i * c, (i + 1) * c) for i in range(nc)]
    e_all = jnp.exp(_split_dot(dd_ref[...], jnp.concatenate([logf[rs] for rs in rows], axis=1)))
    intra, kv, qe, dec = [], [], [], []
    for i, rs in enumerate(rows):
        qc, kc, vc = q[rs], kg[rs], v[rs]
        e = e_all[:, i * HG_D:(i + 1) * HG_D]
        a = jnp.where(diag, jnp.sum(qc * kc, axis=1, keepdims=True), 0.0)
        for lev in range(nlev):
            el = e[lev * c:(lev + 1) * c]
            a = a + mk_ref[lev] * _dot_nt(_bf(qc * el * up_ref[lev]), _bf(kc * el * lo_ref[lev]))
        e_in = e[nlev * c:(nlev + 1) * c]
        e_out = e[(nlev + 1) * c:(nlev + 2) * c]
        intra.append(_dot(_bf(a), _bf(vc)))
        kv.append(_dot_tn(_bf(vc), _bf(kc * e_out)))
        qe.append(_bf(qc * e_in))
        dec.append(e_in[c - 1:c, :])
    st = st_sc[...]
    outs = []
    for i in range(nc):
        outs.append(_dot_nt(qe[i], _bf(st)) + intra[i])
        st = st * dec[i] + kv[i]
    st_sc[...] = st
    o = jnp.concatenate(outs, axis=0) if nc > 1 else outs[0]
    ms = jnp.mean(o * o, axis=-1, keepdims=True)
    gr = pg_ref[0]
    o_ref[0] = o * lax.rsqrt(ms + RMS_EPS) * gain_ref[...] * (gr * _sigmoid(gr))

    @pl.when(l == pl.num_programs(2) - 1)
    def _():
        sout_ref[0, 0] = _dot_nt(eye_d, st, HI)


def hgrn_scan(p, lb_logits, out_gain, s0, layer, tl, chunk):
    bsz, seq, _ = p.shape
    assert seq % tl == 0 and tl % chunk == 0
    dd, up, lo, mk = _hgrn_constants(chunk)
    col = lambda off: pl.BlockSpec((1, tl, HG_D), lambda b, h, li: (b, li, off + h))
    full = lambda a: pl.BlockSpec(a.shape, lambda b, h, li: (0,) * a.ndim)
    st_spec = pl.BlockSpec((1, 1, HG_D, HG_D), lambda b, h, li: (b, h, 0, 0))
    return pl.pallas_call(
        functools.partial(_hgrn_kernel, chunk=chunk, layer=layer),
        grid=(bsz, HG_HEADS, seq // tl),
        in_specs=[col(0), col(HG_HEADS), col(2 * HG_HEADS), col(3 * HG_HEADS),
                  pl.BlockSpec((lb_logits.shape[0], HG_D), lambda b, h, li: (0, h)),
                  pl.BlockSpec((1, HG_D), lambda b, h, li: (0, 0)),
                  st_spec, full(dd), full(up), full(lo), full(mk)],
        out_specs=[pl.BlockSpec((1, tl, HG_D), lambda b, h, li: (b, li, h)), st_spec],
        out_shape=[jax.ShapeDtypeStruct((bsz, seq, D_MODEL), F32),
                   jax.ShapeDtypeStruct((bsz, HG_HEADS, HG_D, HG_D), F32)],
        scratch_shapes=[pltpu.VMEM((HG_D, HG_D), F32)],
        compiler_params=_cparams("parallel", "parallel", "arbitrary"),
    )(p, p, p, p, lb_logits, out_gain.reshape(1, HG_D), s0, dd, up, lo, mk)


def _row_to_col(row, eye):
    return jnp.sum(jnp.where(eye, row, 0.0), axis=1, keepdims=True)


def _col_to_row(col, eye):
    return jnp.sum(jnp.where(eye, col, 0.0), axis=0, keepdims=True)


def _rwkv_step_kernel(r_ref, lw_ref, k_ref, v_ref, kk_ref, b_ref, g_ref, lnw_ref, lnb_ref, rk_ref,
                      s0_ref, o_ref, sout_ref):
    hd = RW_HEAD
    ri = lax.broadcasted_iota(jnp.int32, (hd, hd), 0)
    ci = lax.broadcasted_iota(jnp.int32, (hd, hd), 1)
    eye = ri == ci
    for h in range(RW_HEADS):
        sl = slice(h * hd, (h + 1) * hd)
        row = lambda ref: ref[0][:, sl]
        r, k, v, kk, b = row(r_ref), row(k_ref), row(v_ref), row(kk_ref), row(b_ref)
        s = s0_ref[0, h]
        s_kk = jnp.sum(s * kk, axis=1, keepdims=True)
        s = s * jnp.exp(row(lw_ref)) - s_kk * b + _row_to_col(v, eye) * k
        sout_ref[0, h] = s
        o = _col_to_row(jnp.sum(s * r, axis=1, keepdims=True), eye)
        mean = jnp.mean(o, axis=-1, keepdims=True)
        var = jnp.mean(jnp.square(o - mean), axis=-1, keepdims=True)
        o = (o - mean) * lax.rsqrt(var + GN_EPS) * lnw_ref[:, sl] + lnb_ref[:, sl]
        bonus = jnp.sum(r * k * rk_ref[:, sl], axis=-1, keepdims=True) * v
        o_ref[0, :, sl] = (o + bonus) * row(g_ref)


def rwkv_step(r, lw, k, v, kk, b, g, ln_w, ln_b, r_k, s0):
    bsz = r.shape[0]
    seq_spec = pl.BlockSpec((1, 1, RW_WIDTH), lambda i: (i, 0, 0))
    par_spec = pl.BlockSpec((1, RW_WIDTH), lambda i: (0, 0))
    st_spec = pl.BlockSpec((1, RW_HEADS, RW_HEAD, RW_HEAD), lambda i: (i, 0, 0, 0))
    row = lambda a: a.reshape(1, RW_WIDTH)
    tok = lambda a: a.reshape(bsz, 1, RW_WIDTH)
    o, s = pl.pallas_call(
        _rwkv_step_kernel,
        grid=(bsz,),
        in_specs=[seq_spec] * 7 + [par_spec] * 3 + [st_spec],
        out_specs=[seq_spec, st_spec],
        out_shape=[jax.ShapeDtypeStruct((bsz, 1, RW_WIDTH), F32),
                   jax.ShapeDtypeStruct((bsz, RW_HEADS, RW_HEAD, RW_HEAD), F32)],
        compiler_params=_cparams("parallel"),
    )(*[tok(a) for a in (r, lw, k, v, kk, b, g)], row(ln_w), row(ln_b), row(r_k), s0)
    return o.reshape(bsz, RW_WIDTH), s


def _hgrn_step_kernel(p_ref, lbl_ref, gain_ref, s0_ref, o_ref, sout_ref, *, layer):
    d = HG_D
    ri = lax.broadcasted_iota(jnp.int32, (d, d), 0)
    ci = lax.broadcasted_iota(jnp.int32, (d, d), 1)
    eye = ri == ci
    logits = lbl_ref[...]
    ex = jnp.exp(logits - jnp.max(logits, axis=0, keepdims=True))
    prob = ex / jnp.sum(ex, axis=0, keepdims=True)
    lb_all = jnp.sum(prob[0:layer + 1], axis=0, keepdims=True) - prob[0:1]
    width = HG_HEADS * d
    for h in range(HG_HEADS):
        col = lambda part: p_ref[0][:, part * width + h * d:part * width + (h + 1) * d]
        qr, fr, v, gr = col(0), col(1), col(2), col(3)
        lb = lb_all[:, h * d:(h + 1) * d]
        q = qr * _sigmoid(qr)
        forget = lb + (1.0 - lb) * _sigmoid(fr)
        decay = jnp.maximum(forget, 1e-30)
        s = _row_to_col(decay, eye) * s0_ref[0, h] + _row_to_col(1.0 - forget, eye) * v
        sout_ref[0, h] = s
        o = jnp.sum(_row_to_col(q, eye) * s, axis=0, keepdims=True)
        ms = jnp.mean(o * o, axis=-1, keepdims=True)
        o_ref[0, :, h * d:(h + 1) * d] = o * lax.rsqrt(ms + RMS_EPS) * gain_ref[...] * (gr * _sigmoid(gr))


def hgrn_step(p, lb_logits, out_gain, s0, layer):
    bsz = p.shape[0]
    st_spec = pl.BlockSpec((1, HG_HEADS, HG_D, HG_D), lambda i: (i, 0, 0, 0))
    o, s = pl.pallas_call(
        functools.partial(_hgrn_step_kernel, layer=layer),
        grid=(bsz,),
        in_specs=[pl.BlockSpec((1, 1, p.shape[1]), lambda i: (i, 0, 0)),
                  pl.BlockSpec(lb_logits.shape, lambda i: (0, 0)),
                  pl.BlockSpec((1, HG_D), lambda i: (0, 0)), st_spec],
        out_specs=[pl.BlockSpec((1, 1, D_MODEL), lambda i: (i, 0, 0)), st_spec],
        out_shape=[jax.ShapeDtypeStruct((bsz, 1, D_MODEL), F32),
                   jax.ShapeDtypeStruct((bsz, HG_HEADS, HG_D, HG_D), F32)],
        compiler_params=_cparams("parallel"),
    )(p.reshape(bsz, 1, -1), lb_logits, out_gain.reshape(1, HG_D), s0)
    return o.reshape(bsz, D_MODEL), s


def _even_layer(xp, xs, il, vf_p, vf_s, w, cache_k, cache_v, page_table, state_rwkv, state_shift):
    nb_p, seq, d = xp.shape
    nb_s = xs.shape[0]
    tp = nb_p * seq
    rw_params = (w['rw_mu'][il], w['rw_w0'][il], w['w2p'][il], w['rw_a0'][il], w['a2p'][il],
                 w['rw_g2'][il], w['rw_k_k'][il], w['rw_k_a'][il])
    has_vres = il > 0
    vres_w = (w['rw_v0'][il - 1], w['rw_v1'][il - 1], w['rw_v2'][il - 1]) if has_vres else None
    ln = (w['rw_ln_w'][il], w['rw_ln_b'][il], w['rw_r_k'][il].reshape(-1))

    p_rw, p_mb = norm_matmul(xp.reshape(tp, d), w['norm_mix'][2 * il], w['w_in_even'][il],
                             (RW_IN, 3 * MB_WIDTH), 512)
    vres = (vf_p,) + vres_w if has_vres else None
    prep = rwkv_prep(p_rw, None, rw_params, vres, True, seq, 512)
    if not has_vres:
        vf_p = prep[7]
    seq3 = lambda a: a.reshape(nb_p, seq, -1)
    o_rw, wkv_p = rwkv_scan(*[seq3(a) for a in prep[:7]], *ln,
                            jnp.zeros((nb_p, RW_HEADS, RW_HEAD, RW_HEAD), F32), 512, RW_CHUNK)
    qn, kn, kb, vt, ksum = moba_prep(p_mb, w['mb_q_norm'][il], w['mb_k_norm'][il], MOBA_BLOCK)
    o_mb = moba_attn(seq3(qn), seq3(kb), vt, ksum.reshape(nb_p, seq // MOBA_BLOCK, MB_WIDTH))
    lyr = 2 * il
    yp = mix_ffn(xp.reshape(tp, d), [o_rw.reshape(tp, RW_WIDTH), o_mb.reshape(tp, MB_WIDTH)],
                 [w['w_out_even'][il][:RW_WIDTH], w['w_out_even'][il][RW_WIDTH:]],
                 w['norm_ffn'][lyr], w['ffn_w_gate'][lyr], w['ffn_w_up'][lyr], w['ffn_w_down'][lyr],
                 512, D_FF // 2).reshape(nb_p, seq, d)
    mk_p = kn.reshape(nb_p, seq, MB_HEADS, MB_HEAD)
    mv_p = p_mb[:, 2 * MB_WIDTH:].reshape(nb_p, seq, MB_HEADS, MB_HEAD)
    sh_p = p_rw.reshape(nb_p, seq, RW_IN)[:, -1]

    ps_rw, ps_mb = norm_matmul(xs.reshape(nb_s, d), w['norm_mix'][2 * il], w['w_in_even'][il],
                               (RW_IN, 3 * MB_WIDTH), nb_s)
    vres = (vf_s,) + vres_w if has_vres else None
    prep = rwkv_prep(ps_rw, state_shift[il], rw_params, vres, False, 1, nb_s)
    if not has_vres:
        vf_s = prep[7]
    os_rw, wkv_s = rwkv_step(*prep[:7], *ln, state_rwkv[il])
    qs, ks, _, _, _ = moba_prep(ps_mb, w['mb_q_norm'][il], w['mb_k_norm'][il], nb_s)
    vs = ps_mb[:, 2 * MB_WIDTH:]
    os_mb = moba_decode(qs, ks, vs, cache_k, cache_v, page_table, il)
    ys = mix_ffn(xs.reshape(nb_s, d), [os_rw, os_mb],
                 [w['w_out_even'][il][:RW_WIDTH], w['w_out_even'][il][RW_WIDTH:]],
                 w['norm_ffn'][lyr], w['ffn_w_gate'][lyr], w['ffn_w_up'][lyr], w['ffn_w_down'][lyr],
                 nb_s, D_FF // 2).reshape(nb_s, 1, d)
    mk_s = ks.reshape(nb_s, 1, MB_HEADS, MB_HEAD)
    mv_s = vs.reshape(nb_s, 1, MB_HEADS, MB_HEAD)
    return yp, ys, vf_p, vf_s, (mk_p, mv_p, mk_s, mv_s, wkv_p, wkv_s, sh_p, ps_rw)


def _odd_layer(xp, xs, il, w, state_hgrn):
    nb_p, seq, d = xp.shape
    nb_s = xs.shape[0]
    tp = nb_p * seq
    lyr = 2 * il + 1
    (pp,) = norm_matmul(xp.reshape(tp, d), w['norm_mix'][lyr], w['w_in_odd'][il], (4 * d,), 512)
    o_p, hg_p = hgrn_scan(pp.reshape(nb_p, seq, 4 * d), w['hg_lb_logits'], w['hg_out_norm'][il],
                          jnp.zeros((nb_p, HG_HEADS, HG_D, HG_D), F32), il, 512, HG_CHUNK)
    yp = mix_ffn(xp.reshape(tp, d), [o_p.reshape(tp, d)], [w['w_out_odd'][il]],
                 w['norm_ffn'][lyr], w['ffn_w_gate'][lyr], w['ffn_w_up'][lyr], w['ffn_w_down'][lyr],
                 512, D_FF // 2).reshape(nb_p, seq, d)
    (ps,) = norm_matmul(xs.reshape(nb_s, d), w['norm_mix'][lyr], w['w_in_odd'][il], (4 * d,), nb_s)
    o_s, hg_s = hgrn_step(ps, w['hg_lb_logits'], w['hg_out_norm'][il], state_hgrn[il], il)
    ys = mix_ffn(xs.reshape(nb_s, d), [o_s], [w['w_out_odd'][il]],
                 w['norm_ffn'][lyr], w['ffn_w_gate'][lyr], w['ffn_w_up'][lyr], w['ffn_w_down'][lyr],
                 nb_s, D_FF // 2).reshape(nb_s, 1, d)
    return yp, ys, hg_p, hg_s


def kernel(x_prompt, x_sample, cache_moba_k, cache_moba_v, page_table, state_rwkv, state_rwkv_shift, state_hgrn, norm_mix, norm_ffn, w_in_even, w_out_even, rw_mu, rw_w0, rw_w2, rw_a0, rw_a2, rw_g2, rw_k_k, rw_k_a, rw_r_k, rw_ln_w, rw_ln_b, rw_v0, rw_v1, rw_v2, mb_q_norm, mb_k_norm, w_in_odd, w_out_odd, hg_lb_logits, hg_out_norm, ffn_w_gate, ffn_w_up, ffn_w_down):
    depth = norm_mix.shape[0]
    zeros_lora = jnp.zeros_like(rw_w2)
    w = dict(
        norm_mix=norm_mix, norm_ffn=norm_ffn,
        w_in_even=w_in_even.astype(BF16), w_out_even=w_out_even.astype(BF16),
        w_in_odd=w_in_odd.astype(BF16), w_out_odd=w_out_odd.astype(BF16),
        ffn_w_gate=ffn_w_gate.astype(BF16), ffn_w_up=ffn_w_up.astype(BF16),
        ffn_w_down=ffn_w_down.astype(BF16),
        rw_mu=rw_mu, rw_w0=rw_w0, rw_a0=rw_a0, rw_g2=rw_g2, rw_k_k=rw_k_k, rw_k_a=rw_k_a,
        rw_r_k=rw_r_k, rw_ln_w=rw_ln_w, rw_ln_b=rw_ln_b, rw_v0=rw_v0, rw_v1=rw_v1, rw_v2=rw_v2,
        w2p=jnp.concatenate([rw_w2, zeros_lora], axis=1), a2p=jnp.concatenate([zeros_lora, rw_a2], axis=1),
        mb_q_norm=mb_q_norm, mb_k_norm=mb_k_norm, hg_lb_logits=hg_lb_logits, hg_out_norm=hg_out_norm)
    page_table = page_table.astype(jnp.int32)
    cache_k = jnp.transpose(cache_moba_k, (0, 1, 3, 4, 2))
    cache_v = jnp.transpose(cache_moba_v, (0, 1, 3, 4, 2))

    xp, xs = x_prompt, x_sample
    vf_p = vf_s = None
    even_out, hg_out = [], []
    for layer in range(depth):
        il = layer // 2
        if layer % 2 == 0:
            xp, xs, vf_p, vf_s, outs = _even_layer(xp, xs, il, vf_p, vf_s, w, cache_k, cache_v,
                                                   page_table, state_rwkv, state_rwkv_shift)
            even_out.append(outs)
        else:
            xp, xs, hg_p, hg_s = _odd_layer(xp, xs, il, w, state_hgrn)
            hg_out.append((hg_p, hg_s))
    stack = lambda i: jnp.stack([o[i] for o in even_out])
    return (xp, xs, stack(0), stack(1), stack(2), stack(3), stack(4), stack(5), stack(6), stack(7),
            jnp.stack([o[0] for o in hg_out]), jnp.stack([o[1] for o in hg_out]))
```

```python
import functools

import numpy as np
import jax
import jax.numpy as jnp
from jax import lax
from jax.experimental import pallas as pl
from jax.experimental.pallas import tpu as pltpu

F32 = jnp.float32
BF16 = jnp.bfloat16
HI = lax.Precision.HIGHEST

D_MODEL = 1024
PAGE_SIZE = 128
RW_HEAD = 64
RW_WIDTH = 512
RW_HEADS = 8
RW_IN = 1792
MB_HEAD = 64
MB_WIDTH = 512
MB_HEADS = 8
MOBA_BLOCK = 256
MOBA_TOPK = 3
HG_HEADS = 8
HG_D = 128
D_FF = 2816
RMS_EPS = 1e-6
GN_EPS = 64e-5
MASK_NEG = -1e30
LOG2E = 1.4426950408889634

LANES = 128
VMEM_LIMIT = 56 * 1024 * 1024
RW_CHUNK = 64
HG_CHUNK = 64


def _cparams(*sem):
    return pltpu.CompilerParams(dimension_semantics=sem, vmem_limit_bytes=VMEM_LIMIT)


def _bf(x):
    return x.astype(BF16)


def _dot(a, b, precision=None):
    return jnp.dot(a, b, precision=precision, preferred_element_type=F32)


def _dot_nt(a, b, precision=None):
    return lax.dot_general(a, b, (((1,), (1,)), ((), ())), precision=precision,
                           preferred_element_type=F32)


def _dot_tn(a, b, precision=None):
    return lax.dot_general(a, b, (((0,), (0,)), ((), ())), precision=precision,
                           preferred_element_type=F32)


def _bmm(a, b):
    return lax.dot_general(_bf(a), _bf(b), (((2,), (1,)), ((0,), (0,))), preferred_element_type=F32)


def _bmm_nt(a, b):
    return lax.dot_general(_bf(a), _bf(b), (((2,), (2,)), ((0,), (0,))), preferred_element_type=F32)


def _bmm_tn(a, b):
    return lax.dot_general(_bf(a), _bf(b), (((1,), (1,)), ((0,), (0,))), preferred_element_type=F32)


def _split_dot(a_bf16, b):
    n = b.shape[1]
    hi = b.astype(BF16)
    lo = (b - hi.astype(F32)).astype(BF16)
    out = _dot(a_bf16, jnp.concatenate([hi, lo], axis=1))
    return out[:, :n] + out[:, n:]


def _hi_lo(x):
    hi = x.astype(BF16)
    return hi, (x - hi.astype(F32)).astype(BF16)


def _dot_x3(a, b):
    a_hi, a_lo = _hi_lo(a)
    b_hi, b_lo = _hi_lo(b)
    return _dot(a_hi, b_hi) + (_dot(a_hi, b_lo) + _dot(a_lo, b_hi))


def _dot_x2(a, g_bf16):
    a_hi, a_lo = _hi_lo(a)
    return _dot(a_hi, g_bf16) + _dot(a_lo, g_bf16)


def _sigmoid(x):
    return 1.0 / (1.0 + jnp.exp(-x))


def _group_matrices(width, group):
    g = np.zeros((width, LANES), np.float32)
    g[np.arange(width), np.arange(width) // group] = 1.0
    return jnp.asarray(g, dtype=BF16), jnp.asarray(g.T.copy(), dtype=BF16)


def _norm_matmul_kernel(x_ref, gain_ref, w_ref, *out_refs, widths, tn):
    x = x_ref[...]
    ms = jnp.mean(x * x, axis=-1, keepdims=True)
    h = (x * lax.rsqrt(ms + RMS_EPS) * gain_ref[...]).astype(BF16)
    col = 0
    for o_ref, width in zip(out_refs, widths):
        for c0 in range(0, width, tn):
            sz = min(tn, width - c0)
            o_ref[:, c0:c0 + sz] = _dot(h, w_ref[:, col + c0:col + c0 + sz])
        col += width


def norm_matmul(x, gain, w_bf16, widths, tm):
    m, d = x.shape
    n = w_bf16.shape[1]
    assert sum(widths) == n and m % tm == 0
    kern = functools.partial(_norm_matmul_kernel, widths=tuple(widths), tn=512)
    return pl.pallas_call(
        kern,
        grid=(m // tm,),
        in_specs=[pl.BlockSpec((tm, d), lambda i: (i, 0)),
                  pl.BlockSpec((1, d), lambda i: (0, 0)),
                  pl.BlockSpec((d, n), lambda i: (0, 0))],
        out_specs=[pl.BlockSpec((tm, wd), lambda i: (i, 0)) for wd in widths],
        out_shape=[jax.ShapeDtypeStruct((m, wd), F32) for wd in widths],
        compiler_params=_cparams("parallel"),
    )(x, gain.reshape(1, d), w_bf16)


def _mix_ffn_kernel(*refs, n_act):
    x_ref = refs[0]
    a_refs = refs[1:1 + n_act]
    wo_refs = refs[1 + n_act:1 + 2 * n_act]
    gain_ref, wg_ref, wu_ref, wd_ref, y_ref, x1_sc, h_sc, acc_sc = refs[1 + 2 * n_act:]
    j = pl.program_id(1)

    @pl.when(j == 0)
    def _():
        x1 = x_ref[...]
        for a_ref, wo_ref in zip(a_refs, wo_refs):
            x1 = x1 + _dot(a_ref[...].astype(BF16), wo_ref[...])
        x1_sc[...] = x1
        ms = jnp.mean(x1 * x1, axis=-1, keepdims=True)
        h_sc[...] = (x1 * lax.rsqrt(ms + RMS_EPS) * gain_ref[...]).astype(BF16)
        acc_sc[...] = jnp.zeros_like(acc_sc)

    h = h_sc[...]
    g = _dot(h, wg_ref[...])
    u = _dot(h, wu_ref[...])
    act = (g * _sigmoid(g) * u).astype(BF16)
    acc_sc[...] += _dot(act, wd_ref[...])

    @pl.when(j == pl.num_programs(1) - 1)
    def _():
        y_ref[...] = x1_sc[...] + acc_sc[...]


def mix_ffn(x, acts, wos_bf16, gain, wg, wu, wd, tm, tf):
    m, d = x.shape
    f = wg.shape[1]
    assert m % tm == 0 and f % tf == 0
    n_act = len(acts)
    in_specs = [pl.BlockSpec((tm, d), lambda i, j: (i, 0))]
    in_specs += [pl.BlockSpec((tm, a.shape[1]), lambda i, j: (i, 0)) for a in acts]
    in_specs += [pl.BlockSpec(w.shape, lambda i, j: (0, 0)) for w in wos_bf16]
    in_specs += [pl.BlockSpec((1, d), lambda i, j: (0, 0)),
                 pl.BlockSpec((d, tf), lambda i, j: (0, j)),
                 pl.BlockSpec((d, tf), lambda i, j: (0, j)),
                 pl.BlockSpec((tf, d), lambda i, j: (j, 0))]
    return pl.pallas_call(
        functools.partial(_mix_ffn_kernel, n_act=n_act),
        grid=(m // tm, f // tf),
        in_specs=in_specs,
        out_specs=pl.BlockSpec((tm, d), lambda i, j: (i, 0)),
        out_shape=jax.ShapeDtypeStruct((m, d), F32),
        scratch_shapes=[pltpu.VMEM((tm, d), F32), pltpu.VMEM((tm, d), BF16), pltpu.VMEM((tm, d), F32)],
        compiler_params=_cparams("parallel", "arbitrary"),
    )(x, *acts, *wos_bf16, gain.reshape(1, d), wg, wu, wd)


def _rwkv_prep_kernel(*refs, shift_mode, has_vres, tiles_per_seq):
    it = iter(refs)
    p_ref = next(it)
    prev_ref = next(it)
    mu_ref, w0_ref, a0_ref, kk_ref, ka_ref, w2_ref, a2_ref, g2_ref, grp_ref, grpt_ref = (
        next(it) for _ in range(10))
    if has_vres:
        vf_ref, v0_ref, v1_ref, v2_ref = (next(it) for _ in range(4))
    r_o, lw_o, k_o, v_o, kkn_o, b_o, g_o = (next(it) for _ in range(7))
    if not has_vres:
        vf_o = next(it)

    pf = p_ref[...]
    if shift_mode:
        first = (pl.program_id(0) % tiles_per_seq) == 0
        prev_row = jnp.where(first, 0.0, prev_ref[7:8, :])
        rows = lax.broadcasted_iota(jnp.int32, pf.shape, 0)
        prev = jnp.where(rows == 0, prev_row, pltpu.roll(pf, 1, axis=0))
    else:
        prev = prev_ref[...]
    xs = pf + (prev - pf) * mu_ref[...]
    r = xs[:, 0:RW_WIDTH]
    k = xs[:, RW_WIDTH:2 * RW_WIDTH]
    v = xs[:, 2 * RW_WIDTH:3 * RW_WIDTH]
    wa_lo = xs[:, 3 * RW_WIDTH:3 * RW_WIDTH + LANES]
    g_lo = xs[:, 3 * RW_WIDTH + LANES:RW_IN]
    z = w0_ref[...] + _dot_x3(jnp.tanh(wa_lo), w2_ref[...])
    sp = jnp.maximum(-z, 0.0) + jnp.log(1.0 + jnp.exp(-jnp.abs(z)))
    logw = -jnp.exp(-sp - 0.5)
    a = _sigmoid(a0_ref[...] + _dot_x3(wa_lo, a2_ref[...]))
    g = _dot_x3(_sigmoid(g_lo), g2_ref[...])
    if has_vres:
        gate = _sigmoid(v0_ref[...] + _dot_x3(_dot_x3(v, v1_ref[...]), v2_ref[...]))
        v = v + (vf_ref[...] - v) * gate
    else:
        vf_o[...] = v
    kk = k * kk_ref[...]
    ss = _dot_x2(kk * kk, grp_ref[...])
    inv = _dot_x2(lax.rsqrt(jnp.maximum(ss, 1e-24)), grpt_ref[...])
    kk = kk * inv
    k = k * (1.0 + (a - 1.0) * ka_ref[...])
    r_o[...] = r
    lw_o[...] = logw
    k_o[...] = k
    v_o[...] = v
    kkn_o[...] = kk
    b_o[...] = kk * a
    g_o[...] = g


def rwkv_prep(p_rw, prev, params, vres, shift_mode, seq_len, tm):
    t = p_rw.shape[0]
    assert t % tm == 0
    mu, w0, w2p, a0, a2p, g2, k_k, k_a = params
    grp, grpt = _group_matrices(RW_WIDTH, RW_HEAD)
    row = lambda a: a.reshape(1, -1)
    tile = lambda wd: pl.BlockSpec((tm, wd), lambda i: (i, 0))
    full = lambda a: pl.BlockSpec(a.shape, lambda i: (0,) * a.ndim)
    args = [p_rw]
    specs = [tile(RW_IN)]
    if shift_mode:
        r8 = tm // 8
        args.append(p_rw)
        specs.append(pl.BlockSpec((8, RW_IN), lambda i: (jnp.maximum(i * r8 - 1, 0), 0)))
    else:
        args.append(prev)
        specs.append(tile(RW_IN))
    small = [row(mu), row(w0), row(a0), row(k_k), row(k_a), w2p, a2p, g2, grp, grpt]
    args += small
    specs += [full(a) for a in small]
    has_vres = vres is not None
    if has_vres:
        vf, v0, v1, v2 = vres
        extra = [row(v0), v1, v2]
        args += [vf] + extra
        specs += [tile(RW_WIDTH)] + [full(a) for a in extra]
    n_out = 7 if has_vres else 8
    kern = functools.partial(_rwkv_prep_kernel, shift_mode=shift_mode, has_vres=has_vres,
                             tiles_per_seq=max(seq_len // tm, 1))
    outs = pl.pallas_call(
        kern,
        grid=(t // tm,),
        in_specs=specs,
        out_specs=[tile(RW_WIDTH)] * n_out,
        out_shape=[jax.ShapeDtypeStruct((t, RW_WIDTH), F32)] * n_out,
        compiler_params=_cparams("parallel"),
    )(*args)
    return outs


def _unit_lower_inverse(n, c):
    ri = lax.broadcasted_iota(jnp.int32, (c, c), 0)
    ci = lax.broadcasted_iota(jnp.int32, (c, c), 1)
    eye = (ri == ci).astype(F32)[None]
    blk = min(16, c)
    same = ((ri // blk) == (ci // blk))[None]
    nd = jnp.where(same, n, 0.0)
    x = eye - nd
    pw = nd
    span = 2
    while span < blk:
        pw = _bmm(pw, pw)
        x = _bmm(x, eye + pw)
        span *= 2
    if c > blk:
        p = _bmm(x, jnp.where(same, 0.0, n))
        y = eye - p
        pw = p
        span = 2
        while span < c // blk:
            pw = _bmm(pw, pw)
            y = _bmm(y, eye + pw)
            span *= 2
        x = _bmm(y, x)
    return x


def _rwkv_scan_kernel(r_ref, lw_ref, k_ref, v_ref, kk_ref, b_ref, g_ref, lnw_ref, lnb_ref, rk_ref,
                      s0_ref, tri_ref, o_ref, sout_ref, s_sc, *, chunk):
    l = pl.program_id(2)
    tl = r_ref.shape[1]
    c = chunk
    nc = tl // c
    hd = RW_HEAD

    @pl.when(l == 0)
    def _():
        s_sc[...] = s0_ref[0]

    ri = lax.broadcasted_iota(jnp.int32, (c, c), 0)
    ci = lax.broadcasted_iota(jnp.int32, (c, c), 1)
    tri_incl = (ri >= ci)[None]
    tri_strict = (ri > ci)[None]
    lw_all = lw_ref[0]
    lc_all = _split_dot(tri_ref[...], lw_all)

    for h in range(LANES // hd):
        sl = slice(h * hd, (h + 1) * hd)
        ch = lambda ref: ref[0][:, sl].reshape(nc, c, hd)
        r, k, v, kk, b = ch(r_ref), ch(k_ref), ch(v_ref), ch(kk_ref), ch(b_ref)
        lw = lw_all[:, sl].reshape(nc, c, hd)
        lc = lc_all[:, sl].reshape(nc, c, hd)
        tot = lc[:, c - 1:c, :]
        e_pos = jnp.exp(lc)
        e_neg = jnp.exp(-lc)
        e_end = jnp.exp(tot - lc)
        kk_t = kk * jnp.exp(lc - lw)
        r_t = r * e_pos
        b_t = b * e_neg
        k_t = k * e_neg
        a_ab = jnp.where(tri_strict, _bmm_nt(kk_t, b_t), 0.0)
        a_ak = jnp.where(tri_strict, _bmm_nt(kk_t, k_t), 0.0)
        a_rb = jnp.where(tri_incl, _bmm_nt(r_t, b_t), 0.0)
        a_rk = jnp.where(tri_incl, _bmm_nt(r_t, k_t), 0.0)
        tinv = _unit_lower_inverse(a_ab, c)
        wm = _bmm(tinv, kk_t)
        u0 = -_bmm(tinv, _bmm(a_ak, v))
        q_eff = r_t - _bmm(a_rb, wm)
        o0 = _bmm(a_rk, v) + _bmm(a_rb, u0)
        b_g = b * e_end
        k_g = k * e_end
        kr = lax.broadcasted_iota(jnp.int32, (hd, hd), 0)
        kc = lax.broadcasted_iota(jnp.int32, (hd, hd), 1)
        eye = (kr == kc).astype(F32)[None]
        m_mat = eye * jnp.exp(tot) - _bmm_tn(wm, b_g)
        d_mat = _bmm_tn(u0, b_g) + _bmm_tn(v, k_g)
        s = s_sc[h]
        states = []
        for i in range(nc):
            states.append(s)
            s = _dot(_bf(s), _bf(m_mat[i])) + d_mat[i]
        s_sc[h] = s
        o = o0 + _bmm_nt(q_eff, jnp.stack(states))
        o = o.reshape(tl, hd)
        mean = jnp.mean(o, axis=-1, keepdims=True)
        var = jnp.mean(jnp.square(o - mean), axis=-1, keepdims=True)
        o = (o - mean) * lax.rsqrt(var + GN_EPS) * lnw_ref[:, sl] + lnb_ref[:, sl]
        r2 = r_ref[0][:, sl]
        k2 = k_ref[0][:, sl]
        bonus = jnp.sum(r2 * k2 * rk_ref[:, sl], axis=-1, keepdims=True) * v_ref[0][:, sl]
        o_ref[0, :, sl] = (o + bonus) * g_ref[0][:, sl]

    @pl.when(l == pl.num_programs(2) - 1)
    def _():
        sout_ref[0] = s_sc[...]


def rwkv_scan(r, lw, k, v, kk, b, g, ln_w, ln_b, r_k, s0, tl, chunk):
    bsz, seq, _ = r.shape
    assert seq % tl == 0 and tl % chunk == 0
    hp = RW_WIDTH // LANES
    per = LANES // RW_HEAD
    seq_spec = pl.BlockSpec((1, tl, LANES), lambda bi, hi, li: (bi, li, hi))
    par_spec = pl.BlockSpec((1, LANES), lambda bi, hi, li: (0, hi))
    st_spec = pl.BlockSpec((1, per, RW_HEAD, RW_HEAD), lambda bi, hi, li: (bi, hi, 0, 0))
    row = lambda a: a.reshape(1, RW_WIDTH)
    idx = np.arange(tl)
    tri = jnp.asarray(((idx[:, None] >= idx[None, :]) & (idx[:, None] // chunk == idx[None, :] // chunk)),
                      dtype=BF16)
    return pl.pallas_call(
        functools.partial(_rwkv_scan_kernel, chunk=chunk),
        grid=(bsz, hp, seq // tl),
        in_specs=[seq_spec] * 7 + [par_spec] * 3 + [st_spec, pl.BlockSpec((tl, tl), lambda bi, hi, li: (0, 0))],
        out_specs=[seq_spec, st_spec],
        out_shape=[jax.ShapeDtypeStruct((bsz, seq, RW_WIDTH), F32),
                   jax.ShapeDtypeStruct((bsz, RW_HEADS, RW_HEAD, RW_HEAD), F32)],
        scratch_shapes=[pltpu.VMEM((per, RW_HEAD, RW_HEAD), F32)],
        compiler_params=_cparams("parallel", "parallel", "arbitrary"),
    )(r, lw, k, v, kk, b, g, row(ln_w), row(ln_b), row(r_k), s0, tri)


def _moba_prep_kernel(p_ref, qg_ref, kg_ref, grp_ref, grpt_ref, q_o, k_o, kb_o, vt_o, ks_o):
    p = p_ref[...]
    q = p[:, 0:MB_WIDTH]
    k = p[:, MB_WIDTH:2 * MB_WIDTH]
    v = p[:, 2 * MB_WIDTH:3 * MB_WIDTH]

    def head_norm(x, gain_ref):
        ms = _dot_x2(x * x, grp_ref[...]) * (1.0 / MB_HEAD)
        inv = _dot_x2(lax.rsqrt(ms + RMS_EPS), grpt_ref[...])
        return x * inv * gain_ref[...]

    qn = head_norm(q, qg_ref)
    kn = head_norm(k, kg_ref)
    q_o[...] = qn
    k_o[...] = kn
    kb_o[...] = kn.astype(BF16)
    vt_o[...] = v.T.astype(BF16)
    ks_o[0] = jnp.sum(kn, axis=0, keepdims=True)


def moba_prep(p_mb, q_gain, k_gain, tm):
    t = p_mb.shape[0]
    assert t % tm == 0
    grp, grpt = _group_matrices(MB_WIDTH, MB_HEAD)
    tile = lambda i: (i, 0)
    full = lambda a: pl.BlockSpec(a.shape, lambda i: (0,) * a.ndim)
    qg = jnp.tile(q_gain, MB_HEADS).reshape(1, MB_WIDTH)
    kg = jnp.tile(k_gain, MB_HEADS).reshape(1, MB_WIDTH)
    return pl.pallas_call(
        _moba_prep_kernel,
        grid=(t // tm,),
        in_specs=[pl.BlockSpec((tm, 3 * MB_WIDTH), tile), full(qg), full(kg), full(grp), full(grpt)],
        out_specs=[pl.BlockSpec((tm, MB_WIDTH), tile)] * 3
                  + [pl.BlockSpec((MB_WIDTH, tm), lambda i: (0, i)),
                     pl.BlockSpec((1, 1, MB_WIDTH), lambda i: (i, 0, 0))],
        out_shape=[jax.ShapeDtypeStruct((t, MB_WIDTH), F32), jax.ShapeDtypeStruct((t, MB_WIDTH), F32),
                   jax.ShapeDtypeStruct((t, MB_WIDTH), BF16), jax.ShapeDtypeStruct((MB_WIDTH, t), BF16),
                   jax.ShapeDtypeStruct((t // tm, 1, MB_WIDTH), F32)],
        compiler_params=_cparams("parallel"),
    )(p_mb, qg, kg, grp, grpt)


def _top_blocks(gate, idx, axis):
    sel = jnp.zeros_like(gate)
    rem = gate
    for _ in range(MOBA_TOPK):
        mx = jnp.max(rem, axis=axis, keepdims=True)
        live = jnp.logical_and(rem == mx, rem > 0.5 * MASK_NEG)
        first = jnp.min(jnp.where(live, idx, 1e9), axis=axis, keepdims=True)
        pick = idx == first
        sel = jnp.where(pick, 1.0, sel)
        rem = jnp.where(pick, MASK_NEG, rem)
    return sel


def _moba_attn_kernel(q_ref, k_ref, vt_ref, ks_ref, o_ref, s_a, s_b, p_a, p_b):
    qi = pl.program_id(2)
    tq = q_ref.shape[1]
    nb = ks_ref.shape[1]
    n_heads = LANES // MB_HEAD
    q = q_ref[0]
    lane = lax.broadcasted_iota(jnp.int32, (1, LANES), 1)
    bmean = ks_ref[0] * (1.0 / MOBA_BLOCK)
    blk_f = lax.broadcasted_iota(jnp.int32, (nb, tq), 0).astype(F32)
    qi_f = qi.astype(F32)
    q_cols, bias_cols = [], []
    pad_rows = jnp.full((LANES - nb, tq), MASK_NEG, F32)
    for h in range(n_heads):
        q_h = jnp.where((lane // MB_HEAD) == h, q, 0.0)
        gate_t = _dot_nt(bmean, q_h, HI)
        gate_t = jnp.where(blk_f < qi_f, gate_t, MASK_NEG)
        bias = (1.0 - _top_blocks(gate_t, blk_f, 0)) * MASK_NEG
        bias_cols.append(jnp.concatenate([bias, pad_rows], axis=0).T)
        q_cols.append(q_h * (LOG2E * MB_HEAD ** -0.5))
    ncol = n_heads * tq
    q_all = _bf(jnp.concatenate(q_cols, axis=0))
    q_ext = jnp.concatenate([q_all, _bf(jnp.concatenate(bias_cols, axis=0))], axis=1)

    def ones_rows(n):
        return _bf(jnp.where(lax.broadcasted_iota(jnp.int32, (16, n), 0) == 0, 1.0, 0.0))

    row0 = pl.multiple_of(qi * tq, tq)
    ri = lax.broadcasted_iota(jnp.int32, (tq, ncol), 0)
    ci = lax.broadcasted_iota(jnp.int32, (tq, ncol), 1)
    causal_t = ri <= (ci % tq)
    s = jnp.where(causal_t, _dot_nt(k_ref[0, pl.ds(row0, tq), :], q_all), MASK_NEG)
    m = jnp.max(s, axis=0, keepdims=True)
    p = _bf(jnp.exp2(s - m))
    acc = jnp.concatenate(
        [_dot(jnp.concatenate([vt_ref[h * MB_HEAD:(h + 1) * MB_HEAD, pl.ds(row0, tq)], ones_rows(tq)], axis=0),
              p[:, h * tq:(h + 1) * tq]) for h in range(n_heads)], axis=1)

    lane_i = lax.broadcasted_iota(jnp.int32, (1, LANES), 1)

    def pair_rows(i):
        return pl.multiple_of(jnp.minimum(2 * i, nb - 2) * tq, tq)

    def scores(i, s_ref):
        k2 = k_ref[0, pl.ds(pair_rows(i), 2 * tq), :]
        onehot = [jnp.broadcast_to(_bf(jnp.where(lane_i == 2 * i + u, 1.0, 0.0)), (tq, LANES)) for u in range(2)]
        k_ext = jnp.concatenate([k2, jnp.concatenate(onehot, axis=0)], axis=1)
        s_ref[...] = _dot_nt(k_ext, q_ext)

    def values(i, p_ref):
        r0 = pair_rows(i)
        out = []
        for h in range(n_heads):
            vt_h = jnp.concatenate([vt_ref[h * MB_HEAD:(h + 1) * MB_HEAD, pl.ds(r0, 2 * tq)], ones_rows(2 * tq)],
                                   axis=0)
            out.append(_dot(vt_h, p_ref[:, h * tq:(h + 1) * tq]))
        return jnp.concatenate(out, axis=1)

    def softmax(s_ref, p_ref, m):
        s = s_ref[...]
        m_new = jnp.maximum(m, jnp.max(s, axis=0, keepdims=True))
        p_ref[...] = _bf(jnp.exp2(s - m_new))
        return m_new, jnp.exp2(m - m_new)

    scores(0, s_a)
    p_b[...] = jnp.zeros_like(p_b)

    def body(t, carry):
        m, acc = carry
        scores(2 * t + 1, s_b)
        pv = values(jnp.maximum(2 * t - 1, 0), p_b)
        m, alpha = softmax(s_a, p_a, m)
        acc = (acc + pv) * alpha
        scores(2 * t + 2, s_a)
        pv = values(2 * t, p_a)
        m, alpha = softmax(s_b, p_b, m)
        acc = (acc + pv) * alpha
        return m, acc

    trips = (qi + 3) // 4
    m, acc = lax.fori_loop(0, trips, body, (m, acc))
    acc = acc + values(jnp.maximum(2 * trips - 1, 0), p_b)
    o = acc[:MB_HEAD] / acc[MB_HEAD:MB_HEAD + 1]
    o_ref[0] = jnp.concatenate([o[:, h * tq:(h + 1) * tq] for h in range(n_heads)], axis=0).T


def moba_attn(qn, kb, vt, ksum):
    bsz, seq, _ = qn.shape
    tq = MOBA_BLOCK
    nb = seq // tq
    assert nb >= 2
    hp = MB_WIDTH // LANES
    ncol = (LANES // MB_HEAD) * tq
    return pl.pallas_call(
        _moba_attn_kernel,
        grid=(bsz, hp, nb),
        in_specs=[pl.BlockSpec((1, tq, LANES), lambda b, h, i: (b, i, h)),
                  pl.BlockSpec((1, seq, LANES), lambda b, h, i: (b, 0, h)),
                  pl.BlockSpec((LANES, seq), lambda b, h, i: (h, b)),
                  pl.BlockSpec((1, nb, LANES), lambda b, h, i: (b, 0, h))],
        out_specs=pl.BlockSpec((1, tq, LANES), lambda b, h, i: (b, i, h)),
        out_shape=jax.ShapeDtypeStruct((bsz, seq, MB_WIDTH), F32),
        scratch_shapes=[pltpu.VMEM((2 * tq, ncol), F32), pltpu.VMEM((2 * tq, ncol), F32),
                        pltpu.VMEM((2 * tq, ncol), BF16), pltpu.VMEM((2 * tq, ncol), BF16)],
        compiler_params=_cparams("parallel", "parallel", "arbitrary"),
    )(qn, kb, vt, ksum)


def _moba_scores_kernel(pt_ref, q_ref, *refs, n_pg):
    k_refs = refs[:n_pg]
    s_o = refs[n_pg]
    q = q_ref[0]
    for j in range(n_pg):
        s_o[0, j] = jnp.sum(k_refs[j][0, 0] * q, axis=1)


def _moba_select_kernel(s_ref, q_ref, kn_ref, p_o, w_o, id_o, *, ppb):
    scale = MB_HEAD ** -0.5
    s = s_ref[0]
    n_lp = s.shape[0]
    nblk = n_lp // ppb
    page_sum = jnp.sum(s, axis=-1, keepdims=True)
    gate = jnp.sum(page_sum.reshape(nblk, ppb, MB_HEADS, 1), axis=1) * (1.0 / MOBA_BLOCK)
    idx = lax.broadcasted_iota(jnp.int32, gate.shape, 0).astype(F32)
    lane = lax.broadcasted_iota(jnp.int32, (MB_HEADS, LANES), 1)
    sel = jnp.zeros_like(gate)
    rem = gate
    ids = jnp.zeros((MB_HEADS, LANES), F32)
    for t in range(MOBA_TOPK):
        mx = jnp.max(rem, axis=0, keepdims=True)
        first = jnp.min(jnp.where(rem == mx, idx, 1e9), axis=0, keepdims=True)
        pick = idx == first
        sel = jnp.where(pick, 1.0, sel)
        rem = jnp.where(pick, MASK_NEG, rem)
        for u in range(ppb):
            ids = jnp.where(lane == t * ppb + u, first[0] * ppb + u, ids)
    id_o[0] = ids.astype(jnp.int32)
    sel_pg = jnp.broadcast_to(sel[:, None], (nblk, ppb, MB_HEADS, 1)).reshape(n_lp, MB_HEADS, 1) > 0.5
    s_own = jnp.sum(kn_ref[0] * q_ref[0], axis=-1, keepdims=True) * scale
    sm = jnp.where(sel_pg, s * scale, MASK_NEG)
    m_all = jnp.maximum(jnp.max(jnp.max(sm, axis=0), axis=-1, keepdims=True), s_own)
    p = jnp.where(sel_pg, jnp.exp(sm - m_all[None]), 0.0)
    w_own = jnp.exp(s_own - m_all)
    inv = 1.0 / (jnp.sum(jnp.sum(p, axis=0), axis=-1, keepdims=True) + w_own)
    p_o[0] = p * inv[None]
    w_o[0] = jnp.broadcast_to(w_own * inv, (MB_HEADS, LANES))


def _moba_values_kernel(pt_ref, id_ref, p_ref, w_ref, vn_ref, *refs, n_sel, hps):
    v_refs = refs[:hps * n_sel]
    o_ref = refs[hps * n_sel]
    b = pl.program_id(0)
    for hh in range(hps):
        h = pl.program_id(1) * hps + hh
        acc = w_ref[0, pl.ds(h, 1), 0:1] * vn_ref[0, hh]
        for j in range(n_sel):
            page = id_ref[(b * MB_HEADS + h) * n_sel + j]
            prob = p_ref[0, page, pl.ds(h, 1), :]
            acc = acc + _dot_nt(prob, v_refs[hh * n_sel + j][0, 0, 0], HI)
        o_ref[0, hh] = acc


def moba_decode(qn, kn, vn, k_cache_t, v_cache_t, page_table, layer):
    bsz = qn.shape[0]
    n_lp = page_table.shape[1]
    ppb = MOBA_BLOCK // PAGE_SIZE
    assert n_lp % ppb == 0 and n_lp // ppb >= MOBA_TOPK
    n_pg = 16
    assert n_lp % n_pg == 0
    n_sel = MOBA_TOPK * ppb
    hps = 4
    heads = lambda a: a.reshape(bsz, MB_HEADS, MB_HEAD)
    page_blk = (1, 1, MB_HEADS, MB_HEAD, PAGE_SIZE)

    scores = pl.pallas_call(
        functools.partial(_moba_scores_kernel, n_pg=n_pg),
        grid_spec=pltpu.PrefetchScalarGridSpec(
            num_scalar_prefetch=1,
            grid=(bsz, n_lp // n_pg),
            in_specs=[pl.BlockSpec((1, MB_HEADS, MB_HEAD, 1), lambda b, g, pt: (b, 0, 0, 0))]
                     + [pl.BlockSpec(page_blk, functools.partial(
                         lambda b, g, pt, j: (layer, pt[b, g * n_pg + j], 0, 0, 0), j=j)) for j in range(n_pg)],
            out_specs=pl.BlockSpec((1, n_pg, MB_HEADS, PAGE_SIZE), lambda b, g, pt: (b, g, 0, 0))),
        out_shape=jax.ShapeDtypeStruct((bsz, n_lp, MB_HEADS, PAGE_SIZE), F32),
        compiler_params=_cparams("parallel", "arbitrary"),
    )(page_table, qn.reshape(bsz, MB_HEADS, MB_HEAD, 1), *([k_cache_t] * n_pg))

    vec = pl.BlockSpec((1, MB_HEADS, MB_HEAD), lambda b: (b, 0, 0))
    probs, w_own, ids = pl.pallas_call(
        functools.partial(_moba_select_kernel, ppb=ppb),
        grid=(bsz,),
        in_specs=[pl.BlockSpec((1, n_lp, MB_HEADS, PAGE_SIZE), lambda b: (b, 0, 0, 0)), vec, vec],
        out_specs=[pl.BlockSpec((1, n_lp, MB_HEADS, PAGE_SIZE), lambda b: (b, 0, 0, 0)),
                   pl.BlockSpec((1, MB_HEADS, LANES), lambda b: (b, 0, 0)),
                   pl.BlockSpec((1, MB_HEADS, LANES), lambda b: (b, 0, 0))],
        out_shape=[jax.ShapeDtypeStruct((bsz, n_lp, MB_HEADS, PAGE_SIZE), F32),
                   jax.ShapeDtypeStruct((bsz, MB_HEADS, LANES), F32),
                   jax.ShapeDtypeStruct((bsz, MB_HEADS, LANES), jnp.int32)],
        compiler_params=_cparams("parallel"),
    )(scores, heads(qn), heads(kn))

    sel_ids = ids[:, :, :n_sel].reshape(-1)
    out = pl.pallas_call(
        functools.partial(_moba_values_kernel, n_sel=n_sel, hps=hps),
        grid_spec=pltpu.PrefetchScalarGridSpec(
            num_scalar_prefetch=2,
            grid=(bsz, MB_HEADS // hps),
            in_specs=[pl.BlockSpec((1, n_lp, MB_HEADS, PAGE_SIZE), lambda b, g, pt, sid: (b, 0, 0, 0)),
                      pl.BlockSpec((1, MB_HEADS, LANES), lambda b, g, pt, sid: (b, 0, 0)),
                      pl.BlockSpec((1, hps, 1, MB_HEAD), lambda b, g, pt, sid: (b, g, 0, 0))]
                     + [pl.BlockSpec((1, 1, 1, MB_HEAD, PAGE_SIZE), functools.partial(
                         lambda b, g, pt, sid, hh, j: (
                             layer, pt[b, sid[(b * MB_HEADS + g * hps + hh) * n_sel + j]], g * hps + hh, 0, 0),
                         hh=hh, j=j)) for hh in range(hps) for j in range(n_sel)],
            out_specs=pl.BlockSpec((1, hps, 1, MB_HEAD), lambda b, g, pt, sid: (b, g, 0, 0))),
        out_shape=jax.ShapeDtypeStruct((bsz, MB_HEADS, 1, MB_HEAD), F32),
        compiler_params=_cparams("parallel", "arbitrary"),
    )(page_table, sel_ids, probs, w_own, vn.reshape(bsz, MB_HEADS, 1, MB_HEAD), *([v_cache_t] * (hps * n_sel)))
    return out.reshape(bsz, MB_WIDTH)


def _hgrn_constants(c):
    rows = np.arange(c)
    dmats, ups, los, masks = [], [], [], []
    z = c
    while z >= 2:
        seg = rows // z
        mid = seg * z + z // 2
        up = rows >= mid
        d = np.zeros((c, c), np.float32)
        for i in rows:
            if up[i]:
                d[i, mid[i]:i + 1] = 1.0
            else:
                d[i, i + 1:mid[i]] = 1.0
        dmats.append(d)
        ups.append(np.repeat(up[:, None], HG_D, 1).astype(np.float32))
        los.append(np.repeat(~up[:, None], HG_D, 1).astype(np.float32))
        masks.append(((seg[:, None] == seg[None, :]) & up[:, None] & (~up)[None, :]).astype(np.float32))
        z //= 2
    tri = np.tril(np.ones((c, c), np.float32))
    dmats += [tri, 1.0 - tri]
    return (jnp.asarray(np.concatenate(dmats, 0), dtype=BF16), jnp.asarray(np.stack(ups)),
            jnp.asarray(np.stack(los)), jnp.asarray(np.stack(masks)))


def _hgrn_kernel(pq_ref, pf_ref, pi_ref, pg_ref, lbl_ref, gain_ref, s0_ref, dd_ref, up_ref, lo_ref,
                 mk_ref, o_ref, sout_ref, st_sc, *, chunk, layer):
    l = pl.program_id(2)
    tl = pq_ref.shape[1]
    c = chunk
    nc = tl // c
    nlev = up_ref.shape[0]
    er = lax.broadcasted_iota(jnp.int32, (HG_D, HG_D), 0)
    ec = lax.broadcasted_iota(jnp.int32, (HG_D, HG_D), 1)
    eye_d = (er == ec).astype(F32)

    @pl.when(l == 0)
    def _():
        st_sc[...] = _dot_nt(eye_d, s0_ref[0, 0], HI)

    logits = lbl_ref[...]
    ex = jnp.exp(logits - jnp.max(logits, axis=0, keepdims=True))
    prob = ex / jnp.sum(ex, axis=0, keepdims=True)
    lb = jnp.sum(prob[0:layer + 1], axis=0, keepdims=True) - prob[0:1]

    qr = pq_ref[0]
    q = qr * _sigmoid(qr)
    forget = lb + (1.0 - lb) * _sigmoid(pf_ref[0])
    logf = jnp.log(jnp.maximum(forget, 1e-30))
    kg = 1.0 - forget
    v = pi_ref[0]

    ri = lax.broadcasted_iota(jnp.int32, (c, c), 0)
    ci = lax.broadcasted_iota(jnp.int32, (c, c), 1)
    diag = ri == ci
    rows = [slice(i * c, (i + 1) * c) for i in range(nc)]
    e_all = jnp.exp(_split_dot(dd_ref[...], jnp.concatenate([logf[rs] for rs in rows], axis=1)))
    intra, kv, qe, dec = [], [], [], []
    for i, rs in enumerate(rows):
        qc, kc, vc = q[rs], kg[rs], v[rs]
        e = e_all[:, i * HG_D:(i + 1) * HG_D]
        a = jnp.where(diag, jnp.sum(qc * kc, axis=1, keepdims=True), 0.0)
        for lev in range(nlev):
            el = e[lev * c:(lev + 1) * c]
            a = a + mk_ref[lev] * _dot_nt(_bf(qc * el * up_ref[lev]), _bf(kc * el * lo_ref[lev]))
        e_in = e[nlev * c:(nlev + 1) * c]
        e_out = e[(nlev + 1) * c:(nlev + 2) * c]
        intra.append(_dot(_bf(a), _bf(vc)))
        kv.append(_dot_tn(_bf(vc), _bf(kc * e_out)))
        qe.append(_bf(qc * e_in))
        dec.append(e_in[c - 1:c, :])
    st = st_sc[...]
    outs = []
    for i in range(nc):
        outs.append(_dot_nt(qe[i], _bf(st)) + intra[i])
        st = st * dec[i] + kv[i]
    st_sc[...] = st
    o = jnp.concatenate(outs, axis=0) if nc > 1 else outs[0]
    ms = jnp.mean(o * o, axis=-1, keepdims=True)
    gr = pg_ref[0]
    o_ref[0] = o * lax.rsqrt(ms + RMS_EPS) * gain_ref[...] * (gr * _sigmoid(gr))

    @pl.when(l == pl.num_programs(2) - 1)
    def _():
        sout_ref[0, 0] = _dot_nt(eye_d, st, HI)


def hgrn_scan(p, lb_logits, out_gain, s0, layer, tl, chunk):
    bsz, seq, _ = p.shape
    assert seq % tl == 0 and tl % chunk == 0
    dd, up, lo, mk = _hgrn_constants(chunk)
    col = lambda off: pl.BlockSpec((1, tl, HG_D), lambda b, h, li: (b, li, off + h))
    full = lambda a: pl.BlockSpec(a.shape, lambda b, h, li: (0,) * a.ndim)
    st_spec = pl.BlockSpec((1, 1, HG_D, HG_D), lambda b, h, li: (b, h, 0, 0))
    return pl.pallas_call(
        functools.partial(_hgrn_kernel, chunk=chunk, layer=layer),
        grid=(bsz, HG_HEADS, seq // tl),
        in_specs=[col(0), col(HG_HEADS), col(2 * HG_HEADS), col(3 * HG_HEADS),
                  pl.BlockSpec((lb_logits.shape[0], HG_D), lambda b, h, li: (0, h)),
                  pl.BlockSpec((1, HG_D), lambda b, h, li: (0, 0)),
                  st_spec, full(dd), full(up), full(lo), full(mk)],
        out_specs=[pl.BlockSpec((1, tl, HG_D), lambda b, h, li: (b, li, h)), st_spec],
        out_shape=[jax.ShapeDtypeStruct((bsz, seq, D_MODEL), F32),
                   jax.ShapeDtypeStruct((bsz, HG_HEADS, HG_D, HG_D), F32)],
        scratch_shapes=[pltpu.VMEM((HG_D, HG_D), F32)],
        compiler_params=_cparams("parallel", "parallel", "arbitrary"),
    )(p, p, p, p, lb_logits, out_gain.reshape(1, HG_D), s0, dd, up, lo, mk)


def _row_to_col(row, eye):
    return jnp.sum(jnp.where(eye, row, 0.0), axis=1, keepdims=True)


def _col_to_row(col, eye):
    return jnp.sum(jnp.where(eye, col, 0.0), axis=0, keepdims=True)


def _rwkv_step_kernel(r_ref, lw_ref, k_ref, v_ref, kk_ref, b_ref, g_ref, lnw_ref, lnb_ref, rk_ref,
                      s0_ref, o_ref, sout_ref):
    hd = RW_HEAD
    ri = lax.broadcasted_iota(jnp.int32, (hd, hd), 0)
    ci = lax.broadcasted_iota(jnp.int32, (hd, hd), 1)
    eye = ri == ci
    for h in range(RW_HEADS):
        sl = slice(h * hd, (h + 1) * hd)
        row = lambda ref: ref[0][:, sl]
        r, k, v, kk, b = row(r_ref), row(k_ref), row(v_ref), row(kk_ref), row(b_ref)
        s = s0_ref[0, h]
        s_kk = jnp.sum(s * kk, axis=1, keepdims=True)
        s = s * jnp.exp(row(lw_ref)) - s_kk * b + _row_to_col(v, eye) * k
        sout_ref[0, h] = s
        o = _col_to_row(jnp.sum(s * r, axis=1, keepdims=True), eye)
        mean = jnp.mean(o, axis=-1, keepdims=True)
        var = jnp.mean(jnp.square(o - mean), axis=-1, keepdims=True)
        o = (o - mean) * lax.rsqrt(var + GN_EPS) * lnw_ref[:, sl] + lnb_ref[:, sl]
        bonus = jnp.sum(r * k * rk_ref[:, sl], axis=-1, keepdims=True) * v
        o_ref[0, :, sl] = (o + bonus) * row(g_ref)


def rwkv_step(r, lw, k, v, kk, b, g, ln_w, ln_b, r_k, s0):
    bsz = r.shape[0]
    seq_spec = pl.BlockSpec((1, 1, RW_WIDTH), lambda i: (i, 0, 0))
    par_spec = pl.BlockSpec((1, RW_WIDTH), lambda i: (0, 0))
    st_spec = pl.BlockSpec((1, RW_HEADS, RW_HEAD, RW_HEAD), lambda i: (i, 0, 0, 0))
    row = lambda a: a.reshape(1, RW_WIDTH)
    tok = lambda a: a.reshape(bsz, 1, RW_WIDTH)
    o, s = pl.pallas_call(
        _rwkv_step_kernel,
        grid=(bsz,),
        in_specs=[seq_spec] * 7 + [par_spec] * 3 + [st_spec],
        out_specs=[seq_spec, st_spec],
        out_shape=[jax.ShapeDtypeStruct((bsz, 1, RW_WIDTH), F32),
                   jax.ShapeDtypeStruct((bsz, RW_HEADS, RW_HEAD, RW_HEAD), F32)],
        compiler_params=_cparams("parallel"),
    )(*[tok(a) for a in (r, lw, k, v, kk, b, g)], row(ln_w), row(ln_b), row(r_k), s0)
    return o.reshape(bsz, RW_WIDTH), s


def _hgrn_step_kernel(p_ref, lbl_ref, gain_ref, s0_ref, o_ref, sout_ref, *, layer):
    d = HG_D
    ri = lax.broadcasted_iota(jnp.int32, (d, d), 0)
    ci = lax.broadcasted_iota(jnp.int32, (d, d), 1)
    eye = ri == ci
    logits = lbl_ref[...]
    ex = jnp.exp(logits - jnp.max(logits, axis=0, keepdims=True))
    prob = ex / jnp.sum(ex, axis=0, keepdims=True)
    lb_all = jnp.sum(prob[0:layer + 1], axis=0, keepdims=True) - prob[0:1]
    width = HG_HEADS * d
    for h in range(HG_HEADS):
        col = lambda part: p_ref[0][:, part * width + h * d:part * width + (h + 1) * d]
        qr, fr, v, gr = col(0), col(1), col(2), col(3)
        lb = lb_all[:, h * d:(h + 1) * d]
        q = qr * _sigmoid(qr)
        forget = lb + (1.0 - lb) * _sigmoid(fr)
        decay = jnp.maximum(forget, 1e-30)
        s = _row_to_col(decay, eye) * s0_ref[0, h] + _row_to_col(1.0 - forget, eye) * v
        sout_ref[0, h] = s
        o = jnp.sum(_row_to_col(q, eye) * s, axis=0, keepdims=True)
        ms = jnp.mean(o * o, axis=-1, keepdims=True)
        o_ref[0, :, h * d:(h + 1) * d] = o * lax.rsqrt(ms + RMS_EPS) * gain_ref[...] * (gr * _sigmoid(gr))


def hgrn_step(p, lb_logits, out_gain, s0, layer):
    bsz = p.shape[0]
    st_spec = pl.BlockSpec((1, HG_HEADS, HG_D, HG_D), lambda i: (i, 0, 0, 0))
    o, s = pl.pallas_call(
        functools.partial(_hgrn_step_kernel, layer=layer),
        grid=(bsz,),
        in_specs=[pl.BlockSpec((1, 1, p.shape[1]), lambda i: (i, 0, 0)),
                  pl.BlockSpec(lb_logits.shape, lambda i: (0, 0)),
                  pl.BlockSpec((1, HG_D), lambda i: (0, 0)), st_spec],
        out_specs=[pl.BlockSpec((1, 1, D_MODEL), lambda i: (i, 0, 0)), st_spec],
        out_shape=[jax.ShapeDtypeStruct((bsz, 1, D_MODEL), F32),
                   jax.ShapeDtypeStruct((bsz, HG_HEADS, HG_D, HG_D), F32)],
        compiler_params=_cparams("parallel"),
    )(p.reshape(bsz, 1, -1), lb_logits, out_gain.reshape(1, HG_D), s0)
    return o.reshape(bsz, D_MODEL), s


def _even_layer(xp, xs, il, vf_p, vf_s, w, cache_k, cache_v, page_table, state_rwkv, state_shift):
    nb_p, seq, d = xp.shape
    nb_s = xs.shape[0]
    tp = nb_p * seq
    rw_params = (w['rw_mu'][il], w['rw_w0'][il], w['w2p'][il], w['rw_a0'][il], w['a2p'][il],
                 w['rw_g2'][il], w['rw_k_k'][il], w['rw_k_a'][il])
    has_vres = il > 0
    vres_w = (w['rw_v0'][il - 1], w['rw_v1'][il - 1], w['rw_v2'][il - 1]) if has_vres else None
    ln = (w['rw_ln_w'][il], w['rw_ln_b'][il], w['rw_r_k'][il].reshape(-1))

    p_rw, p_mb = norm_matmul(xp.reshape(tp, d), w['norm_mix'][2 * il], w['w_in_even'][il],
                             (RW_IN, 3 * MB_WIDTH), 512)
    vres = (vf_p,) + vres_w if has_vres else None
    prep = rwkv_prep(p_rw, None, rw_params, vres, True, seq, 512)
    if not has_vres:
        vf_p = prep[7]
    seq3 = lambda a: a.reshape(nb_p, seq, -1)
    o_rw, wkv_p = rwkv_scan(*[seq3(a) for a in prep[:7]], *ln,
                            jnp.zeros((nb_p, RW_HEADS, RW_HEAD, RW_HEAD), F32), 1024, RW_CHUNK)
    qn, kn, kb, vt, ksum = moba_prep(p_mb, w['mb_q_norm'][il], w['mb_k_norm'][il], MOBA_BLOCK)
    o_mb = moba_attn(seq3(qn), seq3(kb), vt, ksum.reshape(nb_p, seq // MOBA_BLOCK, MB_WIDTH))
    lyr = 2 * il
    yp = mix_ffn(xp.reshape(tp, d), [o_rw.reshape(tp, RW_WIDTH), o_mb.reshape(tp, MB_WIDTH)],
                 [w['w_out_even'][il][:RW_WIDTH], w['w_out_even'][il][RW_WIDTH:]],
                 w['norm_ffn'][lyr], w['ffn_w_gate'][lyr], w['ffn_w_up'][lyr], w['ffn_w_down'][lyr],
                 512, D_FF // 2).reshape(nb_p, seq, d)
    mk_p = kn.reshape(nb_p, seq, MB_HEADS, MB_HEAD)
    mv_p = p_mb[:, 2 * MB_WIDTH:].reshape(nb_p, seq, MB_HEADS, MB_HEAD)
    sh_p = p_rw.reshape(nb_p, seq, RW_IN)[:, -1]

    ps_rw, ps_mb = norm_matmul(xs.reshape(nb_s, d), w['norm_mix'][2 * il], w['w_in_even'][il],
                               (RW_IN, 3 * MB_WIDTH), nb_s)
    vres = (vf_s,) + vres_w if has_vres else None
    prep = rwkv_prep(ps_rw, state_shift[il], rw_params, vres, False, 1, nb_s)
    if not has_vres:
        vf_s = prep[7]
    os_rw, wkv_s = rwkv_step(*prep[:7], *ln, state_rwkv[il])
    qs, ks, _, _, _ = moba_prep(ps_mb, w['mb_q_norm'][il], w['mb_k_norm'][il], nb_s)
    vs = ps_mb[:, 2 * MB_WIDTH:]
    os_mb = moba_decode(qs, ks, vs, cache_k, cache_v, page_table, il)
    ys = mix_ffn(xs.reshape(nb_s, d), [os_rw, os_mb],
                 [w['w_out_even'][il][:RW_WIDTH], w['w_out_even'][il][RW_WIDTH:]],
                 w['norm_ffn'][lyr], w['ffn_w_gate'][lyr], w['ffn_w_up'][lyr], w['ffn_w_down'][lyr],
                 nb_s, D_FF // 2).reshape(nb_s, 1, d)
    mk_s = ks.reshape(nb_s, 1, MB_HEADS, MB_HEAD)
    mv_s = vs.reshape(nb_s, 1, MB_HEADS, MB_HEAD)
    return yp, ys, vf_p, vf_s, (mk_p, mv_p, mk_s, mv_s, wkv_p, wkv_s, sh_p, ps_rw)


def _odd_layer(xp, xs, il, w, state_hgrn):
    nb_p, seq, d = xp.shape
    nb_s = xs.shape[0]
    tp = nb_p * seq
    lyr = 2 * il + 1
    (pp,) = norm_matmul(xp.reshape(tp, d), w['norm_mix'][lyr], w['w_in_odd'][il], (4 * d,), 512)
    o_p, hg_p = hgrn_scan(pp.reshape(nb_p, seq, 4 * d), w['hg_lb_logits'], w['hg_out_norm'][il],
                          jnp.zeros((nb_p, HG_HEADS, HG_D, HG_D), F32), il, 1024, HG_CHUNK)
    yp = mix_ffn(xp.reshape(tp, d), [o_p.reshape(tp, d)], [w['w_out_odd'][il]],
                 w['norm_ffn'][lyr], w['ffn_w_gate'][lyr], w['ffn_w_up'][lyr], w['ffn_w_down'][lyr],
                 512, D_FF // 2).reshape(nb_p, seq, d)
    (ps,) = norm_matmul(xs.reshape(nb_s, d), w['norm_mix'][lyr], w['w_in_odd'][il], (4 * d,), nb_s)
    o_s, hg_s = hgrn_step(ps, w['hg_lb_logits'], w['hg_out_norm'][il], state_hgrn[il], il)
    ys = mix_ffn(xs.reshape(nb_s, d), [o_s], [w['w_out_odd'][il]],
                 w['norm_ffn'][lyr], w['ffn_w_gate'][lyr], w['ffn_w_up'][lyr], w['ffn_w_down'][lyr],
                 nb_s, D_FF // 2).reshape(nb_s, 1, d)
    return yp, ys, hg_p, hg_s


def kernel(x_prompt, x_sample, cache_moba_k, cache_moba_v, page_table, state_rwkv, state_rwkv_shift, state_hgrn, norm_mix, norm_ffn, w_in_even, w_out_even, rw_mu, rw_w0, rw_w2, rw_a0, rw_a2, rw_g2, rw_k_k, rw_k_a, rw_r_k, rw_ln_w, rw_ln_b, rw_v0, rw_v1, rw_v2, mb_q_norm, mb_k_norm, w_in_odd, w_out_odd, hg_lb_logits, hg_out_norm, ffn_w_gate, ffn_w_up, ffn_w_down):
    depth = norm_mix.shape[0]
    zeros_lora = jnp.zeros_like(rw_w2)
    w = dict(
        norm_mix=norm_mix, norm_ffn=norm_ffn,
        w_in_even=w_in_even.astype(BF16), w_out_even=w_out_even.astype(BF16),
        w_in_odd=w_in_odd.astype(BF16), w_out_odd=w_out_odd.astype(BF16),
        ffn_w_gate=ffn_w_gate.astype(BF16), ffn_w_up=ffn_w_up.astype(BF16),
        ffn_w_down=ffn_w_down.astype(BF16),
        rw_mu=rw_mu, rw_w0=rw_w0, rw_a0=rw_a0, rw_g2=rw_g2, rw_k_k=rw_k_k, rw_k_a=rw_k_a,
        rw_r_k=rw_r_k, rw_ln_w=rw_ln_w, rw_ln_b=rw_ln_b, rw_v0=rw_v0, rw_v1=rw_v1, rw_v2=rw_v2,
        w2p=jnp.concatenate([rw_w2, zeros_lora], axis=1), a2p=jnp.concatenate([zeros_lora, rw_a2], axis=1),
        mb_q_norm=mb_q_norm, mb_k_norm=mb_k_norm, hg_lb_logits=hg_lb_logits, hg_out_norm=hg_out_norm)
    page_table = page_table.astype(jnp.int32)
    cache_k = jnp.transpose(cache_moba_k, (0, 1, 3, 4, 2))
    cache_v = jnp.transpose(cache_moba_v, (0, 1, 3, 4, 2))

    xp, xs = x_prompt, x_sample
    vf_p = vf_s = None
    even_out, hg_out = [], []
    for layer in range(depth):
        il = layer // 2
        if layer % 2 == 0:
            xp, xs, vf_p, vf_s, outs = _even_layer(xp, xs, il, vf_p, vf_s, w, cache_k, cache_v,
                                                   page_table, state_rwkv, state_rwkv_shift)
            even_out.append(outs)
        else:
            xp, xs, hg_p, hg_s = _odd_layer(xp, xs, il, w, state_hgrn)
            hg_out.append((hg_p, hg_s))
    stack = lambda i: jnp.stack([o[i] for o in even_out])
    return (xp, xs, stack(0), stack(1), stack(2), stack(3), stack(4), stack(5), stack(6), stack(7),
            jnp.stack([o[0] for o in hg_out]), jnp.stack([o[1] for o in hg_out]))
```

```python
import functools

import numpy as np
import jax
import jax.numpy as jnp
from jax import lax
from jax.experimental import pallas as pl
from jax.experimental.pallas import tpu as pltpu

F32 = jnp.float32
BF16 = jnp.bfloat16
HI = lax.Precision.HIGHEST

D_MODEL = 1024
PAGE_SIZE = 128
RW_HEAD = 64
RW_WIDTH = 512
RW_HEADS = 8
RW_IN = 1792
MB_HEAD = 64
MB_WIDTH = 512
MB_HEADS = 8
MOBA_BLOCK = 256
MOBA_TOPK = 3
HG_HEADS = 8
HG_D = 128
D_FF = 2816
RMS_EPS = 1e-6
GN_EPS = 64e-5
MASK_NEG = -1e30
LOG2E = 1.4426950408889634

LANES = 128
VMEM_LIMIT = 56 * 1024 * 1024
RW_CHUNK = 64
HG_CHUNK = 64


def _cparams(*sem):
    return pltpu.CompilerParams(dimension_semantics=sem, vmem_limit_bytes=VMEM_LIMIT)


def _bf(x):
    return x.astype(BF16)


def _dot(a, b, precision=None):
    return jnp.dot(a, b, precision=precision, preferred_element_type=F32)


def _dot_nt(a, b, precision=None):
    return lax.dot_general(a, b, (((1,), (1,)), ((), ())), precision=precision,
                           preferred_element_type=F32)


def _dot_tn(a, b, precision=None):
    return lax.dot_general(a, b, (((0,), (0,)), ((), ())), precision=precision,
                           preferred_element_type=F32)


def _bmm(a, b):
    return lax.dot_general(_bf(a), _bf(b), (((2,), (1,)), ((0,), (0,))), preferred_element_type=F32)


def _bmm_nt(a, b):
    return lax.dot_general(_bf(a), _bf(b), (((2,), (2,)), ((0,), (0,))), preferred_element_type=F32)


def _bmm_tn(a, b):
    return lax.dot_general(_bf(a), _bf(b), (((1,), (1,)), ((0,), (0,))), preferred_element_type=F32)


def _split_dot(a_bf16, b):
    n = b.shape[1]
    hi = b.astype(BF16)
    lo = (b - hi.astype(F32)).astype(BF16)
    out = _dot(a_bf16, jnp.concatenate([hi, lo], axis=1))
    return out[:, :n] + out[:, n:]


def _hi_lo(x):
    hi = x.astype(BF16)
    return hi, (x - hi.astype(F32)).astype(BF16)


def _dot_x3(a, b):
    a_hi, a_lo = _hi_lo(a)
    b_hi, b_lo = _hi_lo(b)
    return _dot(a_hi, b_hi) + (_dot(a_hi, b_lo) + _dot(a_lo, b_hi))


def _dot_x2(a, g_bf16):
    a_hi, a_lo = _hi_lo(a)
    return _dot(a_hi, g_bf16) + _dot(a_lo, g_bf16)


def _sigmoid(x):
    return 1.0 / (1.0 + jnp.exp(-x))


def _group_matrices(width, group):
    g = np.zeros((width, LANES), np.float32)
    g[np.arange(width), np.arange(width) // group] = 1.0
    return jnp.asarray(g, dtype=BF16), jnp.asarray(g.T.copy(), dtype=BF16)


def _norm_matmul_kernel(x_ref, gain_ref, w_ref, *out_refs, widths, tn):
    x = x_ref[...]
    ms = jnp.mean(x * x, axis=-1, keepdims=True)
    h = (x * lax.rsqrt(ms + RMS_EPS) * gain_ref[...]).astype(BF16)
    col = 0
    for o_ref, width in zip(out_refs, widths):
        for c0 in range(0, width, tn):
            sz = min(tn, width - c0)
            o_ref[:, c0:c0 + sz] = _dot(h, w_ref[:, col + c0:col + c0 + sz])
        col += width


def norm_matmul(x, gain, w_bf16, widths, tm):
    m, d = x.shape
    n = w_bf16.shape[1]
    assert sum(widths) == n and m % tm == 0
    kern = functools.partial(_norm_matmul_kernel, widths=tuple(widths), tn=512)
    return pl.pallas_call(
        kern,
        grid=(m // tm,),
        in_specs=[pl.BlockSpec((tm, d), lambda i: (i, 0)),
                  pl.BlockSpec((1, d), lambda i: (0, 0)),
                  pl.BlockSpec((d, n), lambda i: (0, 0))],
        out_specs=[pl.BlockSpec((tm, wd), lambda i: (i, 0)) for wd in widths],
        out_shape=[jax.ShapeDtypeStruct((m, wd), F32) for wd in widths],
        compiler_params=_cparams("parallel"),
    )(x, gain.reshape(1, d), w_bf16)


def _mix_ffn_kernel(*refs, n_act):
    x_ref = refs[0]
    a_refs = refs[1:1 + n_act]
    wo_refs = refs[1 + n_act:1 + 2 * n_act]
    gain_ref, wg_ref, wu_ref, wd_ref, y_ref, x1_sc, h_sc, acc_sc = refs[1 + 2 * n_act:]
    j = pl.program_id(1)

    @pl.when(j == 0)
    def _():
        x1 = x_ref[...]
        for a_ref, wo_ref in zip(a_refs, wo_refs):
            x1 = x1 + _dot(a_ref[...].astype(BF16), wo_ref[...])
        x1_sc[...] = x1
        ms = jnp.mean(x1 * x1, axis=-1, keepdims=True)
        h_sc[...] = (x1 * lax.rsqrt(ms + RMS_EPS) * gain_ref[...]).astype(BF16)
        acc_sc[...] = jnp.zeros_like(acc_sc)

    h = h_sc[...]
    g = _dot(h, wg_ref[...])
    u = _dot(h, wu_ref[...])
    act = (g * _sigmoid(g) * u).astype(BF16)
    acc_sc[...] += _dot(act, wd_ref[...])

    @pl.when(j == pl.num_programs(1) - 1)
    def _():
        y_ref[...] = x1_sc[...] + acc_sc[...]


def mix_ffn(x, acts, wos_bf16, gain, wg, wu, wd, tm, tf):
    m, d = x.shape
    f = wg.shape[1]
    assert m % tm == 0 and f % tf == 0
    n_act = len(acts)
    in_specs = [pl.BlockSpec((tm, d), lambda i, j: (i, 0))]
    in_specs += [pl.BlockSpec((tm, a.shape[1]), lambda i, j: (i, 0)) for a in acts]
    in_specs += [pl.BlockSpec(w.shape, lambda i, j: (0, 0)) for w in wos_bf16]
    in_specs += [pl.BlockSpec((1, d), lambda i, j: (0, 0)),
                 pl.BlockSpec((d, tf), lambda i, j: (0, j)),
                 pl.BlockSpec((d, tf), lambda i, j: (0, j)),
                 pl.BlockSpec((tf, d), lambda i, j: (j, 0))]
    return pl.pallas_call(
        functools.partial(_mix_ffn_kernel, n_act=n_act),
        grid=(m // tm, f // tf),
        in_specs=in_specs,
        out_specs=pl.BlockSpec((tm, d), lambda i, j: (i, 0)),
        out_shape=jax.ShapeDtypeStruct((m, d), F32),
        scratch_shapes=[pltpu.VMEM((tm, d), F32), pltpu.VMEM((tm, d), BF16), pltpu.VMEM((tm, d), F32)],
        compiler_params=_cparams("parallel", "arbitrary"),
    )(x, *acts, *wos_bf16, gain.reshape(1, d), wg, wu, wd)


def _rwkv_prep_kernel(*refs, shift_mode, has_vres, tiles_per_seq):
    it = iter(refs)
    p_ref = next(it)
    prev_ref = next(it)
    mu_ref, w0_ref, a0_ref, kk_ref, ka_ref, w2_ref, a2_ref, g2_ref, grp_ref, grpt_ref = (
        next(it) for _ in range(10))
    if has_vres:
        vf_ref, v0_ref, v1_ref, v2_ref = (next(it) for _ in range(4))
    r_o, lw_o, k_o, v_o, kkn_o, b_o, g_o = (next(it) for _ in range(7))
    if not has_vres:
        vf_o = next(it)

    pf = p_ref[...]
    if shift_mode:
        first = (pl.program_id(0) % tiles_per_seq) == 0
        prev_row = jnp.where(first, 0.0, prev_ref[7:8, :])
        rows = lax.broadcasted_iota(jnp.int32, pf.shape, 0)
        prev = jnp.where(rows == 0, prev_row, pltpu.roll(pf, 1, axis=0))
    else:
        prev = prev_ref[...]
    xs = pf + (prev - pf) * mu_ref[...]
    r = xs[:, 0:RW_WIDTH]
    k = xs[:, RW_WIDTH:2 * RW_WIDTH]
    v = xs[:, 2 * RW_WIDTH:3 * RW_WIDTH]
    wa_lo = xs[:, 3 * RW_WIDTH:3 * RW_WIDTH + LANES]
    g_lo = xs[:, 3 * RW_WIDTH + LANES:RW_IN]
    z = w0_ref[...] + _dot_x3(jnp.tanh(wa_lo), w2_ref[...])
    sp = jnp.maximum(-z, 0.0) + jnp.log(1.0 + jnp.exp(-jnp.abs(z)))
    logw = -jnp.exp(-sp - 0.5)
    a = _sigmoid(a0_ref[...] + _dot_x3(wa_lo, a2_ref[...]))
    g = _dot_x3(_sigmoid(g_lo), g2_ref[...])
    if has_vres:
        gate = _sigmoid(v0_ref[...] + _dot_x3(_dot_x3(v, v1_ref[...]), v2_ref[...]))
        v = v + (vf_ref[...] - v) * gate
    else:
        vf_o[...] = v
    kk = k * kk_ref[...]
    ss = _dot_x2(kk * kk, grp_ref[...])
    inv = _dot_x2(lax.rsqrt(jnp.maximum(ss, 1e-24)), grpt_ref[...])
    kk = kk * inv
    k = k * (1.0 + (a - 1.0) * ka_ref[...])
    r_o[...] = r
    lw_o[...] = logw
    k_o[...] = k
    v_o[...] = v
    kkn_o[...] = kk
    b_o[...] = kk * a
    g_o[...] = g


def rwkv_prep(p_rw, prev, params, vres, shift_mode, seq_len, tm):
    t = p_rw.shape[0]
    assert t % tm == 0
    mu, w0, w2p, a0, a2p, g2, k_k, k_a = params
    grp, grpt = _group_matrices(RW_WIDTH, RW_HEAD)
    row = lambda a: a.reshape(1, -1)
    tile = lambda wd: pl.BlockSpec((tm, wd), lambda i: (i, 0))
    full = lambda a: pl.BlockSpec(a.shape, lambda i: (0,) * a.ndim)
    args = [p_rw]
    specs = [tile(RW_IN)]
    if shift_mode:
        r8 = tm // 8
        args.append(p_rw)
        specs.append(pl.BlockSpec((8, RW_IN), lambda i: (jnp.maximum(i * r8 - 1, 0), 0)))
    else:
        args.append(prev)
        specs.append(tile(RW_IN))
    small = [row(mu), row(w0), row(a0), row(k_k), row(k_a), w2p, a2p, g2, grp, grpt]
    args += small
    specs += [full(a) for a in small]
    has_vres = vres is not None
    if has_vres:
        vf, v0, v1, v2 = vres
        extra = [row(v0), v1, v2]
        args += [vf] + extra
        specs += [tile(RW_WIDTH)] + [full(a) for a in extra]
    n_out = 7 if has_vres else 8
    kern = functools.partial(_rwkv_prep_kernel, shift_mode=shift_mode, has_vres=has_vres,
                             tiles_per_seq=max(seq_len // tm, 1))
    outs = pl.pallas_call(
        kern,
        grid=(t // tm,),
        in_specs=specs,
        out_specs=[tile(RW_WIDTH)] * n_out,
        out_shape=[jax.ShapeDtypeStruct((t, RW_WIDTH), F32)] * n_out,
        compiler_params=_cparams("parallel"),
    )(*args)
    return outs


def _unit_lower_inverse(n, c):
    ri = lax.broadcasted_iota(jnp.int32, (c, c), 0)
    ci = lax.broadcasted_iota(jnp.int32, (c, c), 1)
    eye = (ri == ci).astype(F32)[None]
    blk = min(16, c)
    same = ((ri // blk) == (ci // blk))[None]
    nd = jnp.where(same, n, 0.0)
    x = eye - nd
    pw = nd
    span = 2
    while span < blk:
        pw = _bmm(pw, pw)
        x = _bmm(x, eye + pw)
        span *= 2
    if c > blk:
        p = _bmm(x, jnp.where(same, 0.0, n))
        y = eye - p
        pw = p
        span = 2
        while span < c // blk:
            pw = _bmm(pw, pw)
            y = _bmm(y, eye + pw)
            span *= 2
        x = _bmm(y, x)
    return x


def _rwkv_scan_kernel(r_ref, lw_ref, k_ref, v_ref, kk_ref, b_ref, g_ref, lnw_ref, lnb_ref, rk_ref,
                      s0_ref, tri_ref, o_ref, sout_ref, s_sc, *, chunk):
    l = pl.program_id(2)
    tl = r_ref.shape[1]
    c = chunk
    nc = tl // c
    hd = RW_HEAD

    @pl.when(l == 0)
    def _():
        s_sc[...] = s0_ref[0]

    ri = lax.broadcasted_iota(jnp.int32, (c, c), 0)
    ci = lax.broadcasted_iota(jnp.int32, (c, c), 1)
    tri_incl = (ri >= ci)[None]
    tri_strict = (ri > ci)[None]
    lw_all = lw_ref[0]
    lc_all = _split_dot(tri_ref[...], lw_all)

    for h in range(LANES // hd):
        sl = slice(h * hd, (h + 1) * hd)
        ch = lambda ref: ref[0][:, sl].reshape(nc, c, hd)
        r, k, v, kk, b = ch(r_ref), ch(k_ref), ch(v_ref), ch(kk_ref), ch(b_ref)
        lw = lw_all[:, sl].reshape(nc, c, hd)
        lc = lc_all[:, sl].reshape(nc, c, hd)
        tot = lc[:, c - 1:c, :]
        e_pos = jnp.exp(lc)
        e_neg = jnp.exp(-lc)
        e_end = jnp.exp(tot - lc)
        kk_t = kk * jnp.exp(lc - lw)
        r_t = r * e_pos
        b_t = b * e_neg
        k_t = k * e_neg
        a_ab = jnp.where(tri_strict, _bmm_nt(kk_t, b_t), 0.0)
        a_ak = jnp.where(tri_strict, _bmm_nt(kk_t, k_t), 0.0)
        a_rb = jnp.where(tri_incl, _bmm_nt(r_t, b_t), 0.0)
        a_rk = jnp.where(tri_incl, _bmm_nt(r_t, k_t), 0.0)
        tinv = _unit_lower_inverse(a_ab, c)
        wm = _bmm(tinv, kk_t)
        u0 = -_bmm(tinv, _bmm(a_ak, v))
        q_eff = r_t - _bmm(a_rb, wm)
        o0 = _bmm(a_rk, v) + _bmm(a_rb, u0)
        b_g = b * e_end
        k_g = k * e_end
        kr = lax.broadcasted_iota(jnp.int32, (hd, hd), 0)
        kc = lax.broadcasted_iota(jnp.int32, (hd, hd), 1)
        eye = (kr == kc).astype(F32)[None]
        m_mat = eye * jnp.exp(tot) - _bmm_tn(wm, b_g)
        d_mat = _bmm_tn(u0, b_g) + _bmm_tn(v, k_g)
        s = s_sc[h]
        states = []
        for i in range(nc):
            states.append(s)
            s = _dot(_bf(s), _bf(m_mat[i])) + d_mat[i]
        s_sc[h] = s
        o = o0 + _bmm_nt(q_eff, jnp.stack(states))
        o = o.reshape(tl, hd)
        mean = jnp.mean(o, axis=-1, keepdims=True)
        var = jnp.mean(jnp.square(o - mean), axis=-1, keepdims=True)
        o = (o - mean) * lax.rsqrt(var + GN_EPS) * lnw_ref[:, sl] + lnb_ref[:, sl]
        r2 = r_ref[0][:, sl]
        k2 = k_ref[0][:, sl]
        bonus = jnp.sum(r2 * k2 * rk_ref[:, sl], axis=-1, keepdims=True) * v_ref[0][:, sl]
        o_ref[0, :, sl] = (o + bonus) * g_ref[0][:, sl]

    @pl.when(l == pl.num_programs(2) - 1)
    def _():
        sout_ref[0] = s_sc[...]


def rwkv_scan(r, lw, k, v, kk, b, g, ln_w, ln_b, r_k, s0, tl, chunk):
    bsz, seq, _ = r.shape
    assert seq % tl == 0 and tl % chunk == 0
    hp = RW_WIDTH // LANES
    per = LANES // RW_HEAD
    seq_spec = pl.BlockSpec((1, tl, LANES), lambda bi, hi, li: (bi, li, hi))
    par_spec = pl.BlockSpec((1, LANES), lambda bi, hi, li: (0, hi))
    st_spec = pl.BlockSpec((1, per, RW_HEAD, RW_HEAD), lambda bi, hi, li: (bi, hi, 0, 0))
    row = lambda a: a.reshape(1, RW_WIDTH)
    idx = np.arange(tl)
    tri = jnp.asarray(((idx[:, None] >= idx[None, :]) & (idx[:, None] // chunk == idx[None, :] // chunk)),
                      dtype=BF16)
    return pl.pallas_call(
        functools.partial(_rwkv_scan_kernel, chunk=chunk),
        grid=(bsz, hp, seq // tl),
        in_specs=[seq_spec] * 7 + [par_spec] * 3 + [st_spec, pl.BlockSpec((tl, tl), lambda bi, hi, li: (0, 0))],
        out_specs=[seq_spec, st_spec],
        out_shape=[jax.ShapeDtypeStruct((bsz, seq, RW_WIDTH), F32),
                   jax.ShapeDtypeStruct((bsz, RW_HEADS, RW_HEAD, RW_HEAD), F32)],
        scratch_shapes=[pltpu.VMEM((per, RW_HEAD, RW_HEAD), F32)],
        compiler_params=_cparams("parallel", "parallel", "arbitrary"),
    )(r, lw, k, v, kk, b, g, row(ln_w), row(ln_b), row(r_k), s0, tri)


def _moba_prep_kernel(p_ref, qg_ref, kg_ref, grp_ref, grpt_ref, q_o, k_o, kb_o, vt_o, ks_o):
    p = p_ref[...]
    q = p[:, 0:MB_WIDTH]
    k = p[:, MB_WIDTH:2 * MB_WIDTH]
    v = p[:, 2 * MB_WIDTH:3 * MB_WIDTH]

    def head_norm(x, gain_ref):
        ms = _dot_x2(x * x, grp_ref[...]) * (1.0 / MB_HEAD)
        inv = _dot_x2(lax.rsqrt(ms + RMS_EPS), grpt_ref[...])
        return x * inv * gain_ref[...]

    qn = head_norm(q, qg_ref)
    kn = head_norm(k, kg_ref)
    q_o[...] = qn
    k_o[...] = kn
    kb_o[...] = kn.astype(BF16)
    vt_o[...] = v.T.astype(BF16)
    ks_o[0] = jnp.sum(kn, axis=0, keepdims=True)


def moba_prep(p_mb, q_gain, k_gain, tm):
    t = p_mb.shape[0]
    assert t % tm == 0
    grp, grpt = _group_matrices(MB_WIDTH, MB_HEAD)
    tile = lambda i: (i, 0)
    full = lambda a: pl.BlockSpec(a.shape, lambda i: (0,) * a.ndim)
    qg = jnp.tile(q_gain, MB_HEADS).reshape(1, MB_WIDTH)
    kg = jnp.tile(k_gain, MB_HEADS).reshape(1, MB_WIDTH)
    return pl.pallas_call(
        _moba_prep_kernel,
        grid=(t // tm,),
        in_specs=[pl.BlockSpec((tm, 3 * MB_WIDTH), tile), full(qg), full(kg), full(grp), full(grpt)],
        out_specs=[pl.BlockSpec((tm, MB_WIDTH), tile)] * 3
                  + [pl.BlockSpec((MB_WIDTH, tm), lambda i: (0, i)),
                     pl.BlockSpec((1, 1, MB_WIDTH), lambda i: (i, 0, 0))],
        out_shape=[jax.ShapeDtypeStruct((t, MB_WIDTH), F32), jax.ShapeDtypeStruct((t, MB_WIDTH), F32),
                   jax.ShapeDtypeStruct((t, MB_WIDTH), BF16), jax.ShapeDtypeStruct((MB_WIDTH, t), BF16),
                   jax.ShapeDtypeStruct((t // tm, 1, MB_WIDTH), F32)],
        compiler_params=_cparams("parallel"),
    )(p_mb, qg, kg, grp, grpt)


def _top_blocks(gate, idx, axis):
    sel = jnp.zeros_like(gate)
    rem = gate
    for _ in range(MOBA_TOPK):
        mx = jnp.max(rem, axis=axis, keepdims=True)
        live = jnp.logical_and(rem == mx, rem > 0.5 * MASK_NEG)
        first = jnp.min(jnp.where(live, idx, 1e9), axis=axis, keepdims=True)
        pick = idx == first
        sel = jnp.where(pick, 1.0, sel)
        rem = jnp.where(pick, MASK_NEG, rem)
    return sel


def _moba_attn_kernel(q_ref, k_ref, vt_ref, ks_ref, o_ref, s_a, s_b, p_a, p_b):
    qi = pl.program_id(2)
    tq = q_ref.shape[1]
    blk = MOBA_BLOCK
    assert tq == 2 * blk
    nb = ks_ref.shape[1]
    n_heads = LANES // MB_HEAD
    ncol = n_heads * tq
    q = q_ref[0]
    lane = lax.broadcasted_iota(jnp.int32, (1, LANES), 1)
    bmean = ks_ref[0] * (1.0 / MOBA_BLOCK)
    blk_f = lax.broadcasted_iota(jnp.int32, (nb, tq), 0).astype(F32)
    second = lax.broadcasted_iota(jnp.int32, (1, tq), 1) >= blk
    own_f = (2 * qi).astype(F32) + jnp.where(second, 1.0, 0.0)
    q_cols, bias_cols, own_bias = [], [], []
    pad_rows = jnp.full((LANES - nb, tq), MASK_NEG, F32)
    first_blk = lax.broadcasted_iota(jnp.int32, (nb, tq), 0) == 2 * qi
    for h in range(n_heads):
        q_h = jnp.where((lane // MB_HEAD) == h, q, 0.0)
        gate_t = _dot_nt(bmean, q_h, HI)
        gate_t = jnp.where(blk_f < own_f, gate_t, MASK_NEG)
        bias = (1.0 - _top_blocks(gate_t, blk_f, 0)) * MASK_NEG
        own_bias.append(jnp.max(jnp.where(first_blk, bias, MASK_NEG), axis=0, keepdims=True))
        bias_cols.append(jnp.concatenate([bias, pad_rows], axis=0).T)
        q_cols.append(q_h * (LOG2E * MB_HEAD ** -0.5))
    q_all = _bf(jnp.concatenate(q_cols, axis=0))
    q_ext = jnp.concatenate([q_all, _bf(jnp.concatenate(bias_cols, axis=0))], axis=1)

    def ones_rows(n):
        return _bf(jnp.where(lax.broadcasted_iota(jnp.int32, (16, n), 0) == 0, 1.0, 0.0))

    def values(r0, p):
        out = []
        for h in range(n_heads):
            vt_h = jnp.concatenate([vt_ref[h * MB_HEAD:(h + 1) * MB_HEAD, pl.ds(r0, tq)], ones_rows(tq)], axis=0)
            out.append(_dot(vt_h, p[:, h * tq:(h + 1) * tq]))
        return jnp.concatenate(out, axis=1)

    row0 = pl.multiple_of(qi * tq, tq)
    ri = lax.broadcasted_iota(jnp.int32, (tq, ncol), 0)
    cq = lax.broadcasted_iota(jnp.int32, (tq, ncol), 1) % tq
    gated = jnp.logical_and(ri < blk, cq >= blk)
    s = _dot_nt(k_ref[0, pl.ds(row0, tq), :], q_all) + jnp.where(gated, jnp.concatenate(own_bias, axis=1), 0.0)
    s = jnp.where(ri <= cq, s, MASK_NEG)
    m = jnp.max(s, axis=0, keepdims=True)
    acc = values(row0, _bf(jnp.exp2(s - m)))

    lane_i = lax.broadcasted_iota(jnp.int32, (1, LANES), 1)

    def pair_rows(i):
        return pl.multiple_of(jnp.minimum(i, nb // 2 - 1) * tq, tq)

    def scores(i, s_ref):
        k2 = k_ref[0, pl.ds(pair_rows(i), tq), :]
        onehot = [jnp.broadcast_to(_bf(jnp.where(lane_i == jnp.where(i < qi, 2 * i + u, LANES - 1), 1.0, 0.0)),
                                   (blk, LANES)) for u in range(2)]
        k_ext = jnp.concatenate([k2, jnp.concatenate(onehot, axis=0)], axis=1)
        s_ref[...] = _dot_nt(k_ext, q_ext)

    def softmax(s_ref, p_ref, m):
        s = s_ref[...]
        m_new = jnp.maximum(m, jnp.max(s, axis=0, keepdims=True))
        p_ref[...] = _bf(jnp.exp2(s - m_new))
        return m_new, jnp.exp2(m - m_new)

    scores(0, s_a)
    p_b[...] = jnp.zeros_like(p_b)

    def body(t, carry):
        m, acc = carry
        scores(2 * t + 1, s_b)
        pv = values(pair_rows(jnp.maximum(2 * t - 1, 0)), p_b[...])
        m, alpha = softmax(s_a, p_a, m)
        acc = (acc + pv) * alpha
        scores(2 * t + 2, s_a)
        pv = values(pair_rows(2 * t), p_a[...])
        m, alpha = softmax(s_b, p_b, m)
        acc = (acc + pv) * alpha
        return m, acc

    trips = (qi + 1) // 2
    m, acc = lax.fori_loop(0, trips, body, (m, acc))
    acc = acc + values(pair_rows(jnp.maximum(2 * trips - 1, 0)), p_b[...])
    o = acc[:MB_HEAD] / acc[MB_HEAD:MB_HEAD + 1]
    o_ref[0] = jnp.concatenate([o[:, h * tq:(h + 1) * tq] for h in range(n_heads)], axis=0).T


def moba_attn(qn, kb, vt, ksum):
    bsz, seq, _ = qn.shape
    tq = 2 * MOBA_BLOCK
    nb = seq // MOBA_BLOCK
    assert seq % tq == 0 and nb <= LANES
    hp = MB_WIDTH // LANES
    ncol = (LANES // MB_HEAD) * tq
    return pl.pallas_call(
        _moba_attn_kernel,
        grid=(bsz, hp, seq // tq),
        in_specs=[pl.BlockSpec((1, tq, LANES), lambda b, h, i: (b, i, h)),
                  pl.BlockSpec((1, seq, LANES), lambda b, h, i: (b, 0, h)),
                  pl.BlockSpec((LANES, seq), lambda b, h, i: (h, b)),
                  pl.BlockSpec((1, nb, LANES), lambda b, h, i: (b, 0, h))],
        out_specs=pl.BlockSpec((1, tq, LANES), lambda b, h, i: (b, i, h)),
        out_shape=jax.ShapeDtypeStruct((bsz, seq, MB_WIDTH), F32),
        scratch_shapes=[pltpu.VMEM((tq, ncol), F32), pltpu.VMEM((tq, ncol), F32),
                        pltpu.VMEM((tq, ncol), BF16), pltpu.VMEM((tq, ncol), BF16)],
        compiler_params=_cparams("parallel", "parallel", "arbitrary"),
    )(qn, kb, vt, ksum)


def _moba_scores_kernel(pt_ref, q_ref, *refs, n_pg):
    k_refs = refs[:n_pg]
    s_o = refs[n_pg]
    q = q_ref[0]
    for j in range(n_pg):
        s_o[0, j] = jnp.sum(k_refs[j][0, 0] * q, axis=1)


def _moba_select_kernel(s_ref, q_ref, kn_ref, p_o, w_o, id_o, *, ppb):
    scale = MB_HEAD ** -0.5
    s = s_ref[0]
    n_lp = s.shape[0]
    nblk = n_lp // ppb
    page_sum = jnp.sum(s, axis=-1, keepdims=True)
    gate = jnp.sum(page_sum.reshape(nblk, ppb, MB_HEADS, 1), axis=1) * (1.0 / MOBA_BLOCK)
    idx = lax.broadcasted_iota(jnp.int32, gate.shape, 0).astype(F32)
    lane = lax.broadcasted_iota(jnp.int32, (MB_HEADS, LANES), 1)
    sel = jnp.zeros_like(gate)
    rem = gate
    ids = jnp.zeros((MB_HEADS, LANES), F32)
    for t in range(MOBA_TOPK):
        mx = jnp.max(rem, axis=0, keepdims=True)
        first = jnp.min(jnp.where(rem == mx, idx, 1e9), axis=0, keepdims=True)
        pick = idx == first
        sel = jnp.where(pick, 1.0, sel)
        rem = jnp.where(pick, MASK_NEG, rem)
        for u in range(ppb):
            ids = jnp.where(lane == t * ppb + u, first[0] * ppb + u, ids)
    id_o[0] = ids.astype(jnp.int32)
    sel_pg = jnp.broadcast_to(sel[:, None], (nblk, ppb, MB_HEADS, 1)).reshape(n_lp, MB_HEADS, 1) > 0.5
    s_own = jnp.sum(kn_ref[0] * q_ref[0], axis=-1, keepdims=True) * scale
    sm = jnp.where(sel_pg, s * scale, MASK_NEG)
    m_all = jnp.maximum(jnp.max(jnp.max(sm, axis=0), axis=-1, keepdims=True), s_own)
    p = jnp.where(sel_pg, jnp.exp(sm - m_all[None]), 0.0)
    w_own = jnp.exp(s_own - m_all)
    inv = 1.0 / (jnp.sum(jnp.sum(p, axis=0), axis=-1, keepdims=True) + w_own)
    p_o[0] = p * inv[None]
    w_o[0] = jnp.broadcast_to(w_own * inv, (MB_HEADS, LANES))


def _moba_values_kernel(pt_ref, id_ref, p_ref, w_ref, vn_ref, *refs, n_sel, hps):
    v_refs = refs[:hps * n_sel]
    o_ref = refs[hps * n_sel]
    b = pl.program_id(0)
    for hh in range(hps):
        h = pl.program_id(1) * hps + hh
        acc = w_ref[0, pl.ds(h, 1), 0:1] * vn_ref[0, hh]
        for j in range(n_sel):
            page = id_ref[(b * MB_HEADS + h) * n_sel + j]
            prob = p_ref[0, page, pl.ds(h, 1), :]
            acc = acc + _dot_nt(prob, v_refs[hh * n_sel + j][0, 0, 0], HI)
        o_ref[0, hh] = acc


def moba_decode(qn, kn, vn, k_cache_t, v_cache_t, page_table, layer):
    bsz = qn.shape[0]
    n_lp = page_table.shape[1]
    ppb = MOBA_BLOCK // PAGE_SIZE
    assert n_lp % ppb == 0 and n_lp // ppb >= MOBA_TOPK
    n_pg = 16
    assert n_lp % n_pg == 0
    n_sel = MOBA_TOPK * ppb
    hps = 4
    heads = lambda a: a.reshape(bsz, MB_HEADS, MB_HEAD)
    page_blk = (1, 1, MB_HEADS, MB_HEAD, PAGE_SIZE)

    scores = pl.pallas_call(
        functools.partial(_moba_scores_kernel, n_pg=n_pg),
        grid_spec=pltpu.PrefetchScalarGridSpec(
            num_scalar_prefetch=1,
            grid=(bsz, n_lp // n_pg),
            in_specs=[pl.BlockSpec((1, MB_HEADS, MB_HEAD, 1), lambda b, g, pt: (b, 0, 0, 0))]
                     + [pl.BlockSpec(page_blk, functools.partial(
                         lambda b, g, pt, j: (layer, pt[b, g * n_pg + j], 0, 0, 0), j=j)) for j in range(n_pg)],
            out_specs=pl.BlockSpec((1, n_pg, MB_HEADS, PAGE_SIZE), lambda b, g, pt: (b, g, 0, 0))),
        out_shape=jax.ShapeDtypeStruct((bsz, n_lp, MB_HEADS, PAGE_SIZE), F32),
        compiler_params=_cparams("parallel", "arbitrary"),
    )(page_table, qn.reshape(bsz, MB_HEADS, MB_HEAD, 1), *([k_cache_t] * n_pg))

    vec = pl.BlockSpec((1, MB_HEADS, MB_HEAD), lambda b: (b, 0, 0))
    probs, w_own, ids = pl.pallas_call(
        functools.partial(_moba_select_kernel, ppb=ppb),
        grid=(bsz,),
        in_specs=[pl.BlockSpec((1, n_lp, MB_HEADS, PAGE_SIZE), lambda b: (b, 0, 0, 0)), vec, vec],
        out_specs=[pl.BlockSpec((1, n_lp, MB_HEADS, PAGE_SIZE), lambda b: (b, 0, 0, 0)),
                   pl.BlockSpec((1, MB_HEADS, LANES), lambda b: (b, 0, 0)),
                   pl.BlockSpec((1, MB_HEADS, LANES), lambda b: (b, 0, 0))],
        out_shape=[jax.ShapeDtypeStruct((bsz, n_lp, MB_HEADS, PAGE_SIZE), F32),
                   jax.ShapeDtypeStruct((bsz, MB_HEADS, LANES), F32),
                   jax.ShapeDtypeStruct((bsz, MB_HEADS, LANES), jnp.int32)],
        compiler_params=_cparams("parallel"),
    )(scores, heads(qn), heads(kn))

    sel_ids = ids[:, :, :n_sel].reshape(-1)
    out = pl.pallas_call(
        functools.partial(_moba_values_kernel, n_sel=n_sel, hps=hps),
        grid_spec=pltpu.PrefetchScalarGridSpec(
            num_scalar_prefetch=2,
            grid=(bsz, MB_HEADS // hps),
            in_specs=[pl.BlockSpec((1, n_lp, MB_HEADS, PAGE_SIZE), lambda b, g, pt, sid: (b, 0, 0, 0)),
                      pl.BlockSpec((1, MB_HEADS, LANES), lambda b, g, pt, sid: (b, 0, 0)),
                      pl.BlockSpec((1, hps, 1, MB_HEAD), lambda b, g, pt, sid: (b, g, 0, 0))]
                     + [pl.BlockSpec((1, 1, 1, MB_HEAD, PAGE_SIZE), functools.partial(
                         lambda b, g, pt, sid, hh, j: (
                             layer, pt[b, sid[(b * MB_HEADS + g * hps + hh) * n_sel + j]], g * hps + hh, 0, 0),
                         hh=hh, j=j)) for hh in range(hps) for j in range(n_sel)],
            out_specs=pl.BlockSpec((1, hps, 1, MB_HEAD), lambda b, g, pt, sid: (b, g, 0, 0))),
        out_shape=jax.ShapeDtypeStruct((bsz, MB_HEADS, 1, MB_HEAD), F32),
        compiler_params=_cparams("parallel", "arbitrary"),
    )(page_table, sel_ids, probs, w_own, vn.reshape(bsz, MB_HEADS, 1, MB_HEAD), *([v_cache_t] * (hps * n_sel)))
    return out.reshape(bsz, MB_WIDTH)


def _hgrn_constants(c):
    rows = np.arange(c)
    dmats, ups, los, masks = [], [], [], []
    z = c
    while z >= 2:
        seg = rows // z
        mid = seg * z + z // 2
        up = rows >= mid
        d = np.zeros((c, c), np.float32)
        for i in rows:
            if up[i]:
                d[i, mid[i]:i + 1] = 1.0
            else:
                d[i, i + 1:mid[i]] = 1.0
        dmats.append(d)
        ups.append(np.repeat(up[:, None], HG_D, 1).astype(np.float32))
        los.append(np.repeat(~up[:, None], HG_D, 1).astype(np.float32))
        masks.append(((seg[:, None] == seg[None, :]) & up[:, None] & (~up)[None, :]).astype(np.float32))
        z //= 2
    tri = np.tril(np.ones((c, c), np.float32))
    dmats += [tri, 1.0 - tri]
    return (jnp.asarray(np.concatenate(dmats, 0), dtype=BF16), jnp.asarray(np.stack(ups)),
            jnp.asarray(np.stack(los)), jnp.asarray(np.stack(masks)))


def _hgrn_kernel(pq_ref, pf_ref, pi_ref, pg_ref, lbl_ref, gain_ref, s0_ref, dd_ref, up_ref, lo_ref,
                 mk_ref, o_ref, sout_ref, st_sc, *, chunk, layer):
    l = pl.program_id(2)
    tl = pq_ref.shape[1]
    c = chunk
    nc = tl // c
    nlev = up_ref.shape[0]
    er = lax.broadcasted_iota(jnp.int32, (HG_D, HG_D), 0)
    ec = lax.broadcasted_iota(jnp.int32, (HG_D, HG_D), 1)
    eye_d = (er == ec).astype(F32)

    @pl.when(l == 0)
    def _():
        st_sc[...] = _dot_nt(eye_d, s0_ref[0, 0], HI)

    logits = lbl_ref[...]
    ex = jnp.exp(logits - jnp.max(logits, axis=0, keepdims=True))
    prob = ex / jnp.sum(ex, axis=0, keepdims=True)
    lb = jnp.sum(prob[0:layer + 1], axis=0, keepdims=True) - prob[0:1]

    qr = pq_ref[0]
    q = qr * _sigmoid(qr)
    forget = lb + (1.0 - lb) * _sigmoid(pf_ref[0])
    logf = jnp.log(jnp.maximum(forget, 1e-30))
    kg = 1.0 - forget
    v = pi_ref[0]

    ri = lax.broadcasted_iota(jnp.int32, (c, c), 0)
    ci = lax.broadcasted_iota(jnp.int32, (c, c), 1)
    diag = ri == ci
    rows = [slice(i * c, (i + 1) * c) for i in range(nc)]
    e_all = jnp.exp(_split_dot(dd_ref[...], jnp.concatenate([logf[rs] for rs in rows], axis=1)))
    intra, kv, qe, dec = [], [], [], []
    for i, rs in enumerate(rows):
        qc, kc, vc = q[rs], kg[rs], v[rs]
        e = e_all[:, i * HG_D:(i + 1) * HG_D]
        a = jnp.where(diag, jnp.sum(qc * kc, axis=1, keepdims=True), 0.0)
        for lev in range(nlev):
            el = e[lev * c:(lev + 1) * c]
            a = a + mk_ref[lev] * _dot_nt(_bf(qc * el * up_ref[lev]), _bf(kc * el * lo_ref[lev]))
        e_in = e[nlev * c:(nlev + 1) * c]
        e_out = e[(nlev + 1) * c:(nlev + 2) * c]
        intra.append(_dot(_bf(a), _bf(vc)))
        kv.append(_dot_tn(_bf(vc), _bf(kc * e_out)))
        qe.append(_bf(qc * e_in))
        dec.append(e_in[c - 1:c, :])
    st = st_sc[...]
    outs = []
    for i in range(nc):
        outs.append(_dot_nt(qe[i], _bf(st)) + intra[i])
        st = st * dec[i] + kv[i]
    st_sc[...] = st
    o = jnp.concatenate(outs, axis=0) if nc > 1 else outs[0]
    ms = jnp.mean(o * o, axis=-1, keepdims=True)
    gr = pg_ref[0]
    o_ref[0] = o * lax.rsqrt(ms + RMS_EPS) * gain_ref[...] * (gr * _sigmoid(gr))

    @pl.when(l == pl.num_programs(2) - 1)
    def _():
        sout_ref[0, 0] = _dot_nt(eye_d, st, HI)


def hgrn_scan(p, lb_logits, out_gain, s0, layer, tl, chunk):
    bsz, seq, _ = p.shape
    assert seq % tl == 0 and tl % chunk == 0
    dd, up, lo, mk = _hgrn_constants(chunk)
    col = lambda off: pl.BlockSpec((1, tl, HG_D), lambda b, h, li: (b, li, off + h))
    full = lambda a: pl.BlockSpec(a.shape, lambda b, h, li: (0,) * a.ndim)
    st_spec = pl.BlockSpec((1, 1, HG_D, HG_D), lambda b, h, li: (b, h, 0, 0))
    return pl.pallas_call(
        functools.partial(_hgrn_kernel, chunk=chunk, layer=layer),
        grid=(bsz, HG_HEADS, seq // tl),
        in_specs=[col(0), col(HG_HEADS), col(2 * HG_HEADS), col(3 * HG_HEADS),
                  pl.BlockSpec((lb_logits.shape[0], HG_D), lambda b, h, li: (0, h)),
                  pl.BlockSpec((1, HG_D), lambda b, h, li: (0, 0)),
                  st_spec, full(dd), full(up), full(lo), full(mk)],
        out_specs=[pl.BlockSpec((1, tl, HG_D), lambda b, h, li: (b, li, h)), st_spec],
        out_shape=[jax.ShapeDtypeStruct((bsz, seq, D_MODEL), F32),
                   jax.ShapeDtypeStruct((bsz, HG_HEADS, HG_D, HG_D), F32)],
        scratch_shapes=[pltpu.VMEM((HG_D, HG_D), F32)],
        compiler_params=_cparams("parallel", "parallel", "arbitrary"),
    )(p, p, p, p, lb_logits, out_gain.reshape(1, HG_D), s0, dd, up, lo, mk)


def _row_to_col(row, eye):
    return jnp.sum(jnp.where(eye, row, 0.0), axis=1, keepdims=True)


def _col_to_row(col, eye):
    return jnp.sum(jnp.where(eye, col, 0.0), axis=0, keepdims=True)


def _rwkv_step_kernel(r_ref, lw_ref, k_ref, v_ref, kk_ref, b_ref, g_ref, lnw_ref, lnb_ref, rk_ref,
                      s0_ref, o_ref, sout_ref):
    hd = RW_HEAD
    ri = lax.broadcasted_iota(jnp.int32, (hd, hd), 0)
    ci = lax.broadcasted_iota(jnp.int32, (hd, hd), 1)
    eye = ri == ci
    for h in range(RW_HEADS):
        sl = slice(h * hd, (h + 1) * hd)
        row = lambda ref: ref[0][:, sl]
        r, k, v, kk, b = row(r_ref), row(k_ref), row(v_ref), row(kk_ref), row(b_ref)
        s = s0_ref[0, h]
        s_kk = jnp.sum(s * kk, axis=1, keepdims=True)
        s = s * jnp.exp(row(lw_ref)) - s_kk * b + _row_to_col(v, eye) * k
        sout_ref[0, h] = s
        o = _col_to_row(jnp.sum(s * r, axis=1, keepdims=True), eye)
        mean = jnp.mean(o, axis=-1, keepdims=True)
        var = jnp.mean(jnp.square(o - mean), axis=-1, keepdims=True)
        o = (o - mean) * lax.rsqrt(var + GN_EPS) * lnw_ref[:, sl] + lnb_ref[:, sl]
        bonus = jnp.sum(r * k * rk_ref[:, sl], axis=-1, keepdims=True) * v
        o_ref[0, :, sl] = (o + bonus) * row(g_ref)


def rwkv_step(r, lw, k, v, kk, b, g, ln_w, ln_b, r_k, s0):
    bsz = r.shape[0]
    seq_spec = pl.BlockSpec((1, 1, RW_WIDTH), lambda i: (i, 0, 0))
    par_spec = pl.BlockSpec((1, RW_WIDTH), lambda i: (0, 0))
    st_spec = pl.BlockSpec((1, RW_HEADS, RW_HEAD, RW_HEAD), lambda i: (i, 0, 0, 0))
    row = lambda a: a.reshape(1, RW_WIDTH)
    tok = lambda a: a.reshape(bsz, 1, RW_WIDTH)
    o, s = pl.pallas_call(
        _rwkv_step_kernel,
        grid=(bsz,),
        in_specs=[seq_spec] * 7 + [par_spec] * 3 + [st_spec],
        out_specs=[seq_spec, st_spec],
        out_shape=[jax.ShapeDtypeStruct((bsz, 1, RW_WIDTH), F32),
                   jax.ShapeDtypeStruct((bsz, RW_HEADS, RW_HEAD, RW_HEAD), F32)],
        compiler_params=_cparams("parallel"),
    )(*[tok(a) for a in (r, lw, k, v, kk, b, g)], row(ln_w), row(ln_b), row(r_k), s0)
    return o.reshape(bsz, RW_WIDTH), s


def _hgrn_step_kernel(p_ref, lbl_ref, gain_ref, s0_ref, o_ref, sout_ref, *, layer):
    d = HG_D
    ri = lax.broadcasted_iota(jnp.int32, (d, d), 0)
    ci = lax.broadcasted_iota(jnp.int32, (d, d), 1)
    eye = ri == ci
    logits = lbl_ref[...]
    ex = jnp.exp(logits - jnp.max(logits, axis=0, keepdims=True))
    prob = ex / jnp.sum(ex, axis=0, keepdims=True)
    lb_all = jnp.sum(prob[0:layer + 1], axis=0, keepdims=True) - prob[0:1]
    width = HG_HEADS * d
    for h in range(HG_HEADS):
        col = lambda part: p_ref[0][:, part * width + h * d:part * width + (h + 1) * d]
        qr, fr, v, gr = col(0), col(1), col(2), col(3)
        lb = lb_all[:, h * d:(h + 1) * d]
        q = qr * _sigmoid(qr)
        forget = lb + (1.0 - lb) * _sigmoid(fr)
        decay = jnp.maximum(forget, 1e-30)
        s = _row_to_col(decay, eye) * s0_ref[0, h] + _row_to_col(1.0 - forget, eye) * v
        sout_ref[0, h] = s
        o = jnp.sum(_row_to_col(q, eye) * s, axis=0, keepdims=True)
        ms = jnp.mean(o * o, axis=-1, keepdims=True)
        o_ref[0, :, h * d:(h + 1) * d] = o * lax.rsqrt(ms + RMS_EPS) * gain_ref[...] * (gr * _sigmoid(gr))


def hgrn_step(p, lb_logits, out_gain, s0, layer):
    bsz = p.shape[0]
    st_spec = pl.BlockSpec((1, HG_HEADS, HG_D, HG_D), lambda i: (i, 0, 0, 0))
    o, s = pl.pallas_call(
        functools.partial(_hgrn_step_kernel, layer=layer),
        grid=(bsz,),
        in_specs=[pl.BlockSpec((1, 1, p.shape[1]), lambda i: (i, 0, 0)),
                  pl.BlockSpec(lb_logits.shape, lambda i: (0, 0)),
                  pl.BlockSpec((1, HG_D), lambda i: (0, 0)), st_spec],
        out_specs=[pl.BlockSpec((1, 1, D_MODEL), lambda i: (i, 0, 0)), st_spec],
        out_shape=[jax.ShapeDtypeStruct((bsz, 1, D_MODEL), F32),
                   jax.ShapeDtypeStruct((bsz, HG_HEADS, HG_D, HG_D), F32)],
        compiler_params=_cparams("parallel"),
    )(p.reshape(bsz, 1, -1), lb_logits, out_gain.reshape(1, HG_D), s0)
    return o.reshape(bsz, D_MODEL), s


def _even_layer(xp, xs, il, vf_p, vf_s, w, cache_k, cache_v, page_table, state_rwkv, state_shift):
    nb_p, seq, d = xp.shape
    nb_s = xs.shape[0]
    tp = nb_p * seq
    rw_params = (w['rw_mu'][il], w['rw_w0'][il], w['w2p'][il], w['rw_a0'][il], w['a2p'][il],
                 w['rw_g2'][il], w['rw_k_k'][il], w['rw_k_a'][il])
    has_vres = il > 0
    vres_w = (w['rw_v0'][il - 1], w['rw_v1'][il - 1], w['rw_v2'][il - 1]) if has_vres else None
    ln = (w['rw_ln_w'][il], w['rw_ln_b'][il], w['rw_r_k'][il].reshape(-1))

    p_rw, p_mb = norm_matmul(xp.reshape(tp, d), w['norm_mix'][2 * il], w['w_in_even'][il],
                             (RW_IN, 3 * MB_WIDTH), 512)
    vres = (vf_p,) + vres_w if has_vres else None
    prep = rwkv_prep(p_rw, None, rw_params, vres, True, seq, 512)
    if not has_vres:
        vf_p = prep[7]
    seq3 = lambda a: a.reshape(nb_p, seq, -1)
    o_rw, wkv_p = rwkv_scan(*[seq3(a) for a in prep[:7]], *ln,
                            jnp.zeros((nb_p, RW_HEADS, RW_HEAD, RW_HEAD), F32), 1024, RW_CHUNK)
    qn, kn, kb, vt, ksum = moba_prep(p_mb, w['mb_q_norm'][il], w['mb_k_norm'][il], MOBA_BLOCK)
    o_mb = moba_attn(seq3(qn), seq3(kb), vt, ksum.reshape(nb_p, seq // MOBA_BLOCK, MB_WIDTH))
    lyr = 2 * il
    yp = mix_ffn(xp.reshape(tp, d), [o_rw.reshape(tp, RW_WIDTH), o_mb.reshape(tp, MB_WIDTH)],
                 [w['w_out_even'][il][:RW_WIDTH], w['w_out_even'][il][RW_WIDTH:]],
                 w['norm_ffn'][lyr], w['ffn_w_gate'][lyr], w['ffn_w_up'][lyr], w['ffn_w_down'][lyr],
                 512, D_FF // 2).reshape(nb_p, seq, d)
    mk_p = kn.reshape(nb_p, seq, MB_HEADS, MB_HEAD)
    mv_p = p_mb[:, 2 * MB_WIDTH:].reshape(nb_p, seq, MB_HEADS, MB_HEAD)
    sh_p = p_rw.reshape(nb_p, seq, RW_IN)[:, -1]

    ps_rw, ps_mb = norm_matmul(xs.reshape(nb_s, d), w['norm_mix'][2 * il], w['w_in_even'][il],
                               (RW_IN, 3 * MB_WIDTH), nb_s)
    vres = (vf_s,) + vres_w if has_vres else None
    prep = rwkv_prep(ps_rw, state_shift[il], rw_params, vres, False, 1, nb_s)
    if not has_vres:
        vf_s = prep[7]
    os_rw, wkv_s = rwkv_step(*prep[:7], *ln, state_rwkv[il])
    qs, ks, _, _, _ = moba_prep(ps_mb, w['mb_q_norm'][il], w['mb_k_norm'][il], nb_s)
    vs = ps_mb[:, 2 * MB_WIDTH:]
    os_mb = moba_decode(qs, ks, vs, cache_k, cache_v, page_table, il)
    ys = mix_ffn(xs.reshape(nb_s, d), [os_rw, os_mb],
                 [w['w_out_even'][il][:RW_WIDTH], w['w_out_even'][il][RW_WIDTH:]],
                 w['norm_ffn'][lyr], w['ffn_w_gate'][lyr], w['ffn_w_up'][lyr], w['ffn_w_down'][lyr],
                 nb_s, D_FF // 2).reshape(nb_s, 1, d)
    mk_s = ks.reshape(nb_s, 1, MB_HEADS, MB_HEAD)
    mv_s = vs.reshape(nb_s, 1, MB_HEADS, MB_HEAD)
    return yp, ys, vf_p, vf_s, (mk_p, mv_p, mk_s, mv_s, wkv_p, wkv_s, sh_p, ps_rw)


def _odd_layer(xp, xs, il, w, state_hgrn):
    nb_p, seq, d = xp.shape
    nb_s = xs.shape[0]
    tp = nb_p * seq
    lyr = 2 * il + 1
    (pp,) = norm_matmul(xp.reshape(tp, d), w['norm_mix'][lyr], w['w_in_odd'][il], (4 * d,), 512)
    o_p, hg_p = hgrn_scan(pp.reshape(nb_p, seq, 4 * d), w['hg_lb_logits'], w['hg_out_norm'][il],
                          jnp.zeros((nb_p, HG_HEADS, HG_D, HG_D), F32), il, 1024, HG_CHUNK)
    yp = mix_ffn(xp.reshape(tp, d), [o_p.reshape(tp, d)], [w['w_out_odd'][il]],
                 w['norm_ffn'][lyr], w['ffn_w_gate'][lyr], w['ffn_w_up'][lyr], w['ffn_w_down'][lyr],
                 512, D_FF // 2).reshape(nb_p, seq, d)
    (ps,) = norm_matmul(xs.reshape(nb_s, d), w['norm_mix'][lyr], w['w_in_odd'][il], (4 * d,), nb_s)
    o_s, hg_s = hgrn_step(ps, w['hg_lb_logits'], w['hg_out_norm'][il], state_hgrn[il], il)
    ys = mix_ffn(xs.reshape(nb_s, d), [o_s], [w['w_out_odd'][il]],
                 w['norm_ffn'][lyr], w['ffn_w_gate'][lyr], w['ffn_w_up'][lyr], w['ffn_w_down'][lyr],
                 nb_s, D_FF // 2).reshape(nb_s, 1, d)
    return yp, ys, hg_p, hg_s


def kernel(x_prompt, x_sample, cache_moba_k, cache_moba_v, page_table, state_rwkv, state_rwkv_shift, state_hgrn, norm_mix, norm_ffn, w_in_even, w_out_even, rw_mu, rw_w0, rw_w2, rw_a0, rw_a2, rw_g2, rw_k_k, rw_k_a, rw_r_k, rw_ln_w, rw_ln_b, rw_v0, rw_v1, rw_v2, mb_q_norm, mb_k_norm, w_in_odd, w_out_odd, hg_lb_logits, hg_out_norm, ffn_w_gate, ffn_w_up, ffn_w_down):
    depth = norm_mix.shape[0]
    zeros_lora = jnp.zeros_like(rw_w2)
    w = dict(
        norm_mix=norm_mix, norm_ffn=norm_ffn,
        w_in_even=w_in_even.astype(BF16), w_out_even=w_out_even.astype(BF16),
        w_in_odd=w_in_odd.astype(BF16), w_out_odd=w_out_odd.astype(BF16),
        ffn_w_gate=ffn_w_gate.astype(BF16), ffn_w_up=ffn_w_up.astype(BF16),
        ffn_w_down=ffn_w_down.astype(BF16),
        rw_mu=rw_mu, rw_w0=rw_w0, rw_a0=rw_a0, rw_g2=rw_g2, rw_k_k=rw_k_k, rw_k_a=rw_k_a,
        rw_r_k=rw_r_k, rw_ln_w=rw_ln_w, rw_ln_b=rw_ln_b, rw_v0=rw_v0, rw_v1=rw_v1, rw_v2=rw_v2,
        w2p=jnp.concatenate([rw_w2, zeros_lora], axis=1), a2p=jnp.concatenate([zeros_lora, rw_a2], axis=1),
        mb_q_norm=mb_q_norm, mb_k_norm=mb_k_norm, hg_lb_logits=hg_lb_logits, hg_out_norm=hg_out_norm)
    page_table = page_table.astype(jnp.int32)
    cache_k = jnp.transpose(cache_moba_k, (0, 1, 3, 4, 2))
    cache_v = jnp.transpose(cache_moba_v, (0, 1, 3, 4, 2))

    xp, xs = x_prompt, x_sample
    vf_p = vf_s = None
    even_out, hg_out = [], []
    for layer in range(depth):
        il = layer // 2
        if layer % 2 == 0:
            xp, xs, vf_p, vf_s, outs = _even_layer(xp, xs, il, vf_p, vf_s, w, cache_k, cache_v,
                                                   page_table, state_rwkv, state_rwkv_shift)
            even_out.append(outs)
        else:
            xp, xs, hg_p, hg_s = _odd_layer(xp, xs, il, w, state_hgrn)
            hg_out.append((hg_p, hg_s))
    stack = lambda i: jnp.stack([o[i] for o in even_out])
    return (xp, xs, stack(0), stack(1), stack(2), stack(3), stack(4), stack(5), stack(6), stack(7),
            jnp.stack([o[0] for o in hg_out]), jnp.stack([o[1] for o in hg_out]))
```

```python
import functools

import numpy as np
import jax
import jax.numpy as jnp
from jax import lax
from jax.experimental import pallas as pl
from jax.experimental.pallas import tpu as pltpu

F32 = jnp.float32
BF16 = jnp.bfloat16
HI = lax.Precision.HIGHEST

D_MODEL = 1024
PAGE_SIZE = 128
RW_HEAD = 64
RW_WIDTH = 512
RW_HEADS = 8
RW_IN = 1792
MB_HEAD = 64
MB_WIDTH = 512
MB_HEADS = 8
MOBA_BLOCK = 256
MOBA_TOPK = 3
HG_HEADS = 8
HG_D = 128
D_FF = 2816
RMS_EPS = 1e-6
GN_EPS = 64e-5
MASK_NEG = -1e30
LOG2E = 1.4426950408889634

LANES = 128
VMEM_LIMIT = 56 * 1024 * 1024
RW_CHUNK = 64
HG_CHUNK = 64


def _cparams(*sem):
    return pltpu.CompilerParams(dimension_semantics=sem, vmem_limit_bytes=VMEM_LIMIT)


def _bf(x):
    return x.astype(BF16)


def _dot(a, b, precision=None):
    return jnp.dot(a, b, precision=precision, preferred_element_type=F32)


def _dot_nt(a, b, precision=None):
    return lax.dot_general(a, b, (((1,), (1,)), ((), ())), precision=precision,
                           preferred_element_type=F32)


def _dot_tn(a, b, precision=None):
    return lax.dot_general(a, b, (((0,), (0,)), ((), ())), precision=precision,
                           preferred_element_type=F32)


def _bmm(a, b):
    return lax.dot_general(_bf(a), _bf(b), (((2,), (1,)), ((0,), (0,))), preferred_element_type=F32)


def _bmm_nt(a, b):
    return lax.dot_general(_bf(a), _bf(b), (((2,), (2,)), ((0,), (0,))), preferred_element_type=F32)


def _bmm_tn(a, b):
    return lax.dot_general(_bf(a), _bf(b), (((1,), (1,)), ((0,), (0,))), preferred_element_type=F32)


def _split_dot(a_bf16, b):
    n = b.shape[1]
    hi = b.astype(BF16)
    lo = (b - hi.astype(F32)).astype(BF16)
    out = _dot(a_bf16, jnp.concatenate([hi, lo], axis=1))
    return out[:, :n] + out[:, n:]


def _hi_lo(x):
    hi = x.astype(BF16)
    return hi, (x - hi.astype(F32)).astype(BF16)


def _dot_x3(a, b):
    a_hi, a_lo = _hi_lo(a)
    b_hi, b_lo = _hi_lo(b)
    return _dot(a_hi, b_hi) + (_dot(a_hi, b_lo) + _dot(a_lo, b_hi))


def _dot_x2(a, g_bf16):
    a_hi, a_lo = _hi_lo(a)
    return _dot(a_hi, g_bf16) + _dot(a_lo, g_bf16)


def _sigmoid(x):
    return 1.0 / (1.0 + jnp.exp(-x))


def _group_matrices(width, group):
    g = np.zeros((width, LANES), np.float32)
    g[np.arange(width), np.arange(width) // group] = 1.0
    return jnp.asarray(g, dtype=BF16), jnp.asarray(g.T.copy(), dtype=BF16)


def _norm_matmul_kernel(x_ref, gain_ref, w_ref, *out_refs, widths, tn):
    x = x_ref[...]
    ms = jnp.mean(x * x, axis=-1, keepdims=True)
    h = (x * lax.rsqrt(ms + RMS_EPS) * gain_ref[...]).astype(BF16)
    col = 0
    for o_ref, width in zip(out_refs, widths):
        for c0 in range(0, width, tn):
            sz = min(tn, width - c0)
            o_ref[:, c0:c0 + sz] = _dot(h, w_ref[:, col + c0:col + c0 + sz])
        col += width


def norm_matmul(x, gain, w_bf16, widths, tm):
    m, d = x.shape
    n = w_bf16.shape[1]
    assert sum(widths) == n and m % tm == 0
    kern = functools.partial(_norm_matmul_kernel, widths=tuple(widths), tn=512)
    return pl.pallas_call(
        kern,
        grid=(m // tm,),
        in_specs=[pl.BlockSpec((tm, d), lambda i: (i, 0)),
                  pl.BlockSpec((1, d), lambda i: (0, 0)),
                  pl.BlockSpec((d, n), lambda i: (0, 0))],
        out_specs=[pl.BlockSpec((tm, wd), lambda i: (i, 0)) for wd in widths],
        out_shape=[jax.ShapeDtypeStruct((m, wd), F32) for wd in widths],
        compiler_params=_cparams("parallel"),
    )(x, gain.reshape(1, d), w_bf16)


def _mix_ffn_kernel(*refs, n_act):
    x_ref = refs[0]
    a_refs = refs[1:1 + n_act]
    wo_refs = refs[1 + n_act:1 + 2 * n_act]
    gain_ref, wg_ref, wu_ref, wd_ref, y_ref, x1_sc, h_sc, acc_sc = refs[1 + 2 * n_act:]
    j = pl.program_id(1)

    @pl.when(j == 0)
    def _():
        x1 = x_ref[...]
        for a_ref, wo_ref in zip(a_refs, wo_refs):
            x1 = x1 + _dot(a_ref[...].astype(BF16), wo_ref[...])
        x1_sc[...] = x1
        ms = jnp.mean(x1 * x1, axis=-1, keepdims=True)
        h_sc[...] = (x1 * lax.rsqrt(ms + RMS_EPS) * gain_ref[...]).astype(BF16)
        acc_sc[...] = jnp.zeros_like(acc_sc)

    h = h_sc[...]
    g = _dot(h, wg_ref[...])
    u = _dot(h, wu_ref[...])
    act = (g * _sigmoid(g) * u).astype(BF16)
    acc_sc[...] += _dot(act, wd_ref[...])

    @pl.when(j == pl.num_programs(1) - 1)
    def _():
        y_ref[...] = x1_sc[...] + acc_sc[...]


def mix_ffn(x, acts, wos_bf16, gain, wg, wu, wd, tm, tf):
    m, d = x.shape
    f = wg.shape[1]
    assert m % tm == 0 and f % tf == 0
    n_act = len(acts)
    in_specs = [pl.BlockSpec((tm, d), lambda i, j: (i, 0))]
    in_specs += [pl.BlockSpec((tm, a.shape[1]), lambda i, j: (i, 0)) for a in acts]
    in_specs += [pl.BlockSpec(w.shape, lambda i, j: (0, 0)) for w in wos_bf16]
    in_specs += [pl.BlockSpec((1, d), lambda i, j: (0, 0)),
                 pl.BlockSpec((d, tf), lambda i, j: (0, j)),
                 pl.BlockSpec((d, tf), lambda i, j: (0, j)),
                 pl.BlockSpec((tf, d), lambda i, j: (j, 0))]
    return pl.pallas_call(
        functools.partial(_mix_ffn_kernel, n_act=n_act),
        grid=(m // tm, f // tf),
        in_specs=in_specs,
        out_specs=pl.BlockSpec((tm, d), lambda i, j: (i, 0)),
        out_shape=jax.ShapeDtypeStruct((m, d), F32),
        scratch_shapes=[pltpu.VMEM((tm, d), F32), pltpu.VMEM((tm, d), BF16), pltpu.VMEM((tm, d), F32)],
        compiler_params=_cparams("parallel", "arbitrary"),
    )(x, *acts, *wos_bf16, gain.reshape(1, d), wg, wu, wd)


def _rwkv_prep_kernel(*refs, shift_mode, has_vres, tiles_per_seq):
    it = iter(refs)
    p_ref = next(it)
    prev_ref = next(it)
    mu_ref, w0_ref, a0_ref, kk_ref, ka_ref, w2_ref, a2_ref, g2_ref, grp_ref, grpt_ref = (
        next(it) for _ in range(10))
    if has_vres:
        vf_ref, v0_ref, v1_ref, v2_ref = (next(it) for _ in range(4))
    r_o, lw_o, k_o, v_o, kkn_o, b_o, g_o = (next(it) for _ in range(7))
    if not has_vres:
        vf_o = next(it)

    pf = p_ref[...]
    if shift_mode:
        first = (pl.program_id(0) % tiles_per_seq) == 0
        prev_row = jnp.where(first, 0.0, prev_ref[7:8, :])
        rows = lax.broadcasted_iota(jnp.int32, pf.shape, 0)
        prev = jnp.where(rows == 0, prev_row, pltpu.roll(pf, 1, axis=0))
    else:
        prev = prev_ref[...]
    xs = pf + (prev - pf) * mu_ref[...]
    r = xs[:, 0:RW_WIDTH]
    k = xs[:, RW_WIDTH:2 * RW_WIDTH]
    v = xs[:, 2 * RW_WIDTH:3 * RW_WIDTH]
    wa_lo = xs[:, 3 * RW_WIDTH:3 * RW_WIDTH + LANES]
    g_lo = xs[:, 3 * RW_WIDTH + LANES:RW_IN]
    z = w0_ref[...] + _dot_x3(jnp.tanh(wa_lo), w2_ref[...])
    sp = jnp.maximum(-z, 0.0) + jnp.log(1.0 + jnp.exp(-jnp.abs(z)))
    logw = -jnp.exp(-sp - 0.5)
    a = _sigmoid(a0_ref[...] + _dot_x3(wa_lo, a2_ref[...]))
    g = _dot_x3(_sigmoid(g_lo), g2_ref[...])
    if has_vres:
        gate = _sigmoid(v0_ref[...] + _dot_x3(_dot_x3(v, v1_ref[...]), v2_ref[...]))
        v = v + (vf_ref[...] - v) * gate
    else:
        vf_o[...] = v
    kk = k * kk_ref[...]
    ss = _dot_x2(kk * kk, grp_ref[...])
    inv = _dot_x2(lax.rsqrt(jnp.maximum(ss, 1e-24)), grpt_ref[...])
    kk = kk * inv
    k = k * (1.0 + (a - 1.0) * ka_ref[...])
    r_o[...] = r
    lw_o[...] = logw
    k_o[...] = k
    v_o[...] = v
    kkn_o[...] = kk
    b_o[...] = kk * a
    g_o[...] = g


def rwkv_prep(p_rw, prev, params, vres, shift_mode, seq_len, tm):
    t = p_rw.shape[0]
    assert t % tm == 0
    mu, w0, w2p, a0, a2p, g2, k_k, k_a = params
    grp, grpt = _group_matrices(RW_WIDTH, RW_HEAD)
    row = lambda a: a.reshape(1, -1)
    tile = lambda wd: pl.BlockSpec((tm, wd), lambda i: (i, 0))
    full = lambda a: pl.BlockSpec(a.shape, lambda i: (0,) * a.ndim)
    args = [p_rw]
    specs = [tile(RW_IN)]
    if shift_mode:
        r8 = tm // 8
        args.append(p_rw)
        specs.append(pl.BlockSpec((8, RW_IN), lambda i: (jnp.maximum(i * r8 - 1, 0), 0)))
    else:
        args.append(prev)
        specs.append(tile(RW_IN))
    small = [row(mu), row(w0), row(a0), row(k_k), row(k_a), w2p, a2p, g2, grp, grpt]
    args += small
    specs += [full(a) for a in small]
    has_vres = vres is not None
    if has_vres:
        vf, v0, v1, v2 = vres
        extra = [row(v0), v1, v2]
        args += [vf] + extra
        specs += [tile(RW_WIDTH)] + [full(a) for a in extra]
    n_out = 7 if has_vres else 8
    kern = functools.partial(_rwkv_prep_kernel, shift_mode=shift_mode, has_vres=has_vres,
                             tiles_per_seq=max(seq_len // tm, 1))
    outs = pl.pallas_call(
        kern,
        grid=(t // tm,),
        in_specs=specs,
        out_specs=[tile(RW_WIDTH)] * n_out,
        out_shape=[jax.ShapeDtypeStruct((t, RW_WIDTH), F32)] * n_out,
        compiler_params=_cparams("parallel"),
    )(*args)
    return outs


def _unit_lower_inverse(n, c):
    ri = lax.broadcasted_iota(jnp.int32, (c, c), 0)
    ci = lax.broadcasted_iota(jnp.int32, (c, c), 1)
    eye = (ri == ci).astype(F32)[None]
    blk = min(16, c)
    same = ((ri // blk) == (ci // blk))[None]
    nd = jnp.where(same, n, 0.0)
    x = eye - nd
    pw = nd
    span = 2
    while span < blk:
        pw = _bmm(pw, pw)
        x = _bmm(x, eye + pw)
        span *= 2
    if c > blk:
        p = _bmm(x, jnp.where(same, 0.0, n))
        y = eye - p
        pw = p
        span = 2
        while span < c // blk:
            pw = _bmm(pw, pw)
            y = _bmm(y, eye + pw)
            span *= 2
        x = _bmm(y, x)
    return x


def _rwkv_scan_kernel(r_ref, lw_ref, k_ref, v_ref, kk_ref, b_ref, g_ref, lnw_ref, lnb_ref, rk_ref,
                      s0_ref, tri_ref, o_ref, sout_ref, s_sc, *, chunk):
    l = pl.program_id(2)
    tl = r_ref.shape[1]
    c = chunk
    nc = tl // c
    hd = RW_HEAD
    n_heads = LANES // hd

    @pl.when(l == 0)
    def _():
        s_sc[...] = jnp.zeros_like(s_sc)
        for h in range(n_heads):
            s_sc[h * hd:(h + 1) * hd, h * hd:(h + 1) * hd] = s0_ref[0, h]

    ri = lax.broadcasted_iota(jnp.int32, (c, c), 0)
    ci = lax.broadcasted_iota(jnp.int32, (c, c), 1)
    tri_incl = (ri >= ci)[None]
    tri_strict = (ri > ci)[None]
    kr = lax.broadcasted_iota(jnp.int32, (LANES, LANES), 0)
    kc = lax.broadcasted_iota(jnp.int32, (LANES, LANES), 1)
    same_head = (kr // hd) == (kc // hd)
    head_of_lane = lax.broadcasted_iota(jnp.int32, (1, 1, LANES), 2) // hd
    to3 = lambda x: x.reshape(nc, c, LANES)

    lw = to3(lw_ref[0])
    lc = to3(_split_dot(tri_ref[...], lw_ref[0]))
    tot = lc[:, c - 1:c, :]
    r, k, v, kk, b = to3(r_ref[0]), to3(k_ref[0]), to3(v_ref[0]), to3(kk_ref[0]), to3(b_ref[0])
    e_neg = jnp.exp(-lc)
    e_end = jnp.exp(tot - lc)
    kk_t = kk * jnp.exp(lc - lw)
    r_t = r * jnp.exp(lc)
    b_t = b * e_neg
    k_t = k * e_neg
    wm = u0 = q_eff = o0 = None
    for h in range(n_heads):
        mine = head_of_lane == h
        lhs = jnp.concatenate([jnp.where(mine, kk_t, 0.0), jnp.where(mine, r_t, 0.0)], axis=1)
        g_b = _bmm_nt(lhs, b_t)
        g_k = _bmm_nt(lhs, k_t)
        a_ab = jnp.where(tri_strict, g_b[:, :c], 0.0)
        a_rb = jnp.where(tri_incl, g_b[:, c:], 0.0)
        a_ak = jnp.where(tri_strict, g_k[:, :c], 0.0)
        a_rk = jnp.where(tri_incl, g_k[:, c:], 0.0)
        tinv = _unit_lower_inverse(a_ab, c)
        wu = _bmm(tinv, jnp.concatenate([kk_t, _bmm(a_ak, v)], axis=2))
        wm_h, u0_h = wu[:, :, :LANES], -wu[:, :, LANES:]
        ru = _bmm(a_rb, jnp.concatenate([wm_h, u0_h], axis=2))
        qe_h = r_t - ru[:, :, :LANES]
        o0_h = _bmm(a_rk, v) + ru[:, :, LANES:]
        if h == 0:
            wm, u0, q_eff, o0 = wm_h, u0_h, qe_h, o0_h
        else:
            wm, u0 = jnp.where(mine, wm_h, wm), jnp.where(mine, u0_h, u0)
            q_eff, o0 = jnp.where(mine, qe_h, q_eff), jnp.where(mine, o0_h, o0)
    eye = (kr == kc).astype(F32)[None]
    m_mat = jnp.where(same_head[None], eye * jnp.exp(tot) - _bmm_tn(wm, b * e_end), 0.0)
    d_mat = jnp.where(same_head[None],
                      _bmm_tn(jnp.concatenate([u0, v], axis=1),
                              jnp.concatenate([b * e_end, k * e_end], axis=1)), 0.0)
    s = s_sc[...]
    states = []
    for i in range(nc):
        states.append(s)
        s = _dot(_bf(s), _bf(m_mat[i])) + d_mat[i]
    s_sc[...] = s
    o = (o0 + _bmm_nt(q_eff, jnp.stack(states))).reshape(tl, LANES)
    head_sum = _bf(jnp.where(same_head, 1.0, 0.0))
    mean = _dot_x2(o, head_sum) * (1.0 / hd)
    var = _dot_x2(jnp.square(o - mean), head_sum) * (1.0 / hd)
    o = (o - mean) * lax.rsqrt(var + GN_EPS) * lnw_ref[...] + lnb_ref[...]
    bonus = _dot_x2(r_ref[0] * k_ref[0] * rk_ref[...], head_sum) * v_ref[0]
    o_ref[0] = (o + bonus) * g_ref[0]

    @pl.when(l == pl.num_programs(2) - 1)
    def _():
        for h in range(n_heads):
            sout_ref[0, h] = s[h * hd:(h + 1) * hd, h * hd:(h + 1) * hd]


def rwkv_scan(r, lw, k, v, kk, b, g, ln_w, ln_b, r_k, s0, tl, chunk):
    bsz, seq, _ = r.shape
    assert seq % tl == 0 and tl % chunk == 0
    hp = RW_WIDTH // LANES
    per = LANES // RW_HEAD
    seq_spec = pl.BlockSpec((1, tl, LANES), lambda bi, hi, li: (bi, li, hi))
    par_spec = pl.BlockSpec((1, LANES), lambda bi, hi, li: (0, hi))
    st_spec = pl.BlockSpec((1, per, RW_HEAD, RW_HEAD), lambda bi, hi, li: (bi, hi, 0, 0))
    row = lambda a: a.reshape(1, RW_WIDTH)
    idx = np.arange(tl)
    tri = jnp.asarray(((idx[:, None] >= idx[None, :]) & (idx[:, None] // chunk == idx[None, :] // chunk)),
                      dtype=BF16)
    return pl.pallas_call(
        functools.partial(_rwkv_scan_kernel, chunk=chunk),
        grid=(bsz, hp, seq // tl),
        in_specs=[seq_spec] * 7 + [par_spec] * 3 + [st_spec, pl.BlockSpec((tl, tl), lambda bi, hi, li: (0, 0))],
        out_specs=[seq_spec, st_spec],
        out_shape=[jax.ShapeDtypeStruct((bsz, seq, RW_WIDTH), F32),
                   jax.ShapeDtypeStruct((bsz, RW_HEADS, RW_HEAD, RW_HEAD), F32)],
        scratch_shapes=[pltpu.VMEM((LANES, LANES), F32)],
        compiler_params=_cparams("parallel", "parallel", "arbitrary"),
    )(r, lw, k, v, kk, b, g, row(ln_w), row(ln_b), row(r_k), s0, tri)


def _moba_prep_kernel(p_ref, qg_ref, kg_ref, grp_ref, grpt_ref, q_o, k_o, kb_o, vt_o, ks_o):
    p = p_ref[...]
    q = p[:, 0:MB_WIDTH]
    k = p[:, MB_WIDTH:2 * MB_WIDTH]
    v = p[:, 2 * MB_WIDTH:3 * MB_WIDTH]

    def head_norm(x, gain_ref):
        ms = _dot_x2(x * x, grp_ref[...]) * (1.0 / MB_HEAD)
        inv = _dot_x2(lax.rsqrt(ms + RMS_EPS), grpt_ref[...])
        return x * inv * gain_ref[...]

    qn = head_norm(q, qg_ref)
    kn = head_norm(k, kg_ref)
    q_o[...] = qn
    k_o[...] = kn
    kb_o[...] = kn.astype(BF16)
    vt_o[...] = v.T.astype(BF16)
    ks_o[0] = jnp.sum(kn, axis=0, keepdims=True)


def moba_prep(p_mb, q_gain, k_gain, tm):
    t = p_mb.shape[0]
    assert t % tm == 0
    grp, grpt = _group_matrices(MB_WIDTH, MB_HEAD)
    tile = lambda i: (i, 0)
    full = lambda a: pl.BlockSpec(a.shape, lambda i: (0,) * a.ndim)
    qg = jnp.tile(q_gain, MB_HEADS).reshape(1, MB_WIDTH)
    kg = jnp.tile(k_gain, MB_HEADS).reshape(1, MB_WIDTH)
    return pl.pallas_call(
        _moba_prep_kernel,
        grid=(t // tm,),
        in_specs=[pl.BlockSpec((tm, 3 * MB_WIDTH), tile), full(qg), full(kg), full(grp), full(grpt)],
        out_specs=[pl.BlockSpec((tm, MB_WIDTH), tile)] * 3
                  + [pl.BlockSpec((MB_WIDTH, tm), lambda i: (0, i)),
                     pl.BlockSpec((1, 1, MB_WIDTH), lambda i: (i, 0, 0))],
        out_shape=[jax.ShapeDtypeStruct((t, MB_WIDTH), F32), jax.ShapeDtypeStruct((t, MB_WIDTH), F32),
                   jax.ShapeDtypeStruct((t, MB_WIDTH), BF16), jax.ShapeDtypeStruct((MB_WIDTH, t), BF16),
                   jax.ShapeDtypeStruct((t // tm, 1, MB_WIDTH), F32)],
        compiler_params=_cparams("parallel"),
    )(p_mb, qg, kg, grp, grpt)


def _top_blocks(gate, idx, axis):
    sel = jnp.zeros_like(gate)
    rem = gate
    for _ in range(MOBA_TOPK):
        mx = jnp.max(rem, axis=axis, keepdims=True)
        live = jnp.logical_and(rem == mx, rem > 0.5 * MASK_NEG)
        first = jnp.min(jnp.where(live, idx, 1e9), axis=axis, keepdims=True)
        pick = idx == first
        sel = jnp.where(pick, 1.0, sel)
        rem = jnp.where(pick, MASK_NEG, rem)
    return sel


def _moba_attn_kernel(q_ref, k_ref, vt_ref, ks_ref, o_ref, s_a, s_b, p_a, p_b):
    qi = pl.program_id(2)
    tq = q_ref.shape[1]
    blk = MOBA_BLOCK
    assert tq == 2 * blk
    nb = ks_ref.shape[1]
    n_heads = LANES // MB_HEAD
    ncol = n_heads * tq
    q = q_ref[0]
    lane = lax.broadcasted_iota(jnp.int32, (1, LANES), 1)
    bmean = ks_ref[0] * (1.0 / MOBA_BLOCK)
    blk_f = lax.broadcasted_iota(jnp.int32, (nb, tq), 0).astype(F32)
    second = lax.broadcasted_iota(jnp.int32, (1, tq), 1) >= blk
    own_f = (2 * qi).astype(F32) + jnp.where(second, 1.0, 0.0)
    q_cols, bias_cols, own_bias = [], [], []
    pad_rows = jnp.full((LANES - nb, tq), MASK_NEG, F32)
    first_blk = lax.broadcasted_iota(jnp.int32, (nb, tq), 0) == 2 * qi
    for h in range(n_heads):
        q_h = jnp.where((lane // MB_HEAD) == h, q, 0.0)
        gate_t = _dot_nt(bmean, q_h, HI)
        gate_t = jnp.where(blk_f < own_f, gate_t, MASK_NEG)
        bias = (1.0 - _top_blocks(gate_t, blk_f, 0)) * MASK_NEG
        own_bias.append(jnp.max(jnp.where(first_blk, bias, MASK_NEG), axis=0, keepdims=True))
        bias_cols.append(jnp.concatenate([bias, pad_rows], axis=0).T)
        q_cols.append(q_h * (LOG2E * MB_HEAD ** -0.5))
    q_all = _bf(jnp.concatenate(q_cols, axis=0))
    q_ext = jnp.concatenate([q_all, _bf(jnp.concatenate(bias_cols, axis=0))], axis=1)

    def ones_rows(n):
        return _bf(jnp.where(lax.broadcasted_iota(jnp.int32, (16, n), 0) == 0, 1.0, 0.0))

    def values(r0, p):
        out = []
        for h in range(n_heads):
            vt_h = jnp.concatenate([vt_ref[h * MB_HEAD:(h + 1) * MB_HEAD, pl.ds(r0, tq)], ones_rows(tq)], axis=0)
            out.append(_dot(vt_h, p[:, h * tq:(h + 1) * tq]))
        return jnp.concatenate(out, axis=1)

    row0 = pl.multiple_of(qi * tq, tq)
    ri = lax.broadcasted_iota(jnp.int32, (tq, ncol), 0)
    cq = lax.broadcasted_iota(jnp.int32, (tq, ncol), 1) % tq
    gated = jnp.logical_and(ri < blk, cq >= blk)
    s = _dot_nt(k_ref[0, pl.ds(row0, tq), :], q_all) + jnp.where(gated, jnp.concatenate(own_bias, axis=1), 0.0)
    s = jnp.where(ri <= cq, s, MASK_NEG)
    m = jnp.max(s, axis=0, keepdims=True)
    acc = values(row0, _bf(jnp.exp2(s - m)))

    lane_i = lax.broadcasted_iota(jnp.int32, (1, LANES), 1)

    def pair_rows(i):
        return pl.multiple_of(jnp.minimum(i, nb // 2 - 1) * tq, tq)

    def scores(i, s_ref):
        k2 = k_ref[0, pl.ds(pair_rows(i), tq), :]
        onehot = [jnp.broadcast_to(_bf(jnp.where(lane_i == jnp.where(i < qi, 2 * i + u, LANES - 1), 1.0, 0.0)),
                                   (blk, LANES)) for u in range(2)]
        k_ext = jnp.concatenate([k2, jnp.concatenate(onehot, axis=0)], axis=1)
        s_ref[...] = _dot_nt(k_ext, q_ext)

    def softmax(s_ref, p_ref, m):
        s = s_ref[...]
        m_new = jnp.maximum(m, jnp.max(s, axis=0, keepdims=True))
        p_ref[...] = _bf(jnp.exp2(s - m_new))
        return m_new, jnp.exp2(m - m_new)

    scores(0, s_a)
    p_b[...] = jnp.zeros_like(p_b)

    def body(t, carry):
        m, acc = carry
        scores(2 * t + 1, s_b)
        pv = values(pair_rows(jnp.maximum(2 * t - 1, 0)), p_b[...])
        m, alpha = softmax(s_a, p_a, m)
        acc = (acc + pv) * alpha
        scores(2 * t + 2, s_a)
        pv = values(pair_rows(2 * t), p_a[...])
        m, alpha = softmax(s_b, p_b, m)
        acc = (acc + pv) * alpha
        return m, acc

    trips = (qi + 1) // 2
    m, acc = lax.fori_loop(0, trips, body, (m, acc))
    acc = acc + values(pair_rows(jnp.maximum(2 * trips - 1, 0)), p_b[...])
    o = acc[:MB_HEAD] / acc[MB_HEAD:MB_HEAD + 1]
    o_ref[0] = jnp.concatenate([o[:, h * tq:(h + 1) * tq] for h in range(n_heads)], axis=0).T


def moba_attn(qn, kb, vt, ksum):
    bsz, seq, _ = qn.shape
    tq = 2 * MOBA_BLOCK
    nb = seq // MOBA_BLOCK
    assert seq % tq == 0 and nb <= LANES
    hp = MB_WIDTH // LANES
    ncol = (LANES // MB_HEAD) * tq
    return pl.pallas_call(
        _moba_attn_kernel,
        grid=(bsz, hp, seq // tq),
        in_specs=[pl.BlockSpec((1, tq, LANES), lambda b, h, i: (b, i, h)),
                  pl.BlockSpec((1, seq, LANES), lambda b, h, i: (b, 0, h)),
                  pl.BlockSpec((LANES, seq), lambda b, h, i: (h, b)),
                  pl.BlockSpec((1, nb, LANES), lambda b, h, i: (b, 0, h))],
        out_specs=pl.BlockSpec((1, tq, LANES), lambda b, h, i: (b, i, h)),
        out_shape=jax.ShapeDtypeStruct((bsz, seq, MB_WIDTH), F32),
        scratch_shapes=[pltpu.VMEM((tq, ncol), F32), pltpu.VMEM((tq, ncol), F32),
                        pltpu.VMEM((tq, ncol), BF16), pltpu.VMEM((tq, ncol), BF16)],
        compiler_params=_cparams("parallel", "parallel", "arbitrary"),
    )(qn, kb, vt, ksum)


def _moba_scores_kernel(pt_ref, q_ref, *refs, n_pg):
    k_refs = refs[:n_pg]
    s_o = refs[n_pg]
    q = q_ref[0]
    for j in range(n_pg):
        s_o[0, j] = jnp.sum(k_refs[j][0, 0] * q, axis=1)


def _moba_select_kernel(s_ref, q_ref, kn_ref, p_o, w_o, id_o, *, ppb):
    scale = MB_HEAD ** -0.5
    s = s_ref[0]
    n_lp = s.shape[0]
    nblk = n_lp // ppb
    page_sum = jnp.sum(s, axis=-1, keepdims=True)
    gate = jnp.sum(page_sum.reshape(nblk, ppb, MB_HEADS, 1), axis=1) * (1.0 / MOBA_BLOCK)
    idx = lax.broadcasted_iota(jnp.int32, gate.shape, 0).astype(F32)
    lane = lax.broadcasted_iota(jnp.int32, (MB_HEADS, LANES), 1)
    sel = jnp.zeros_like(gate)
    rem = gate
    ids = jnp.zeros((MB_HEADS, LANES), F32)
    for t in range(MOBA_TOPK):
        mx = jnp.max(rem, axis=0, keepdims=True)
        first = jnp.min(jnp.where(rem == mx, idx, 1e9), axis=0, keepdims=True)
        pick = idx == first
        sel = jnp.where(pick, 1.0, sel)
        rem = jnp.where(pick, MASK_NEG, rem)
        for u in range(ppb):
            ids = jnp.where(lane == t * ppb + u, first[0] * ppb + u, ids)
    id_o[0] = ids.astype(jnp.int32)
    sel_pg = jnp.broadcast_to(sel[:, None], (nblk, ppb, MB_HEADS, 1)).reshape(n_lp, MB_HEADS, 1) > 0.5
    s_own = jnp.sum(kn_ref[0] * q_ref[0], axis=-1, keepdims=True) * scale
    sm = jnp.where(sel_pg, s * scale, MASK_NEG)
    m_all = jnp.maximum(jnp.max(jnp.max(sm, axis=0), axis=-1, keepdims=True), s_own)
    p = jnp.where(sel_pg, jnp.exp(sm - m_all[None]), 0.0)
    w_own = jnp.exp(s_own - m_all)
    inv = 1.0 / (jnp.sum(jnp.sum(p, axis=0), axis=-1, keepdims=True) + w_own)
    p_o[0] = p * inv[None]
    w_o[0] = jnp.broadcast_to(w_own * inv, (MB_HEADS, LANES))


def _moba_values_kernel(pt_ref, id_ref, p_ref, w_ref, vn_ref, *refs, n_sel, hps):
    v_refs = refs[:hps * n_sel]
    o_ref = refs[hps * n_sel]
    b = pl.program_id(0)
    for hh in range(hps):
        h = pl.program_id(1) * hps + hh
        acc = w_ref[0, pl.ds(h, 1), 0:1] * vn_ref[0, hh]
        for j in range(n_sel):
            page = id_ref[(b * MB_HEADS + h) * n_sel + j]
            prob = p_ref[0, page, pl.ds(h, 1), :]
            acc = acc + _dot_nt(prob, v_refs[hh * n_sel + j][0, 0, 0], HI)
        o_ref[0, hh] = acc


def moba_decode(qn, kn, vn, k_cache_t, v_cache_t, page_table, layer):
    bsz = qn.shape[0]
    n_lp = page_table.shape[1]
    ppb = MOBA_BLOCK // PAGE_SIZE
    assert n_lp % ppb == 0 and n_lp // ppb >= MOBA_TOPK
    n_pg = 16
    assert n_lp % n_pg == 0
    n_sel = MOBA_TOPK * ppb
    hps = 4
    heads = lambda a: a.reshape(bsz, MB_HEADS, MB_HEAD)
    page_blk = (1, 1, MB_HEADS, MB_HEAD, PAGE_SIZE)

    scores = pl.pallas_call(
        functools.partial(_moba_scores_kernel, n_pg=n_pg),
        grid_spec=pltpu.PrefetchScalarGridSpec(
            num_scalar_prefetch=1,
            grid=(bsz, n_lp // n_pg),
            in_specs=[pl.BlockSpec((1, MB_HEADS, MB_HEAD, 1), lambda b, g, pt: (b, 0, 0, 0))]
                     + [pl.BlockSpec(page_blk, functools.partial(
                         lambda b, g, pt, j: (layer, pt[b, g * n_pg + j], 0, 0, 0), j=j)) for j in range(n_pg)],
            out_specs=pl.BlockSpec((1, n_pg, MB_HEADS, PAGE_SIZE), lambda b, g, pt: (b, g, 0, 0))),
        out_shape=jax.ShapeDtypeStruct((bsz, n_lp, MB_HEADS, PAGE_SIZE), F32),
        compiler_params=_cparams("parallel", "arbitrary"),
    )(page_table, qn.reshape(bsz, MB_HEADS, MB_HEAD, 1), *([k_cache_t] * n_pg))

    vec = pl.BlockSpec((1, MB_HEADS, MB_HEAD), lambda b: (b, 0, 0))
    probs, w_own, ids = pl.pallas_call(
        functools.partial(_moba_select_kernel, ppb=ppb),
        grid=(bsz,),
        in_specs=[pl.BlockSpec((1, n_lp, MB_HEADS, PAGE_SIZE), lambda b: (b, 0, 0, 0)), vec, vec],
        out_specs=[pl.BlockSpec((1, n_lp, MB_HEADS, PAGE_SIZE), lambda b: (b, 0, 0, 0)),
                   pl.BlockSpec((1, MB_HEADS, LANES), lambda b: (b, 0, 0)),
                   pl.BlockSpec((1, MB_HEADS, LANES), lambda b: (b, 0, 0))],
        out_shape=[jax.ShapeDtypeStruct((bsz, n_lp, MB_HEADS, PAGE_SIZE), F32),
                   jax.ShapeDtypeStruct((bsz, MB_HEADS, LANES), F32),
                   jax.ShapeDtypeStruct((bsz, MB_HEADS, LANES), jnp.int32)],
        compiler_params=_cparams("parallel"),
    )(scores, heads(qn), heads(kn))

    sel_ids = ids[:, :, :n_sel].reshape(-1)
    out = pl.pallas_call(
        functools.partial(_moba_values_kernel, n_sel=n_sel, hps=hps),
        grid_spec=pltpu.PrefetchScalarGridSpec(
            num_scalar_prefetch=2,
            grid=(bsz, MB_HEADS // hps),
            in_specs=[pl.BlockSpec((1, n_lp, MB_HEADS, PAGE_SIZE), lambda b, g, pt, sid: (b, 0, 0, 0)),
                      pl.BlockSpec((1, MB_HEADS, LANES), lambda b, g, pt, sid: (b, 0, 0)),
                      pl.BlockSpec((1, hps, 1, MB_HEAD), lambda b, g, pt, sid: (b, g, 0, 0))]
                     + [pl.BlockSpec((1, 1, 1, MB_HEAD, PAGE_SIZE), functools.partial(
                         lambda b, g, pt, sid, hh, j: (
                             layer, pt[b, sid[(b * MB_HEADS + g * hps + hh) * n_sel + j]], g * hps + hh, 0, 0),
                         hh=hh, j=j)) for hh in range(hps) for j in range(n_sel)],
            out_specs=pl.BlockSpec((1, hps, 1, MB_HEAD), lambda b, g, pt, sid: (b, g, 0, 0))),
        out_shape=jax.ShapeDtypeStruct((bsz, MB_HEADS, 1, MB_HEAD), F32),
        compiler_params=_cparams("parallel", "arbitrary"),
    )(page_table, sel_ids, probs, w_own, vn.reshape(bsz, MB_HEADS, 1, MB_HEAD), *([v_cache_t] * (hps * n_sel)))
    return out.reshape(bsz, MB_WIDTH)


def _hgrn_constants(c):
    rows = np.arange(c)
    dmats, ups, los, masks = [], [], [], []
    z = c
    while z >= 2:
        seg = rows // z
        mid = seg * z + z // 2
        up = rows >= mid
        d = np.zeros((c, c), np.float32)
        for i in rows:
            if up[i]:
                d[i, mid[i]:i + 1] = 1.0
            else:
                d[i, i + 1:mid[i]] = 1.0
        dmats.append(d)
        ups.append(np.repeat(up[:, None], HG_D, 1).astype(np.float32))
        los.append(np.repeat(~up[:, None], HG_D, 1).astype(np.float32))
        masks.append(((seg[:, None] == seg[None, :]) & up[:, None] & (~up)[None, :]).astype(np.float32))
        z //= 2
    tri = np.tril(np.ones((c, c), np.float32))
    dmats += [tri, 1.0 - tri]
    return (jnp.asarray(np.concatenate(dmats, 0), dtype=BF16), jnp.asarray(np.stack(ups)),
            jnp.asarray(np.stack(los)), jnp.asarray(np.stack(masks)))


def _hgrn_kernel(pq_ref, pf_ref, pi_ref, pg_ref, lbl_ref, gain_ref, s0_ref, dd_ref, up_ref, lo_ref,
                 mk_ref, o_ref, sout_ref, st_sc, *, chunk, layer):
    l = pl.program_id(2)
    tl = pq_ref.shape[1]
    c = chunk
    nc = tl // c
    nlev = up_ref.shape[0]
    er = lax.broadcasted_iota(jnp.int32, (HG_D, HG_D), 0)
    ec = lax.broadcasted_iota(jnp.int32, (HG_D, HG_D), 1)
    eye_d = (er == ec).astype(F32)

    @pl.when(l == 0)
    def _():
        st_sc[...] = _dot_nt(eye_d, s0_ref[0, 0], HI)

    logits = lbl_ref[...]
    ex = jnp.exp(logits - jnp.max(logits, axis=0, keepdims=True))
    prob = ex / jnp.sum(ex, axis=0, keepdims=True)
    lb = jnp.sum(prob[0:layer + 1], axis=0, keepdims=True) - prob[0:1]

    qr = pq_ref[0]
    q = qr * _sigmoid(qr)
    forget = lb + (1.0 - lb) * _sigmoid(pf_ref[0])
    logf = jnp.log(jnp.maximum(forget, 1e-30))
    kg = 1.0 - forget
    v = pi_ref[0]

    ri = lax.broadcasted_iota(jnp.int32, (c, c), 0)
    ci = lax.broadcasted_iota(jnp.int32, (c, c), 1)
    diag = ri == ci
    rows = [slice(i * c, (i + 1) * c) for i in range(nc)]
    e_all = jnp.exp(_split_dot(dd_ref[...], jnp.concatenate([logf[rs] for rs in rows], axis=1)))
    intra, kv, qe, dec = [], [], [], []
    for i, rs in enumerate(rows):
        qc, kc, vc = q[rs], kg[rs], v[rs]
        e = e_all[:, i * HG_D:(i + 1) * HG_D]
        a = jnp.where(diag, jnp.sum(qc * kc, axis=1, keepdims=True), 0.0)
        for lev in range(nlev):
            el = e[lev * c:(lev + 1) * c]
            a = a + mk_ref[lev] * _dot_nt(_bf(qc * el * up_ref[lev]), _bf(kc * el * lo_ref[lev]))
        e_in = e[nlev * c:(nlev + 1) * c]
        e_out = e[(nlev + 1) * c:(nlev + 2) * c]
        intra.append(_dot(_bf(a), _bf(vc)))
        kv.append(_dot_tn(_bf(vc), _bf(kc * e_out)))
        qe.append(_bf(qc * e_in))
        dec.append(e_in[c - 1:c, :])
    st = st_sc[...]
    outs = []
    for i in range(nc):
        outs.append(_dot_nt(qe[i], _bf(st)) + intra[i])
        st = st * dec[i] + kv[i]
    st_sc[...] = st
    o = jnp.concatenate(outs, axis=0) if nc > 1 else outs[0]
    ms = jnp.mean(o * o, axis=-1, keepdims=True)
    gr = pg_ref[0]
    o_ref[0] = o * lax.rsqrt(ms + RMS_EPS) * gain_ref[...] * (gr * _sigmoid(gr))

    @pl.when(l == pl.num_programs(2) - 1)
    def _():
        sout_ref[0, 0] = _dot_nt(eye_d, st, HI)


def hgrn_scan(p, lb_logits, out_gain, s0, layer, tl, chunk):
    bsz, seq, _ = p.shape
    assert seq % tl == 0 and tl % chunk == 0
    dd, up, lo, mk = _hgrn_constants(chunk)
    col = lambda off: pl.BlockSpec((1, tl, HG_D), lambda b, h, li: (b, li, off + h))
    full = lambda a: pl.BlockSpec(a.shape, lambda b, h, li: (0,) * a.ndim)
    st_spec = pl.BlockSpec((1, 1, HG_D, HG_D), lambda b, h, li: (b, h, 0, 0))
    return pl.pallas_call(
        functools.partial(_hgrn_kernel, chunk=chunk, layer=layer),
        grid=(bsz, HG_HEADS, seq // tl),
        in_specs=[col(0), col(HG_HEADS), col(2 * HG_HEADS), col(3 * HG_HEADS),
                  pl.BlockSpec((lb_logits.shape[0], HG_D), lambda b, h, li: (0, h)),
                  pl.BlockSpec((1, HG_D), lambda b, h, li: (0, 0)),
                  st_spec, full(dd), full(up), full(lo), full(mk)],
        out_specs=[pl.BlockSpec((1, tl, HG_D), lambda b, h, li: (b, li, h)), st_spec],
        out_shape=[jax.ShapeDtypeStruct((bsz, seq, D_MODEL), F32),
                   jax.ShapeDtypeStruct((bsz, HG_HEADS, HG_D, HG_D), F32)],
        scratch_shapes=[pltpu.VMEM((HG_D, HG_D), F32)],
        compiler_params=_cparams("parallel", "parallel", "arbitrary"),
    )(p, p, p, p, lb_logits, out_gain.reshape(1, HG_D), s0, dd, up, lo, mk)


def _row_to_col(row, eye):
    return jnp.sum(jnp.where(eye, row, 0.0), axis=1, keepdims=True)


def _col_to_row(col, eye):
    return jnp.sum(jnp.where(eye, col, 0.0), axis=0, keepdims=True)


def _rwkv_step_kernel(r_ref, lw_ref, k_ref, v_ref, kk_ref, b_ref, g_ref, lnw_ref, lnb_ref, rk_ref,
                      s0_ref, o_ref, sout_ref):
    hd = RW_HEAD
    ri = lax.broadcasted_iota(jnp.int32, (hd, hd), 0)
    ci = lax.broadcasted_iota(jnp.int32, (hd, hd), 1)
    eye = ri == ci
    for h in range(RW_HEADS):
        sl = slice(h * hd, (h + 1) * hd)
        row = lambda ref: ref[0][:, sl]
        r, k, v, kk, b = row(r_ref), row(k_ref), row(v_ref), row(kk_ref), row(b_ref)
        s = s0_ref[0, h]
        s_kk = jnp.sum(s * kk, axis=1, keepdims=True)
        s = s * jnp.exp(row(lw_ref)) - s_kk * b + _row_to_col(v, eye) * k
        sout_ref[0, h] = s
        o = _col_to_row(jnp.sum(s * r, axis=1, keepdims=True), eye)
        mean = jnp.mean(o, axis=-1, keepdims=True)
        var = jnp.mean(jnp.square(o - mean), axis=-1, keepdims=True)
        o = (o - mean) * lax.rsqrt(var + GN_EPS) * lnw_ref[:, sl] + lnb_ref[:, sl]
        bonus = jnp.sum(r * k * rk_ref[:, sl], axis=-1, keepdims=True) * v
        o_ref[0, :, sl] = (o + bonus) * row(g_ref)


def rwkv_step(r, lw, k, v, kk, b, g, ln_w, ln_b, r_k, s0):
    bsz = r.shape[0]
    seq_spec = pl.BlockSpec((1, 1, RW_WIDTH), lambda i: (i, 0, 0))
    par_spec = pl.BlockSpec((1, RW_WIDTH), lambda i: (0, 0))
    st_spec = pl.BlockSpec((1, RW_HEADS, RW_HEAD, RW_HEAD), lambda i: (i, 0, 0, 0))
    row = lambda a: a.reshape(1, RW_WIDTH)
    tok = lambda a: a.reshape(bsz, 1, RW_WIDTH)
    o, s = pl.pallas_call(
        _rwkv_step_kernel,
        grid=(bsz,),
        in_specs=[seq_spec] * 7 + [par_spec] * 3 + [st_spec],
        out_specs=[seq_spec, st_spec],
        out_shape=[jax.ShapeDtypeStruct((bsz, 1, RW_WIDTH), F32),
                   jax.ShapeDtypeStruct((bsz, RW_HEADS, RW_HEAD, RW_HEAD), F32)],
        compiler_params=_cparams("parallel"),
    )(*[tok(a) for a in (r, lw, k, v, kk, b, g)], row(ln_w), row(ln_b), row(r_k), s0)
    return o.reshape(bsz, RW_WIDTH), s


def _hgrn_step_kernel(p_ref, lbl_ref, gain_ref, s0_ref, o_ref, sout_ref, *, layer):
    d = HG_D
    ri = lax.broadcasted_iota(jnp.int32, (d, d), 0)
    ci = lax.broadcasted_iota(jnp.int32, (d, d), 1)
    eye = ri == ci
    logits = lbl_ref[...]
    ex = jnp.exp(logits - jnp.max(logits, axis=0, keepdims=True))
    prob = ex / jnp.sum(ex, axis=0, keepdims=True)
    lb_all = jnp.sum(prob[0:layer + 1], axis=0, keepdims=True) - prob[0:1]
    width = HG_HEADS * d
    for h in range(HG_HEADS):
        col = lambda part: p_ref[0][:, part * width + h * d:part * width + (h + 1) * d]
        qr, fr, v, gr = col(0), col(1), col(2), col(3)
        lb = lb_all[:, h * d:(h + 1) * d]
        q = qr * _sigmoid(qr)
        forget = lb + (1.0 - lb) * _sigmoid(fr)
        decay = jnp.maximum(forget, 1e-30)
        s = _row_to_col(decay, eye) * s0_ref[0, h] + _row_to_col(1.0 - forget, eye) * v
        sout_ref[0, h] = s
        o = jnp.sum(_row_to_col(q, eye) * s, axis=0, keepdims=True)
        ms = jnp.mean(o * o, axis=-1, keepdims=True)
        o_ref[0, :, h * d:(h + 1) * d] = o * lax.rsqrt(ms + RMS_EPS) * gain_ref[...] * (gr * _sigmoid(gr))


def hgrn_step(p, lb_logits, out_gain, s0, layer):
    bsz = p.shape[0]
    st_spec = pl.BlockSpec((1, HG_HEADS, HG_D, HG_D), lambda i: (i, 0, 0, 0))
    o, s = pl.pallas_call(
        functools.partial(_hgrn_step_kernel, layer=layer),
        grid=(bsz,),
        in_specs=[pl.BlockSpec((1, 1, p.shape[1]), lambda i: (i, 0, 0)),
                  pl.BlockSpec(lb_logits.shape, lambda i: (0, 0)),
                  pl.BlockSpec((1, HG_D), lambda i: (0, 0)), st_spec],
        out_specs=[pl.BlockSpec((1, 1, D_MODEL), lambda i: (i, 0, 0)), st_spec],
        out_shape=[jax.ShapeDtypeStruct((bsz, 1, D_MODEL), F32),
                   jax.ShapeDtypeStruct((bsz, HG_HEADS, HG_D, HG_D), F32)],
        compiler_params=_cparams("parallel"),
    )(p.reshape(bsz, 1, -1), lb_logits, out_gain.reshape(1, HG_D), s0)
    return o.reshape(bsz, D_MODEL), s


def _even_layer(xp, xs, il, vf_p, vf_s, w, cache_k, cache_v, page_table, state_rwkv, state_shift):
    nb_p, seq, d = xp.shape
    nb_s = xs.shape[0]
    tp = nb_p * seq
    rw_params = (w['rw_mu'][il], w['rw_w0'][il], w['w2p'][il], w['rw_a0'][il], w['a2p'][il],
                 w['rw_g2'][il], w['rw_k_k'][il], w['rw_k_a'][il])
    has_vres = il > 0
    vres_w = (w['rw_v0'][il - 1], w['rw_v1'][il - 1], w['rw_v2'][il - 1]) if has_vres else None
    ln = (w['rw_ln_w'][il], w['rw_ln_b'][il], w['rw_r_k'][il].reshape(-1))

    p_rw, p_mb = norm_matmul(xp.reshape(tp, d), w['norm_mix'][2 * il], w['w_in_even'][il],
                             (RW_IN, 3 * MB_WIDTH), 512)
    vres = (vf_p,) + vres_w if has_vres else None
    prep = rwkv_prep(p_rw, None, rw_params, vres, True, seq, 512)
    if not has_vres:
        vf_p = prep[7]
    seq3 = lambda a: a.reshape(nb_p, seq, -1)
    o_rw, wkv_p = rwkv_scan(*[seq3(a) for a in prep[:7]], *ln,
                            jnp.zeros((nb_p, RW_HEADS, RW_HEAD, RW_HEAD), F32), 1024, RW_CHUNK)
    qn, kn, kb, vt, ksum = moba_prep(p_mb, w['mb_q_norm'][il], w['mb_k_norm'][il], MOBA_BLOCK)
    o_mb = moba_attn(seq3(qn), seq3(kb), vt, ksum.reshape(nb_p, seq // MOBA_BLOCK, MB_WIDTH))
    lyr = 2 * il
    yp = mix_ffn(xp.reshape(tp, d), [o_rw.reshape(tp, RW_WIDTH), o_mb.reshape(tp, MB_WIDTH)],
                 [w['w_out_even'][il][:RW_WIDTH], w['w_out_even'][il][RW_WIDTH:]],
                 w['norm_ffn'][lyr], w['ffn_w_gate'][lyr], w['ffn_w_up'][lyr], w['ffn_w_down'][lyr],
                 512, D_FF // 2).reshape(nb_p, seq, d)
    mk_p = kn.reshape(nb_p, seq, MB_HEADS, MB_HEAD)
    mv_p = p_mb[:, 2 * MB_WIDTH:].reshape(nb_p, seq, MB_HEADS, MB_HEAD)
    sh_p = p_rw.reshape(nb_p, seq, RW_IN)[:, -1]

    ps_rw, ps_mb = norm_matmul(xs.reshape(nb_s, d), w['norm_mix'][2 * il], w['w_in_even'][il],
                               (RW_IN, 3 * MB_WIDTH), nb_s)
    vres = (vf_s,) + vres_w if has_vres else None
    prep = rwkv_prep(ps_rw, state_shift[il], rw_params, vres, False, 1, nb_s)
    if not has_vres:
        vf_s = prep[7]
    os_rw, wkv_s = rwkv_step(*prep[:7], *ln, state_rwkv[il])
    qs, ks, _, _, _ = moba_prep(ps_mb, w['mb_q_norm'][il], w['mb_k_norm'][il], nb_s)
    vs = ps_mb[:, 2 * MB_WIDTH:]
    os_mb = moba_decode(qs, ks, vs, cache_k, cache_v, page_table, il)
    ys = mix_ffn(xs.reshape(nb_s, d), [os_rw, os_mb],
                 [w['w_out_even'][il][:RW_WIDTH], w['w_out_even'][il][RW_WIDTH:]],
                 w['norm_ffn'][lyr], w['ffn_w_gate'][lyr], w['ffn_w_up'][lyr], w['ffn_w_down'][lyr],
                 nb_s, D_FF // 2).reshape(nb_s, 1, d)
    mk_s = ks.reshape(nb_s, 1, MB_HEADS, MB_HEAD)
    mv_s = vs.reshape(nb_s, 1, MB_HEADS, MB_HEAD)
    return yp, ys, vf_p, vf_s, (mk_p, mv_p, mk_s, mv_s, wkv_p, wkv_s, sh_p, ps_rw)


def _odd_layer(xp, xs, il, w, state_hgrn):
    nb_p, seq, d = xp.shape
    nb_s = xs.shape[0]
    tp = nb_p * seq
    lyr = 2 * il + 1
    (pp,) = norm_matmul(xp.reshape(tp, d), w['norm_mix'][lyr], w['w_in_odd'][il], (4 * d,), 512)
    o_p, hg_p = hgrn_scan(pp.reshape(nb_p, seq, 4 * d), w['hg_lb_logits'], w['hg_out_norm'][il],
                          jnp.zeros((nb_p, HG_HEADS, HG_D, HG_D), F32), il, 1024, HG_CHUNK)
    yp = mix_ffn(xp.reshape(tp, d), [o_p.reshape(tp, d)], [w['w_out_odd'][il]],
                 w['norm_ffn'][lyr], w['ffn_w_gate'][lyr], w['ffn_w_up'][lyr], w['ffn_w_down'][lyr],
                 512, D_FF // 2).reshape(nb_p, seq, d)
    (ps,) = norm_matmul(xs.reshape(nb_s, d), w['norm_mix'][lyr], w['w_in_odd'][il], (4 * d,), nb_s)
    o_s, hg_s = hgrn_step(ps, w['hg_lb_logits'], w['hg_out_norm'][il], state_hgrn[il], il)
    ys = mix_ffn(xs.reshape(nb_s, d), [o_s], [w['w_out_odd'][il]],
                 w['norm_ffn'][lyr], w['ffn_w_gate'][lyr], w['ffn_w_up'][lyr], w['ffn_w_down'][lyr],
                 nb_s, D_FF // 2).reshape(nb_s, 1, d)
    return yp, ys, hg_p, hg_s


def kernel(x_prompt, x_sample, cache_moba_k, cache_moba_v, page_table, state_rwkv, state_rwkv_shift, state_hgrn, norm_mix, norm_ffn, w_in_even, w_out_even, rw_mu, rw_w0, rw_w2, rw_a0, rw_a2, rw_g2, rw_k_k, rw_k_a, rw_r_k, rw_ln_w, rw_ln_b, rw_v0, rw_v1, rw_v2, mb_q_norm, mb_k_norm, w_in_odd, w_out_odd, hg_lb_logits, hg_out_norm, ffn_w_gate, ffn_w_up, ffn_w_down):
    depth = norm_mix.shape[0]
    zeros_lora = jnp.zeros_like(rw_w2)
    w = dict(
        norm_mix=norm_mix, norm_ffn=norm_ffn,
        w_in_even=w_in_even.astype(BF16), w_out_even=w_out_even.astype(BF16),
        w_in_odd=w_in_odd.astype(BF16), w_out_odd=w_out_odd.astype(BF16),
        ffn_w_gate=ffn_w_gate.astype(BF16), ffn_w_up=ffn_w_up.astype(BF16),
        ffn_w_down=ffn_w_down.astype(BF16),
        rw_mu=rw_mu, rw_w0=rw_w0, rw_a0=rw_a0, rw_g2=rw_g2, rw_k_k=rw_k_k, rw_k_a=rw_k_a,
        rw_r_k=rw_r_k, rw_ln_w=rw_ln_w, rw_ln_b=rw_ln_b, rw_v0=rw_v0, rw_v1=rw_v1, rw_v2=rw_v2,
        w2p=jnp.concatenate([rw_w2, zeros_lora], axis=1), a2p=jnp.concatenate([zeros_lora, rw_a2], axis=1),
        mb_q_norm=mb_q_norm, mb_k_norm=mb_k_norm, hg_lb_logits=hg_lb_logits, hg_out_norm=hg_out_norm)
    page_table = page_table.astype(jnp.int32)
    cache_k = jnp.transpose(cache_moba_k, (0, 1, 3, 4, 2))
    cache_v = jnp.transpose(cache_moba_v, (0, 1, 3, 4, 2))

    xp, xs = x_prompt, x_sample
    vf_p = vf_s = None
    even_out, hg_out = [], []
    for layer in range(depth):
        il = layer // 2
        if layer % 2 == 0:
            xp, xs, vf_p, vf_s, outs = _even_layer(xp, xs, il, vf_p, vf_s, w, cache_k, cache_v,
                                                   page_table, state_rwkv, state_rwkv_shift)
            even_out.append(outs)
        else:
            xp, xs, hg_p, hg_s = _odd_layer(xp, xs, il, w, state_hgrn)
            hg_out.append((hg_p, hg_s))
    stack = lambda i: jnp.stack([o[i] for o in even_out])
    return (xp, xs, stack(0), stack(1), stack(2), stack(3), stack(4), stack(5), stack(6), stack(7),
            jnp.stack([o[0] for o in hg_out]), jnp.stack([o[1] for o in hg_out]))
```

```python
import functools

import numpy as np
import jax
import jax.numpy as jnp
from jax import lax
from jax.experimental import pallas as pl
from jax.experimental.pallas import tpu as pltpu

F32 = jnp.float32
BF16 = jnp.bfloat16
HI = lax.Precision.HIGHEST

D_MODEL = 1024
PAGE_SIZE = 128
RW_HEAD = 64
RW_WIDTH = 512
RW_HEADS = 8
RW_IN = 1792
MB_HEAD = 64
MB_WIDTH = 512
MB_HEADS = 8
MOBA_BLOCK = 256
MOBA_TOPK = 3
HG_HEADS = 8
HG_D = 128
D_FF = 2816
RMS_EPS = 1e-6
GN_EPS = 64e-5
MASK_NEG = -1e30
LOG2E = 1.4426950408889634

LANES = 128
VMEM_LIMIT = 56 * 1024 * 1024
RW_CHUNK = 64
HG_CHUNK = 64
HG_WIDE_SEG = 16


def _cparams(*sem):
    return pltpu.CompilerParams(dimension_semantics=sem, vmem_limit_bytes=VMEM_LIMIT)


def _bf(x):
    return x.astype(BF16)


def _dot(a, b, precision=None):
    return jnp.dot(a, b, precision=precision, preferred_element_type=F32)


def _dot_nt(a, b, precision=None):
    return lax.dot_general(a, b, (((1,), (1,)), ((), ())), precision=precision,
                           preferred_element_type=F32)


def _dot_tn(a, b, precision=None):
    return lax.dot_general(a, b, (((0,), (0,)), ((), ())), precision=precision,
                           preferred_element_type=F32)


def _bmm(a, b):
    return lax.dot_general(_bf(a), _bf(b), (((2,), (1,)), ((0,), (0,))), preferred_element_type=F32)


def _bmm_nt(a, b):
    return lax.dot_general(_bf(a), _bf(b), (((2,), (2,)), ((0,), (0,))), preferred_element_type=F32)


def _bmm_tn(a, b):
    return lax.dot_general(_bf(a), _bf(b), (((1,), (1,)), ((0,), (0,))), preferred_element_type=F32)


def _split_dot(a_bf16, b):
    n = b.shape[1]
    hi = b.astype(BF16)
    lo = (b - hi.astype(F32)).astype(BF16)
    out = _dot(a_bf16, jnp.concatenate([hi, lo], axis=1))
    return out[:, :n] + out[:, n:]


def _hi_lo(x):
    hi = x.astype(BF16)
    return hi, (x - hi.astype(F32)).astype(BF16)


def _dot_x3(a, b):
    a_hi, a_lo = _hi_lo(a)
    b_hi, b_lo = _hi_lo(b)
    return _dot(a_hi, b_hi) + (_dot(a_hi, b_lo) + _dot(a_lo, b_hi))


def _dot_x3_nt(a, b):
    a_hi, a_lo = _hi_lo(a)
    b_hi, b_lo = _hi_lo(b)
    return _dot_nt(a_hi, b_hi) + (_dot_nt(a_hi, b_lo) + _dot_nt(a_lo, b_hi))


def _dot_x2(a, g_bf16):
    a_hi, a_lo = _hi_lo(a)
    return _dot(a_hi, g_bf16) + _dot(a_lo, g_bf16)


def _sigmoid(x):
    return 1.0 / (1.0 + jnp.exp(-x))


def _group_matrices(width, group):
    g = np.zeros((width, LANES), np.float32)
    g[np.arange(width), np.arange(width) // group] = 1.0
    return jnp.asarray(g, dtype=BF16), jnp.asarray(g.T.copy(), dtype=BF16)


def _norm_matmul_kernel(x_ref, gain_ref, w_ref, *out_refs, widths, tn):
    x = x_ref[...]
    ms = jnp.mean(x * x, axis=-1, keepdims=True)
    h = (x * lax.rsqrt(ms + RMS_EPS) * gain_ref[...]).astype(BF16)
    col = 0
    for o_ref, width in zip(out_refs, widths):
        for c0 in range(0, width, tn):
            sz = min(tn, width - c0)
            o_ref[:, c0:c0 + sz] = _dot(h, w_ref[:, col + c0:col + c0 + sz])
        col += width


def norm_matmul(x, gain, w_bf16, widths, tm):
    m, d = x.shape
    n = w_bf16.shape[1]
    assert sum(widths) == n and m % tm == 0
    kern = functools.partial(_norm_matmul_kernel, widths=tuple(widths), tn=512)
    return pl.pallas_call(
        kern,
        grid=(m // tm,),
        in_specs=[pl.BlockSpec((tm, d), lambda i: (i, 0)),
                  pl.BlockSpec((1, d), lambda i: (0, 0)),
                  pl.BlockSpec((d, n), lambda i: (0, 0))],
        out_specs=[pl.BlockSpec((tm, wd), lambda i: (i, 0)) for wd in widths],
        out_shape=[jax.ShapeDtypeStruct((m, wd), F32) for wd in widths],
        compiler_params=_cparams("parallel"),
    )(x, gain.reshape(1, d), w_bf16)


def _mix_ffn_kernel(*refs, n_act):
    x_ref = refs[0]
    a_refs = refs[1:1 + n_act]
    wo_refs = refs[1 + n_act:1 + 2 * n_act]
    gain_ref, wg_ref, wu_ref, wd_ref, y_ref, x1_sc, h_sc, acc_sc = refs[1 + 2 * n_act:]
    j = pl.program_id(1)

    @pl.when(j == 0)
    def _():
        x1 = x_ref[...]
        for a_ref, wo_ref in zip(a_refs, wo_refs):
            x1 = x1 + _dot(a_ref[...].astype(BF16), wo_ref[...])
        x1_sc[...] = x1
        ms = jnp.mean(x1 * x1, axis=-1, keepdims=True)
        h_sc[...] = (x1 * lax.rsqrt(ms + RMS_EPS) * gain_ref[...]).astype(BF16)
        acc_sc[...] = jnp.zeros_like(acc_sc)

    h = h_sc[...]
    g = _dot(h, wg_ref[...])
    u = _dot(h, wu_ref[...])
    act = (g * _sigmoid(g) * u).astype(BF16)
    acc_sc[...] += _dot(act, wd_ref[...])

    @pl.when(j == pl.num_programs(1) - 1)
    def _():
        y_ref[...] = x1_sc[...] + acc_sc[...]


def mix_ffn(x, acts, wos_bf16, gain, wg, wu, wd, tm, tf):
    m, d = x.shape
    f = wg.shape[1]
    assert m % tm == 0 and f % tf == 0
    n_act = len(acts)
    in_specs = [pl.BlockSpec((tm, d), lambda i, j: (i, 0))]
    in_specs += [pl.BlockSpec((tm, a.shape[1]), lambda i, j: (i, 0)) for a in acts]
    in_specs += [pl.BlockSpec(w.shape, lambda i, j: (0, 0)) for w in wos_bf16]
    in_specs += [pl.BlockSpec((1, d), lambda i, j: (0, 0)),
                 pl.BlockSpec((d, tf), lambda i, j: (0, j)),
                 pl.BlockSpec((d, tf), lambda i, j: (0, j)),
                 pl.BlockSpec((tf, d), lambda i, j: (j, 0))]
    return pl.pallas_call(
        functools.partial(_mix_ffn_kernel, n_act=n_act),
        grid=(m // tm, f // tf),
        in_specs=in_specs,
        out_specs=pl.BlockSpec((tm, d), lambda i, j: (i, 0)),
        out_shape=jax.ShapeDtypeStruct((m, d), F32),
        scratch_shapes=[pltpu.VMEM((tm, d), F32), pltpu.VMEM((tm, d), BF16), pltpu.VMEM((tm, d), F32)],
        compiler_params=_cparams("parallel", "arbitrary"),
    )(x, *acts, *wos_bf16, gain.reshape(1, d), wg, wu, wd)


def _rwkv_prep_kernel(*refs, shift_mode, has_vres, tiles_per_seq):
    it = iter(refs)
    p_ref = next(it)
    prev_ref = next(it)
    mu_ref, w0_ref, a0_ref, kk_ref, ka_ref, w2_ref, a2_ref, g2_ref, grp_ref, grpt_ref = (
        next(it) for _ in range(10))
    if has_vres:
        vf_ref, v0_ref, v1_ref, v2_ref = (next(it) for _ in range(4))
    r_o, lw_o, k_o, v_o, kkn_o, b_o, g_o = (next(it) for _ in range(7))
    if not has_vres:
        vf_o = next(it)

    pf = p_ref[...]
    if shift_mode:
        first = (pl.program_id(0) % tiles_per_seq) == 0
        prev_row = jnp.where(first, 0.0, prev_ref[7:8, :])
        rows = lax.broadcasted_iota(jnp.int32, pf.shape, 0)
        prev = jnp.where(rows == 0, prev_row, pltpu.roll(pf, 1, axis=0))
    else:
        prev = prev_ref[...]
    xs = pf + (prev - pf) * mu_ref[...]
    r = xs[:, 0:RW_WIDTH]
    k = xs[:, RW_WIDTH:2 * RW_WIDTH]
    v = xs[:, 2 * RW_WIDTH:3 * RW_WIDTH]
    wa_lo = xs[:, 3 * RW_WIDTH:3 * RW_WIDTH + LANES]
    g_lo = xs[:, 3 * RW_WIDTH + LANES:RW_IN]
    z = w0_ref[...] + _dot_x3(jnp.tanh(wa_lo), w2_ref[...])
    sp = jnp.maximum(-z, 0.0) + jnp.log(1.0 + jnp.exp(-jnp.abs(z)))
    logw = -jnp.exp(-sp - 0.5)
    a = _sigmoid(a0_ref[...] + _dot_x3(wa_lo, a2_ref[...]))
    g = _dot_x3(_sigmoid(g_lo), g2_ref[...])
    if has_vres:
        gate = _sigmoid(v0_ref[...] + _dot_x3(_dot_x3(v, v1_ref[...]), v2_ref[...]))
        v = v + (vf_ref[...] - v) * gate
    else:
        vf_o[...] = v
    kk = k * kk_ref[...]
    ss = _dot_x2(kk * kk, grp_ref[...])
    inv = _dot_x2(lax.rsqrt(jnp.maximum(ss, 1e-24)), grpt_ref[...])
    kk = kk * inv
    k = k * (1.0 + (a - 1.0) * ka_ref[...])
    r_o[...] = r
    lw_o[...] = logw
    k_o[...] = k
    v_o[...] = v
    kkn_o[...] = kk
    b_o[...] = kk * a
    g_o[...] = g


def rwkv_prep(p_rw, prev, params, vres, shift_mode, seq_len, tm):
    t = p_rw.shape[0]
    assert t % tm == 0
    mu, w0, w2p, a0, a2p, g2, k_k, k_a = params
    grp, grpt = _group_matrices(RW_WIDTH, RW_HEAD)
    row = lambda a: a.reshape(1, -1)
    tile = lambda wd: pl.BlockSpec((tm, wd), lambda i: (i, 0))
    full = lambda a: pl.BlockSpec(a.shape, lambda i: (0,) * a.ndim)
    args = [p_rw]
    specs = [tile(RW_IN)]
    if shift_mode:
        r8 = tm // 8
        args.append(p_rw)
        specs.append(pl.BlockSpec((8, RW_IN), lambda i: (jnp.maximum(i * r8 - 1, 0), 0)))
    else:
        args.append(prev)
        specs.append(tile(RW_IN))
    small = [row(mu), row(w0), row(a0), row(k_k), row(k_a), w2p, a2p, g2, grp, grpt]
    args += small
    specs += [full(a) for a in small]
    has_vres = vres is not None
    if has_vres:
        vf, v0, v1, v2 = vres
        extra = [row(v0), v1, v2]
        args += [vf] + extra
        specs += [tile(RW_WIDTH)] + [full(a) for a in extra]
    n_out = 7 if has_vres else 8
    kern = functools.partial(_rwkv_prep_kernel, shift_mode=shift_mode, has_vres=has_vres,
                             tiles_per_seq=max(seq_len // tm, 1))
    outs = pl.pallas_call(
        kern,
        grid=(t // tm,),
        in_specs=specs,
        out_specs=[tile(RW_WIDTH)] * n_out,
        out_shape=[jax.ShapeDtypeStruct((t, RW_WIDTH), F32)] * n_out,
        compiler_params=_cparams("parallel"),
    )(*args)
    return outs


def _unit_lower_inverse(n, c):
    ri = lax.broadcasted_iota(jnp.int32, (c, c), 0)
    ci = lax.broadcasted_iota(jnp.int32, (c, c), 1)
    eye = (ri == ci).astype(F32)[None]
    blk = min(16, c)
    same = ((ri // blk) == (ci // blk))[None]
    nd = jnp.where(same, n, 0.0)
    x = eye - nd
    pw = nd
    span = 2
    while span < blk:
        pw = _bmm(pw, pw)
        x = _bmm(x, eye + pw)
        span *= 2
    if c > blk:
        p = _bmm(x, jnp.where(same, 0.0, n))
        y = eye - p
        pw = p
        span = 2
        while span < c // blk:
            pw = _bmm(pw, pw)
            y = _bmm(y, eye + pw)
            span *= 2
        x = _bmm(y, x)
    return x


def _rwkv_scan_kernel(r_ref, lw_ref, k_ref, v_ref, kk_ref, b_ref, g_ref, lnw_ref, lnb_ref, rk_ref,
                      s0_ref, tri_ref, o_ref, sout_ref, s_sc, *, chunk):
    l = pl.program_id(2)
    tl = r_ref.shape[1]
    c = chunk
    nc = tl // c
    hd = RW_HEAD
    n_heads = LANES // hd

    @pl.when(l == 0)
    def _():
        s_sc[...] = jnp.zeros_like(s_sc)
        for h in range(n_heads):
            s_sc[h * hd:(h + 1) * hd, h * hd:(h + 1) * hd] = s0_ref[0, h]

    ri = lax.broadcasted_iota(jnp.int32, (c, c), 0)
    ci = lax.broadcasted_iota(jnp.int32, (c, c), 1)
    tri_incl = (ri >= ci)[None]
    tri_strict = (ri > ci)[None]
    kr = lax.broadcasted_iota(jnp.int32, (LANES, LANES), 0)
    kc = lax.broadcasted_iota(jnp.int32, (LANES, LANES), 1)
    same_head = (kr // hd) == (kc // hd)
    head_of_lane = lax.broadcasted_iota(jnp.int32, (1, 1, LANES), 2) // hd
    to3 = lambda x: x.reshape(nc, c, LANES)

    lw = to3(lw_ref[0])
    lc = to3(_split_dot(tri_ref[...], lw_ref[0]))
    tot = lc[:, c - 1:c, :]
    r, k, v, kk, b = to3(r_ref[0]), to3(k_ref[0]), to3(v_ref[0]), to3(kk_ref[0]), to3(b_ref[0])
    e_neg = jnp.exp(-lc)
    e_end = jnp.exp(tot - lc)
    kk_t = kk * jnp.exp(lc - lw)
    r_t = r * jnp.exp(lc)
    b_t = b * e_neg
    k_t = k * e_neg
    wm = u0 = q_eff = o0 = None
    for h in range(n_heads):
        mine = head_of_lane == h
        lhs = jnp.concatenate([jnp.where(mine, kk_t, 0.0), jnp.where(mine, r_t, 0.0)], axis=1)
        g_b = _bmm_nt(lhs, b_t)
        g_k = _bmm_nt(lhs, k_t)
        a_ab = jnp.where(tri_strict, g_b[:, :c], 0.0)
        a_rb = jnp.where(tri_incl, g_b[:, c:], 0.0)
        a_ak = jnp.where(tri_strict, g_k[:, :c], 0.0)
        a_rk = jnp.where(tri_incl, g_k[:, c:], 0.0)
        tinv = _unit_lower_inverse(a_ab, c)
        wu = _bmm(tinv, jnp.concatenate([kk_t, _bmm(a_ak, v)], axis=2))
        wm_h, u0_h = wu[:, :, :LANES], -wu[:, :, LANES:]
        ru = _bmm(a_rb, jnp.concatenate([wm_h, u0_h], axis=2))
        qe_h = r_t - ru[:, :, :LANES]
        o0_h = _bmm(a_rk, v) + ru[:, :, LANES:]
        if h == 0:
            wm, u0, q_eff, o0 = wm_h, u0_h, qe_h, o0_h
        else:
            wm, u0 = jnp.where(mine, wm_h, wm), jnp.where(mine, u0_h, u0)
            q_eff, o0 = jnp.where(mine, qe_h, q_eff), jnp.where(mine, o0_h, o0)
    eye = (kr == kc).astype(F32)[None]
    m_mat = jnp.where(same_head[None], eye * jnp.exp(tot) - _bmm_tn(wm, b * e_end), 0.0)
    d_mat = jnp.where(same_head[None],
                      _bmm_tn(jnp.concatenate([u0, v], axis=1),
                              jnp.concatenate([b * e_end, k * e_end], axis=1)), 0.0)
    s = s_sc[...]
    states = []
    for i in range(nc):
        states.append(s)
        s = _dot(_bf(s), _bf(m_mat[i])) + d_mat[i]
    s_sc[...] = s
    o = (o0 + _bmm_nt(q_eff, jnp.stack(states))).reshape(tl, LANES)
    head_sum = _bf(jnp.where(same_head, 1.0, 0.0))
    mean = _dot_x2(o, head_sum) * (1.0 / hd)
    var = _dot_x2(jnp.square(o - mean), head_sum) * (1.0 / hd)
    o = (o - mean) * lax.rsqrt(var + GN_EPS) * lnw_ref[...] + lnb_ref[...]
    bonus = _dot_x2(r_ref[0] * k_ref[0] * rk_ref[...], head_sum) * v_ref[0]
    o_ref[0] = (o + bonus) * g_ref[0]

    @pl.when(l == pl.num_programs(2) - 1)
    def _():
        for h in range(n_heads):
            sout_ref[0, h] = s[h * hd:(h + 1) * hd, h * hd:(h + 1) * hd]


def rwkv_scan(r, lw, k, v, kk, b, g, ln_w, ln_b, r_k, s0, tl, chunk):
    bsz, seq, _ = r.shape
    assert seq % tl == 0 and tl % chunk == 0
    hp = RW_WIDTH // LANES
    per = LANES // RW_HEAD
    seq_spec = pl.BlockSpec((1, tl, LANES), lambda bi, hi, li: (bi, li, hi))
    par_spec = pl.BlockSpec((1, LANES), lambda bi, hi, li: (0, hi))
    st_spec = pl.BlockSpec((1, per, RW_HEAD, RW_HEAD), lambda bi, hi, li: (bi, hi, 0, 0))
    row = lambda a: a.reshape(1, RW_WIDTH)
    idx = np.arange(tl)
    tri = jnp.asarray(((idx[:, None] >= idx[None, :]) & (idx[:, None] // chunk == idx[None, :] // chunk)),
                      dtype=BF16)
    return pl.pallas_call(
        functools.partial(_rwkv_scan_kernel, chunk=chunk),
        grid=(bsz, hp, seq // tl),
        in_specs=[seq_spec] * 7 + [par_spec] * 3 + [st_spec, pl.BlockSpec((tl, tl), lambda bi, hi, li: (0, 0))],
        out_specs=[seq_spec, st_spec],
        out_shape=[jax.ShapeDtypeStruct((bsz, seq, RW_WIDTH), F32),
                   jax.ShapeDtypeStruct((bsz, RW_HEADS, RW_HEAD, RW_HEAD), F32)],
        scratch_shapes=[pltpu.VMEM((LANES, LANES), F32)],
        compiler_params=_cparams("parallel", "parallel", "arbitrary"),
    )(r, lw, k, v, kk, b, g, row(ln_w), row(ln_b), row(r_k), s0, tri)


def _moba_prep_kernel(p_ref, qg_ref, kg_ref, grp_ref, grpt_ref, q_o, k_o, kb_o, vt_o, ks_o):
    p = p_ref[...]
    q = p[:, 0:MB_WIDTH]
    k = p[:, MB_WIDTH:2 * MB_WIDTH]
    v = p[:, 2 * MB_WIDTH:3 * MB_WIDTH]

    def head_norm(x, gain_ref):
        ms = _dot_x2(x * x, grp_ref[...]) * (1.0 / MB_HEAD)
        inv = _dot_x2(lax.rsqrt(ms + RMS_EPS), grpt_ref[...])
        return x * inv * gain_ref[...]

    qn = head_norm(q, qg_ref)
    kn = head_norm(k, kg_ref)
    q_o[...] = qn
    k_o[...] = kn
    kb_o[...] = kn.astype(BF16)
    vt_o[...] = v.T.astype(BF16)
    ks_o[0] = jnp.sum(kn, axis=0, keepdims=True)


def moba_prep(p_mb, q_gain, k_gain, tm):
    t = p_mb.shape[0]
    assert t % tm == 0
    grp, grpt = _group_matrices(MB_WIDTH, MB_HEAD)
    tile = lambda i: (i, 0)
    full = lambda a: pl.BlockSpec(a.shape, lambda i: (0,) * a.ndim)
    qg = jnp.tile(q_gain, MB_HEADS).reshape(1, MB_WIDTH)
    kg = jnp.tile(k_gain, MB_HEADS).reshape(1, MB_WIDTH)
    return pl.pallas_call(
        _moba_prep_kernel,
        grid=(t // tm,),
        in_specs=[pl.BlockSpec((tm, 3 * MB_WIDTH), tile), full(qg), full(kg), full(grp), full(grpt)],
        out_specs=[pl.BlockSpec((tm, MB_WIDTH), tile)] * 3
                  + [pl.BlockSpec((MB_WIDTH, tm), lambda i: (0, i)),
                     pl.BlockSpec((1, 1, MB_WIDTH), lambda i: (i, 0, 0))],
        out_shape=[jax.ShapeDtypeStruct((t, MB_WIDTH), F32), jax.ShapeDtypeStruct((t, MB_WIDTH), F32),
                   jax.ShapeDtypeStruct((t, MB_WIDTH), BF16), jax.ShapeDtypeStruct((MB_WIDTH, t), BF16),
                   jax.ShapeDtypeStruct((t // tm, 1, MB_WIDTH), F32)],
        compiler_params=_cparams("parallel"),
    )(p_mb, qg, kg, grp, grpt)


def _top_blocks(gate, idx, axis):
    sel = jnp.zeros_like(gate)
    rem = gate
    for _ in range(MOBA_TOPK):
        mx = jnp.max(rem, axis=axis, keepdims=True)
        live = jnp.logical_and(rem == mx, rem > 0.5 * MASK_NEG)
        first = jnp.min(jnp.where(live, idx, 1e9), axis=axis, keepdims=True)
        pick = idx == first
        sel = jnp.where(pick, 1.0, sel)
        rem = jnp.where(pick, MASK_NEG, rem)
    return sel


def _moba_attn_kernel(q_ref, k_ref, vt_ref, ks_ref, o_ref, s_a, s_b, p_a, p_b):
    qi = pl.program_id(2)
    tq = q_ref.shape[1]
    blk = MOBA_BLOCK
    assert tq == 2 * blk
    nb = ks_ref.shape[1]
    n_heads = LANES // MB_HEAD
    ncol = n_heads * tq
    q = q_ref[0]
    lane = lax.broadcasted_iota(jnp.int32, (1, LANES), 1)
    bmean = ks_ref[0] * (1.0 / MOBA_BLOCK)
    blk_f = lax.broadcasted_iota(jnp.int32, (nb, tq), 0).astype(F32)
    second = lax.broadcasted_iota(jnp.int32, (1, tq), 1) >= blk
    own_f = (2 * qi).astype(F32) + jnp.where(second, 1.0, 0.0)
    q_cols, bias_cols, own_bias = [], [], []
    pad_rows = jnp.full((LANES - nb, tq), MASK_NEG, F32)
    first_blk = lax.broadcasted_iota(jnp.int32, (nb, tq), 0) == 2 * qi
    for h in range(n_heads):
        q_h = jnp.where((lane // MB_HEAD) == h, q, 0.0)
        gate_t = _dot_x3_nt(bmean, q_h)
        gate_t = jnp.where(blk_f < own_f, gate_t, MASK_NEG)
        bias = (1.0 - _top_blocks(gate_t, blk_f, 0)) * MASK_NEG
        own_bias.append(jnp.max(jnp.where(first_blk, bias, MASK_NEG), axis=0, keepdims=True))
        bias_cols.append(jnp.concatenate([bias, pad_rows], axis=0).T)
        q_cols.append(q_h * (LOG2E * MB_HEAD ** -0.5))
    q_all = _bf(jnp.concatenate(q_cols, axis=0))
    q_ext = jnp.concatenate([q_all, _bf(jnp.concatenate(bias_cols, axis=0))], axis=1)

    def ones_rows(n):
        return _bf(jnp.where(lax.broadcasted_iota(jnp.int32, (16, n), 0) == 0, 1.0, 0.0))

    def values(r0, p):
        out = []
        for h in range(n_heads):
            vt_h = jnp.concatenate([vt_ref[h * MB_HEAD:(h + 1) * MB_HEAD, pl.ds(r0, tq)], ones_rows(tq)], axis=0)
            out.append(_dot(vt_h, p[:, h * tq:(h + 1) * tq]))
        return jnp.concatenate(out, axis=1)

    row0 = pl.multiple_of(qi * tq, tq)
    ri = lax.broadcasted_iota(jnp.int32, (tq, ncol), 0)
    cq = lax.broadcasted_iota(jnp.int32, (tq, ncol), 1) % tq
    gated = jnp.logical_and(ri < blk, cq >= blk)
    s = _dot_nt(k_ref[0, pl.ds(row0, tq), :], q_all) + jnp.where(gated, jnp.concatenate(own_bias, axis=1), 0.0)
    s = jnp.where(ri <= cq, s, MASK_NEG)
    m = jnp.max(s, axis=0, keepdims=True)
    acc = values(row0, _bf(jnp.exp2(s - m)))

    lane_i = lax.broadcasted_iota(jnp.int32, (1, LANES), 1)

    def pair_rows(i):
        return pl.multiple_of(jnp.minimum(i, nb // 2 - 1) * tq, tq)

    def scores(i, s_ref):
        k2 = k_ref[0, pl.ds(pair_rows(i), tq), :]
        onehot = [jnp.broadcast_to(_bf(jnp.where(lane_i == jnp.where(i < qi, 2 * i + u, LANES - 1), 1.0, 0.0)),
                                   (blk, LANES)) for u in range(2)]
        k_ext = jnp.concatenate([k2, jnp.concatenate(onehot, axis=0)], axis=1)
        s_ref[...] = _dot_nt(k_ext, q_ext)

    def softmax(s_ref, p_ref, m):
        s = s_ref[...]
        m_new = jnp.maximum(m, jnp.max(s, axis=0, keepdims=True))
        p_ref[...] = _bf(jnp.exp2(s - m_new))
        return m_new, jnp.exp2(m - m_new)

    scores(0, s_a)
    p_b[...] = jnp.zeros_like(p_b)

    def body(t, carry):
        m, acc = carry
        scores(2 * t + 1, s_b)
        pv = values(pair_rows(jnp.maximum(2 * t - 1, 0)), p_b[...])
        m, alpha = softmax(s_a, p_a, m)
        acc = (acc + pv) * alpha
        scores(2 * t + 2, s_a)
        pv = values(pair_rows(2 * t), p_a[...])
        m, alpha = softmax(s_b, p_b, m)
        acc = (acc + pv) * alpha
        return m, acc

    trips = (qi + 1) // 2
    m, acc = lax.fori_loop(0, trips, body, (m, acc))
    acc = acc + values(pair_rows(jnp.maximum(2 * trips - 1, 0)), p_b[...])
    o = acc[:MB_HEAD] / acc[MB_HEAD:MB_HEAD + 1]
    o_ref[0] = jnp.concatenate([o[:, h * tq:(h + 1) * tq] for h in range(n_heads)], axis=0).T


def moba_attn(qn, kb, vt, ksum):
    bsz, seq, _ = qn.shape
    tq = 2 * MOBA_BLOCK
    nb = seq // MOBA_BLOCK
    assert seq % tq == 0 and nb <= LANES
    hp = MB_WIDTH // LANES
    ncol = (LANES // MB_HEAD) * tq
    return pl.pallas_call(
        _moba_attn_kernel,
        grid=(bsz, hp, seq // tq),
        in_specs=[pl.BlockSpec((1, tq, LANES), lambda b, h, i: (b, i, h)),
                  pl.BlockSpec((1, seq, LANES), lambda b, h, i: (b, 0, h)),
                  pl.BlockSpec((LANES, seq), lambda b, h, i: (h, b)),
                  pl.BlockSpec((1, nb, LANES), lambda b, h, i: (b, 0, h))],
        out_specs=pl.BlockSpec((1, tq, LANES), lambda b, h, i: (b, i, h)),
        out_shape=jax.ShapeDtypeStruct((bsz, seq, MB_WIDTH), F32),
        scratch_shapes=[pltpu.VMEM((tq, ncol), F32), pltpu.VMEM((tq, ncol), F32),
                        pltpu.VMEM((tq, ncol), BF16), pltpu.VMEM((tq, ncol), BF16)],
        compiler_params=_cparams("parallel", "parallel", "arbitrary"),
    )(qn, kb, vt, ksum)


def _moba_scores_kernel(pt_ref, q_ref, *refs, n_pg):
    k_refs = refs[:n_pg]
    s_o = refs[n_pg]
    q = q_ref[0]
    for j in range(n_pg):
        s_o[0, j] = jnp.sum(k_refs[j][0, 0] * q, axis=1)


def _moba_select_kernel(s_ref, q_ref, kn_ref, p_o, w_o, id_o, *, ppb):
    scale = MB_HEAD ** -0.5
    s = s_ref[0]
    n_lp = s.shape[0]
    nblk = n_lp // ppb
    page_sum = jnp.sum(s, axis=-1, keepdims=True)
    gate = jnp.sum(page_sum.reshape(nblk, ppb, MB_HEADS, 1), axis=1) * (1.0 / MOBA_BLOCK)
    idx = lax.broadcasted_iota(jnp.int32, gate.shape, 0).astype(F32)
    lane = lax.broadcasted_iota(jnp.int32, (MB_HEADS, LANES), 1)
    sel = jnp.zeros_like(gate)
    rem = gate
    ids = jnp.zeros((MB_HEADS, LANES), F32)
    for t in range(MOBA_TOPK):
        mx = jnp.max(rem, axis=0, keepdims=True)
        first = jnp.min(jnp.where(rem == mx, idx, 1e9), axis=0, keepdims=True)
        pick = idx == first
        sel = jnp.where(pick, 1.0, sel)
        rem = jnp.where(pick, MASK_NEG, rem)
        for u in range(ppb):
            ids = jnp.where(lane == t * ppb + u, first[0] * ppb + u, ids)
    id_o[0] = ids.astype(jnp.int32)
    sel_pg = jnp.broadcast_to(sel[:, None], (nblk, ppb, MB_HEADS, 1)).reshape(n_lp, MB_HEADS, 1) > 0.5
    s_own = jnp.sum(kn_ref[0] * q_ref[0], axis=-1, keepdims=True) * scale
    sm = jnp.where(sel_pg, s * scale, MASK_NEG)
    m_all = jnp.maximum(jnp.max(jnp.max(sm, axis=0), axis=-1, keepdims=True), s_own)
    p = jnp.where(sel_pg, jnp.exp(sm - m_all[None]), 0.0)
    w_own = jnp.exp(s_own - m_all)
    inv = 1.0 / (jnp.sum(jnp.sum(p, axis=0), axis=-1, keepdims=True) + w_own)
    p_o[0] = p * inv[None]
    w_o[0] = jnp.broadcast_to(w_own * inv, (MB_HEADS, LANES))


def _moba_values_kernel(pt_ref, id_ref, p_ref, w_ref, vn_ref, *refs, n_sel, hps):
    v_refs = refs[:hps * n_sel]
    o_ref = refs[hps * n_sel]
    b = pl.program_id(0)
    for hh in range(hps):
        h = pl.program_id(1) * hps + hh
        acc = w_ref[0, pl.ds(h, 1), 0:1] * vn_ref[0, hh]
        for j in range(n_sel):
            page = id_ref[(b * MB_HEADS + h) * n_sel + j]
            prob = p_ref[0, page, pl.ds(h, 1), :]
            acc = acc + _dot_nt(prob, v_refs[hh * n_sel + j][0, 0, 0], HI)
        o_ref[0, hh] = acc


def moba_decode(qn, kn, vn, k_cache_t, v_cache_t, page_table, layer):
    bsz = qn.shape[0]
    n_lp = page_table.shape[1]
    ppb = MOBA_BLOCK // PAGE_SIZE
    assert n_lp % ppb == 0 and n_lp // ppb >= MOBA_TOPK
    n_pg = 16
    assert n_lp % n_pg == 0
    n_sel = MOBA_TOPK * ppb
    hps = MB_HEADS
    heads = lambda a: a.reshape(bsz, MB_HEADS, MB_HEAD)
    page_blk = (1, 1, MB_HEADS, MB_HEAD, PAGE_SIZE)

    scores = pl.pallas_call(
        functools.partial(_moba_scores_kernel, n_pg=n_pg),
        grid_spec=pltpu.PrefetchScalarGridSpec(
            num_scalar_prefetch=1,
            grid=(bsz, n_lp // n_pg),
            in_specs=[pl.BlockSpec((1, MB_HEADS, MB_HEAD, 1), lambda b, g, pt: (b, 0, 0, 0))]
                     + [pl.BlockSpec(page_blk, functools.partial(
                         lambda b, g, pt, j: (layer, pt[b, g * n_pg + j], 0, 0, 0), j=j)) for j in range(n_pg)],
            out_specs=pl.BlockSpec((1, n_pg, MB_HEADS, PAGE_SIZE), lambda b, g, pt: (b, g, 0, 0))),
        out_shape=jax.ShapeDtypeStruct((bsz, n_lp, MB_HEADS, PAGE_SIZE), F32),
        compiler_params=_cparams("parallel", "arbitrary"),
    )(page_table, qn.reshape(bsz, MB_HEADS, MB_HEAD, 1), *([k_cache_t] * n_pg))

    vec = pl.BlockSpec((1, MB_HEADS, MB_HEAD), lambda b: (b, 0, 0))
    probs, w_own, ids = pl.pallas_call(
        functools.partial(_moba_select_kernel, ppb=ppb),
        grid=(bsz,),
        in_specs=[pl.BlockSpec((1, n_lp, MB_HEADS, PAGE_SIZE), lambda b: (b, 0, 0, 0)), vec, vec],
        out_specs=[pl.BlockSpec((1, n_lp, MB_HEADS, PAGE_SIZE), lambda b: (b, 0, 0, 0)),
                   pl.BlockSpec((1, MB_HEADS, LANES), lambda b: (b, 0, 0)),
                   pl.BlockSpec((1, MB_HEADS, LANES), lambda b: (b, 0, 0))],
        out_shape=[jax.ShapeDtypeStruct((bsz, n_lp, MB_HEADS, PAGE_SIZE), F32),
                   jax.ShapeDtypeStruct((bsz, MB_HEADS, LANES), F32),
                   jax.ShapeDtypeStruct((bsz, MB_HEADS, LANES), jnp.int32)],
        compiler_params=_cparams("parallel"),
    )(scores, heads(qn), heads(kn))

    sel_ids = ids[:, :, :n_sel].reshape(-1)
    out = pl.pallas_call(
        functools.partial(_moba_values_kernel, n_sel=n_sel, hps=hps),
        grid_spec=pltpu.PrefetchScalarGridSpec(
            num_scalar_prefetch=2,
            grid=(bsz, MB_HEADS // hps),
            in_specs=[pl.BlockSpec((1, n_lp, MB_HEADS, PAGE_SIZE), lambda b, g, pt, sid: (b, 0, 0, 0)),
                      pl.BlockSpec((1, MB_HEADS, LANES), lambda b, g, pt, sid: (b, 0, 0)),
                      pl.BlockSpec((1, hps, 1, MB_HEAD), lambda b, g, pt, sid: (b, g, 0, 0))]
                     + [pl.BlockSpec((1, 1, 1, MB_HEAD, PAGE_SIZE), functools.partial(
                         lambda b, g, pt, sid, hh, j: (
                             layer, pt[b, sid[(b * MB_HEADS + g * hps + hh) * n_sel + j]], g * hps + hh, 0, 0),
                         hh=hh, j=j)) for hh in range(hps) for j in range(n_sel)],
            out_specs=pl.BlockSpec((1, hps, 1, MB_HEAD), lambda b, g, pt, sid: (b, g, 0, 0))),
        out_shape=jax.ShapeDtypeStruct((bsz, MB_HEADS, 1, MB_HEAD), F32),
        compiler_params=_cparams("parallel", "arbitrary"),
    )(page_table, sel_ids, probs, w_own, vn.reshape(bsz, MB_HEADS, 1, MB_HEAD), *([v_cache_t] * (hps * n_sel)))
    return out.reshape(bsz, MB_WIDTH)


def _hgrn_constants(c):
    rows = np.arange(c)
    dmats, ups, los, masks = [], [], [], []
    z = c
    while z >= 2:
        seg = rows // z
        mid = seg * z + z // 2
        up = rows >= mid
        d = np.zeros((c, c), np.float32)
        for i in rows:
            if up[i]:
                d[i, mid[i]:i + 1] = 1.0
            else:
                d[i, i + 1:mid[i]] = 1.0
        dmats.append(d)
        ups.append(np.repeat(up[:, None], HG_D, 1).astype(np.float32))
        los.append(np.repeat(~up[:, None], HG_D, 1).astype(np.float32))
        masks.append(((seg[:, None] == seg[None, :]) & up[:, None] & (~up)[None, :]).astype(np.float32))
        z //= 2
    tri = np.tril(np.ones((c, c), np.float32))
    n_wide = sum(1 for lev in range(len(dmats)) if (c >> lev) >= HG_WIDE_SEG)
    return (jnp.asarray(np.concatenate(dmats[n_wide:] + [tri], 0), dtype=BF16), jnp.asarray(np.stack(ups)),
            jnp.asarray(np.stack(los)), jnp.asarray(np.stack(masks)))


def _hgrn_kernel(pq_ref, pf_ref, pi_ref, pg_ref, lbl_ref, gain_ref, s0_ref, dd_ref, up_ref, lo_ref,
                 mk_ref, o_ref, sout_ref, st_sc, *, chunk, layer):
    l = pl.program_id(2)
    tl = pq_ref.shape[1]
    c = chunk
    nc = tl // c
    nlev = up_ref.shape[0]
    er = lax.broadcasted_iota(jnp.int32, (HG_D, HG_D), 0)
    ec = lax.broadcasted_iota(jnp.int32, (HG_D, HG_D), 1)
    eye_d = (er == ec).astype(F32)

    @pl.when(l == 0)
    def _():
        st_sc[...] = _dot_nt(eye_d, s0_ref[0, 0], HI)

    logits = lbl_ref[...]
    ex = jnp.exp(logits - jnp.max(logits, axis=0, keepdims=True))
    prob = ex / jnp.sum(ex, axis=0, keepdims=True)
    lb = jnp.sum(prob[0:layer + 1], axis=0, keepdims=True) - prob[0:1]

    qr = pq_ref[0]
    q = qr * _sigmoid(qr)
    forget = lb + (1.0 - lb) * _sigmoid(pf_ref[0])
    logf = jnp.log(jnp.maximum(forget, 1e-30))
    kg = 1.0 - forget
    v = pi_ref[0]

    ri = lax.broadcasted_iota(jnp.int32, (c, c), 0)
    ci = lax.broadcasted_iota(jnp.int32, (c, c), 1)
    diag = ri == ci
    rows = [slice(i * c, (i + 1) * c) for i in range(nc)]
    x_all = _split_dot(dd_ref[...], jnp.concatenate([logf[rs] for rs in rows], axis=1))
    n_wide = sum(1 for lev in range(nlev) if (c >> lev) >= HG_WIDE_SEG)
    n_nar = nlev - n_wide
    cum = jnp.concatenate([x_all[n_nar * c:(n_nar + 1) * c, i * HG_D:(i + 1) * HG_D] for i in range(nc)], axis=0)
    e_wide = []
    for lev in range(n_wide):
        z = c >> lev
        cum_z = cum.reshape(tl // z, z, HG_D)
        e_wide.append(jnp.exp(-jnp.abs(cum_z - cum_z[:, z // 2 - 1:z // 2, :])).reshape(tl, HG_D))
    cum_c = cum.reshape(nc, c, HG_D)
    e_in_all = jnp.exp(cum)
    e_out_all = jnp.exp(cum_c[:, c - 1:c, :] - cum_c).reshape(tl, HG_D)
    e_nar = jnp.exp(x_all[:n_nar * c])
    intra, kv, qe, dec = [], [], [], []
    for i, rs in enumerate(rows):
        qc, kc, vc = q[rs], kg[rs], v[rs]
        a = jnp.where(diag, jnp.sum(qc * kc, axis=1, keepdims=True), 0.0)
        for lev in range(nlev):
            if lev < n_wide:
                el = e_wide[lev][rs]
            else:
                el = e_nar[(lev - n_wide) * c:(lev - n_wide + 1) * c, i * HG_D:(i + 1) * HG_D]
            a = a + mk_ref[lev] * _dot_nt(_bf(qc * el * up_ref[lev]), _bf(kc * el * lo_ref[lev]))
        e_in = e_in_all[rs]
        e_out = e_out_all[rs]
        intra.append(_dot(_bf(a), _bf(vc)))
        kv.append(_dot_tn(_bf(vc), _bf(kc * e_out)))
        qe.append(_bf(qc * e_in))
        dec.append(e_in[c - 1:c, :])
    st = st_sc[...]
    outs = []
    for i in range(nc):
        outs.append(_dot_nt(qe[i], _bf(st)) + intra[i])
        st = st * dec[i] + kv[i]
    st_sc[...] = st
    o = jnp.concatenate(outs, axis=0) if nc > 1 else outs[0]
    ms = jnp.mean(o * o, axis=-1, keepdims=True)
    gr = pg_ref[0]
    o_ref[0] = o * lax.rsqrt(ms + RMS_EPS) * gain_ref[...] * (gr * _sigmoid(gr))

    @pl.when(l == pl.num_programs(2) - 1)
    def _():
        sout_ref[0, 0] = _dot_nt(eye_d, st, HI)


def hgrn_scan(p, lb_logits, out_gain, s0, layer, tl, chunk):
    bsz, seq, _ = p.shape
    assert seq % tl == 0 and tl % chunk == 0
    dd, up, lo, mk = _hgrn_constants(chunk)
    col = lambda off: pl.BlockSpec((1, tl, HG_D), lambda b, h, li: (b, li, off + h))
    full = lambda a: pl.BlockSpec(a.shape, lambda b, h, li: (0,) * a.ndim)
    st_spec = pl.BlockSpec((1, 1, HG_D, HG_D), lambda b, h, li: (b, h, 0, 0))
    return pl.pallas_call(
        functools.partial(_hgrn_kernel, chunk=chunk, layer=layer),
        grid=(bsz, HG_HEADS, seq // tl),
        in_specs=[col(0), col(HG_HEADS), col(2 * HG_HEADS), col(3 * HG_HEADS),
                  pl.BlockSpec((lb_logits.shape[0], HG_D), lambda b, h, li: (0, h)),
                  pl.BlockSpec((1, HG_D), lambda b, h, li: (0, 0)),
                  st_spec, full(dd), full(up), full(lo), full(mk)],
        out_specs=[pl.BlockSpec((1, tl, HG_D), lambda b, h, li: (b, li, h)), st_spec],
        out_shape=[jax.ShapeDtypeStruct((bsz, seq, D_MODEL), F32),
                   jax.ShapeDtypeStruct((bsz, HG_HEADS, HG_D, HG_D), F32)],
        scratch_shapes=[pltpu.VMEM((HG_D, HG_D), F32)],
        compiler_params=_cparams("parallel", "parallel", "arbitrary"),
    )(p, p, p, p, lb_logits, out_gain.reshape(1, HG_D), s0, dd, up, lo, mk)


def _row_to_col(row, eye):
    return jnp.sum(jnp.where(eye, row, 0.0), axis=1, keepdims=True)


def _col_to_row(col, eye):
    return jnp.sum(jnp.where(eye, col, 0.0), axis=0, keepdims=True)


def _rwkv_step_kernel(r_ref, lw_ref, k_ref, v_ref, kk_ref, b_ref, g_ref, lnw_ref, lnb_ref, rk_ref,
                      s0_ref, o_ref, sout_ref):
    hd = RW_HEAD
    ri = lax.broadcasted_iota(jnp.int32, (hd, hd), 0)
    ci = lax.broadcasted_iota(jnp.int32, (hd, hd), 1)
    eye = ri == ci
    for h in range(RW_HEADS):
        sl = slice(h * hd, (h + 1) * hd)
        row = lambda ref: ref[0][:, sl]
        r, k, v, kk, b = row(r_ref), row(k_ref), row(v_ref), row(kk_ref), row(b_ref)
        s = s0_ref[0, h]
        s_kk = jnp.sum(s * kk, axis=1, keepdims=True)
        s = s * jnp.exp(row(lw_ref)) - s_kk * b + _row_to_col(v, eye) * k
        sout_ref[0, h] = s
        o = _col_to_row(jnp.sum(s * r, axis=1, keepdims=True), eye)
        mean = jnp.mean(o, axis=-1, keepdims=True)
        var = jnp.mean(jnp.square(o - mean), axis=-1, keepdims=True)
        o = (o - mean) * lax.rsqrt(var + GN_EPS) * lnw_ref[:, sl] + lnb_ref[:, sl]
        bonus = jnp.sum(r * k * rk_ref[:, sl], axis=-1, keepdims=True) * v
        o_ref[0, :, sl] = (o + bonus) * row(g_ref)


def rwkv_step(r, lw, k, v, kk, b, g, ln_w, ln_b, r_k, s0):
    bsz = r.shape[0]
    seq_spec = pl.BlockSpec((1, 1, RW_WIDTH), lambda i: (i, 0, 0))
    par_spec = pl.BlockSpec((1, RW_WIDTH), lambda i: (0, 0))
    st_spec = pl.BlockSpec((1, RW_HEADS, RW_HEAD, RW_HEAD), lambda i: (i, 0, 0, 0))
    row = lambda a: a.reshape(1, RW_WIDTH)
    tok = lambda a: a.reshape(bsz, 1, RW_WIDTH)
    o, s = pl.pallas_call(
        _rwkv_step_kernel,
        grid=(bsz,),
        in_specs=[seq_spec] * 7 + [par_spec] * 3 + [st_spec],
        out_specs=[seq_spec, st_spec],
        out_shape=[jax.ShapeDtypeStruct((bsz, 1, RW_WIDTH), F32),
                   jax.ShapeDtypeStruct((bsz, RW_HEADS, RW_HEAD, RW_HEAD), F32)],
        compiler_params=_cparams("parallel"),
    )(*[tok(a) for a in (r, lw, k, v, kk, b, g)], row(ln_w), row(ln_b), row(r_k), s0)
    return o.reshape(bsz, RW_WIDTH), s


def _hgrn_step_kernel(p_ref, lbl_ref, gain_ref, s0_ref, o_ref, sout_ref, *, layer):
    d = HG_D
    ri = lax.broadcasted_iota(jnp.int32, (d, d), 0)
    ci = lax.broadcasted_iota(jnp.int32, (d, d), 1)
    eye = ri == ci
    logits = lbl_ref[...]
    ex = jnp.exp(logits - jnp.max(logits, axis=0, keepdims=True))
    prob = ex / jnp.sum(ex, axis=0, keepdims=True)
    lb_all = jnp.sum(prob[0:layer + 1], axis=0, keepdims=True) - prob[0:1]
    width = HG_HEADS * d
    for h in range(HG_HEADS):
        col = lambda part: p_ref[0][:, part * width + h * d:part * width + (h + 1) * d]
        qr, fr, v, gr = col(0), col(1), col(2), col(3)
        lb = lb_all[:, h * d:(h + 1) * d]
        q = qr * _sigmoid(qr)
        forget = lb + (1.0 - lb) * _sigmoid(fr)
        decay = jnp.maximum(forget, 1e-30)
        s = _row_to_col(decay, eye) * s0_ref[0, h] + _row_to_col(1.0 - forget, eye) * v
        sout_ref[0, h] = s
        o = jnp.sum(_row_to_col(q, eye) * s, axis=0, keepdims=True)
        ms = jnp.mean(o * o, axis=-1, keepdims=True)
        o_ref[0, :, h * d:(h + 1) * d] = o * lax.rsqrt(ms + RMS_EPS) * gain_ref[...] * (gr * _sigmoid(gr))


def hgrn_step(p, lb_logits, out_gain, s0, layer):
    bsz = p.shape[0]
    st_spec = pl.BlockSpec((1, HG_HEADS, HG_D, HG_D), lambda i: (i, 0, 0, 0))
    o, s = pl.pallas_call(
        functools.partial(_hgrn_step_kernel, layer=layer),
        grid=(bsz,),
        in_specs=[pl.BlockSpec((1, 1, p.shape[1]), lambda i: (i, 0, 0)),
                  pl.BlockSpec(lb_logits.shape, lambda i: (0, 0)),
                  pl.BlockSpec((1, HG_D), lambda i: (0, 0)), st_spec],
        out_specs=[pl.BlockSpec((1, 1, D_MODEL), lambda i: (i, 0, 0)), st_spec],
        out_shape=[jax.ShapeDtypeStruct((bsz, 1, D_MODEL), F32),
                   jax.ShapeDtypeStruct((bsz, HG_HEADS, HG_D, HG_D), F32)],
        compiler_params=_cparams("parallel"),
    )(p.reshape(bsz, 1, -1), lb_logits, out_gain.reshape(1, HG_D), s0)
    return o.reshape(bsz, D_MODEL), s


def _even_layer(xp, xs, il, vf_p, vf_s, w, cache_k, cache_v, page_table, state_rwkv, state_shift):
    nb_p, seq, d = xp.shape
    nb_s = xs.shape[0]
    tp = nb_p * seq
    rw_params = (w['rw_mu'][il], w['rw_w0'][il], w['w2p'][il], w['rw_a0'][il], w['a2p'][il],
                 w['rw_g2'][il], w['rw_k_k'][il], w['rw_k_a'][il])
    has_vres = il > 0
    vres_w = (w['rw_v0'][il - 1], w['rw_v1'][il - 1], w['rw_v2'][il - 1]) if has_vres else None
    ln = (w['rw_ln_w'][il], w['rw_ln_b'][il], w['rw_r_k'][il].reshape(-1))

    p_rw, p_mb = norm_matmul(xp.reshape(tp, d), w['norm_mix'][2 * il], w['w_in_even'][il],
                             (RW_IN, 3 * MB_WIDTH), 512)
    vres = (vf_p,) + vres_w if has_vres else None
    prep = rwkv_prep(p_rw, None, rw_params, vres, True, seq, 512)
    if not has_vres:
        vf_p = prep[7]
    seq3 = lambda a: a.reshape(nb_p, seq, -1)
    o_rw, wkv_p = rwkv_scan(*[seq3(a) for a in prep[:7]], *ln,
                            jnp.zeros((nb_p, RW_HEADS, RW_HEAD, RW_HEAD), F32), 1024, RW_CHUNK)
    qn, kn, kb, vt, ksum = moba_prep(p_mb, w['mb_q_norm'][il], w['mb_k_norm'][il], MOBA_BLOCK)
    o_mb = moba_attn(seq3(qn), seq3(kb), vt, ksum.reshape(nb_p, seq // MOBA_BLOCK, MB_WIDTH))
    lyr = 2 * il
    yp = mix_ffn(xp.reshape(tp, d), [o_rw.reshape(tp, RW_WIDTH), o_mb.reshape(tp, MB_WIDTH)],
                 [w['w_out_even'][il][:RW_WIDTH], w['w_out_even'][il][RW_WIDTH:]],
                 w['norm_ffn'][lyr], w['ffn_w_gate'][lyr], w['ffn_w_up'][lyr], w['ffn_w_down'][lyr],
                 512, D_FF // 2).reshape(nb_p, seq, d)
    mk_p = kn.reshape(nb_p, seq, MB_HEADS, MB_HEAD)
    mv_p = p_mb[:, 2 * MB_WIDTH:].reshape(nb_p, seq, MB_HEADS, MB_HEAD)
    sh_p = p_rw.reshape(nb_p, seq, RW_IN)[:, -1]

    ps_rw, ps_mb = norm_matmul(xs.reshape(nb_s, d), w['norm_mix'][2 * il], w['w_in_even'][il],
                               (RW_IN, 3 * MB_WIDTH), nb_s)
    vres = (vf_s,) + vres_w if has_vres else None
    prep = rwkv_prep(ps_rw, state_shift[il], rw_params, vres, False, 1, nb_s)
    if not has_vres:
        vf_s = prep[7]
    os_rw, wkv_s = rwkv_step(*prep[:7], *ln, state_rwkv[il])
    qs, ks, _, _, _ = moba_prep(ps_mb, w['mb_q_norm'][il], w['mb_k_norm'][il], nb_s)
    vs = ps_mb[:, 2 * MB_WIDTH:]
    os_mb = moba_decode(qs, ks, vs, cache_k, cache_v, page_table, il)
    ys = mix_ffn(xs.reshape(nb_s, d), [os_rw, os_mb],
                 [w['w_out_even'][il][:RW_WIDTH], w['w_out_even'][il][RW_WIDTH:]],
                 w['norm_ffn'][lyr], w['ffn_w_gate'][lyr], w['ffn_w_up'][lyr], w['ffn_w_down'][lyr],
                 nb_s, D_FF // 2).reshape(nb_s, 1, d)
    mk_s = ks.reshape(nb_s, 1, MB_HEADS, MB_HEAD)
    mv_s = vs.reshape(nb_s, 1, MB_HEADS, MB_HEAD)
    return yp, ys, vf_p, vf_s, (mk_p, mv_p, mk_s, mv_s, wkv_p, wkv_s, sh_p, ps_rw)


def _odd_layer(xp, xs, il, w, state_hgrn):
    nb_p, seq, d = xp.shape
    nb_s = xs.shape[0]
    tp = nb_p * seq
    lyr = 2 * il + 1
    (pp,) = norm_matmul(xp.reshape(tp, d), w['norm_mix'][lyr], w['w_in_odd'][il], (4 * d,), 512)
    o_p, hg_p = hgrn_scan(pp.reshape(nb_p, seq, 4 * d), w['hg_lb_logits'], w['hg_out_norm'][il],
                          jnp.zeros((nb_p, HG_HEADS, HG_D, HG_D), F32), il, 2048, HG_CHUNK)
    yp = mix_ffn(xp.reshape(tp, d), [o_p.reshape(tp, d)], [w['w_out_odd'][il]],
                 w['norm_ffn'][lyr], w['ffn_w_gate'][lyr], w['ffn_w_up'][lyr], w['ffn_w_down'][lyr],
                 512, D_FF // 2).reshape(nb_p, seq, d)
    (ps,) = norm_matmul(xs.reshape(nb_s, d), w['norm_mix'][lyr], w['w_in_odd'][il], (4 * d,), nb_s)
    o_s, hg_s = hgrn_step(ps, w['hg_lb_logits'], w['hg_out_norm'][il], state_hgrn[il], il)
    ys = mix_ffn(xs.reshape(nb_s, d), [o_s], [w['w_out_odd'][il]],
                 w['norm_ffn'][lyr], w['ffn_w_gate'][lyr], w['ffn_w_up'][lyr], w['ffn_w_down'][lyr],
                 nb_s, D_FF // 2).reshape(nb_s, 1, d)
    return yp, ys, hg_p, hg_s


def kernel(x_prompt, x_sample, cache_moba_k, cache_moba_v, page_table, state_rwkv, state_rwkv_shift, state_hgrn, norm_mix, norm_ffn, w_in_even, w_out_even, rw_mu, rw_w0, rw_w2, rw_a0, rw_a2, rw_g2, rw_k_k, rw_k_a, rw_r_k, rw_ln_w, rw_ln_b, rw_v0, rw_v1, rw_v2, mb_q_norm, mb_k_norm, w_in_odd, w_out_odd, hg_lb_logits, hg_out_norm, ffn_w_gate, ffn_w_up, ffn_w_down):
    depth = norm_mix.shape[0]
    zeros_lora = jnp.zeros_like(rw_w2)
    w = dict(
        norm_mix=norm_mix, norm_ffn=norm_ffn,
        w_in_even=w_in_even.astype(BF16), w_out_even=w_out_even.astype(BF16),
        w_in_odd=w_in_odd.astype(BF16), w_out_odd=w_out_odd.astype(BF16),
        ffn_w_gate=ffn_w_gate.astype(BF16), ffn_w_up=ffn_w_up.astype(BF16),
        ffn_w_down=ffn_w_down.astype(BF16),
        rw_mu=rw_mu, rw_w0=rw_w0, rw_a0=rw_a0, rw_g2=rw_g2, rw_k_k=rw_k_k, rw_k_a=rw_k_a,
        rw_r_k=rw_r_k, rw_ln_w=rw_ln_w, rw_ln_b=rw_ln_b, rw_v0=rw_v0, rw_v1=rw_v1, rw_v2=rw_v2,
        w2p=jnp.concatenate([rw_w2, zeros_lora], axis=1), a2p=jnp.concatenate([zeros_lora, rw_a2], axis=1),
        mb_q_norm=mb_q_norm, mb_k_norm=mb_k_norm, hg_lb_logits=hg_lb_logits, hg_out_norm=hg_out_norm)
    page_table = page_table.astype(jnp.int32)
    cache_k = jnp.transpose(cache_moba_k, (0, 1, 3, 4, 2))
    cache_v = jnp.transpose(cache_moba_v, (0, 1, 3, 4, 2))

    xp, xs = x_prompt, x_sample
    vf_p = vf_s = None
    even_out, hg_out = [], []
    for layer in range(depth):
        il = layer // 2
        if layer % 2 == 0:
            xp, xs, vf_p, vf_s, outs = _even_layer(xp, xs, il, vf_p, vf_s, w, cache_k, cache_v,
                                                   page_table, state_rwkv, state_rwkv_shift)
            even_out.append(outs)
        else:
            xp, xs, hg_p, hg_s = _odd_layer(xp, xs, il, w, state_hgrn)
            hg_out.append((hg_p, hg_s))
    stack = lambda i: jnp.stack([o[i] for o in even_out])
    return (xp, xs, stack(0), stack(1), stack(2), stack(3), stack(4), stack(5), stack(6), stack(7),
            jnp.stack([o[0] for o in hg_out]), jnp.stack([o[1] for o in hg_out]))
```

```python
import functools

import numpy as np
import jax
import jax.numpy as jnp
from jax import lax
from jax.experimental import pallas as pl
from jax.experimental.pallas import tpu as pltpu

F32 = jnp.float32
BF16 = jnp.bfloat16
HI = lax.Precision.HIGHEST

D_MODEL = 1024
PAGE_SIZE = 128
RW_HEAD = 64
RW_WIDTH = 512
RW_HEADS = 8
RW_IN = 1792
MB_HEAD = 64
MB_WIDTH = 512
MB_HEADS = 8
MOBA_BLOCK = 256
MOBA_TOPK = 3
HG_HEADS = 8
HG_D = 128
D_FF = 2816
RMS_EPS = 1e-6
GN_EPS = 64e-5
MASK_NEG = -1e30
LOG2E = 1.4426950408889634

LANES = 128
VMEM_LIMIT = 56 * 1024 * 1024
RW_CHUNK = 64
HG_CHUNK = 64
HG_WIDE_SEG = 16


def _cparams(*sem):
    return pltpu.CompilerParams(dimension_semantics=sem, vmem_limit_bytes=VMEM_LIMIT)


def _bf(x):
    return x.astype(BF16)


def _dot(a, b, precision=None):
    return jnp.dot(a, b, precision=precision, preferred_element_type=F32)


def _dot_nt(a, b, precision=None):
    return lax.dot_general(a, b, (((1,), (1,)), ((), ())), precision=precision,
                           preferred_element_type=F32)


def _dot_tn(a, b, precision=None):
    return lax.dot_general(a, b, (((0,), (0,)), ((), ())), precision=precision,
                           preferred_element_type=F32)


def _bmm(a, b):
    return lax.dot_general(_bf(a), _bf(b), (((2,), (1,)), ((0,), (0,))), preferred_element_type=F32)


def _bmm_nt(a, b):
    return lax.dot_general(_bf(a), _bf(b), (((2,), (2,)), ((0,), (0,))), preferred_element_type=F32)


def _bmm_tn(a, b):
    return lax.dot_general(_bf(a), _bf(b), (((1,), (1,)), ((0,), (0,))), preferred_element_type=F32)


def _split_dot(a_bf16, b):
    n = b.shape[1]
    hi = b.astype(BF16)
    lo = (b - hi.astype(F32)).astype(BF16)
    out = _dot(a_bf16, jnp.concatenate([hi, lo], axis=1))
    return out[:, :n] + out[:, n:]


def _hi_lo(x):
    hi = x.astype(BF16)
    return hi, (x - hi.astype(F32)).astype(BF16)


def _dot_x3(a, b):
    a_hi, a_lo = _hi_lo(a)
    b_hi, b_lo = _hi_lo(b)
    return _dot(a_hi, b_hi) + (_dot(a_hi, b_lo) + _dot(a_lo, b_hi))


def _dot_x3_nt(a, b):
    a_hi, a_lo = _hi_lo(a)
    b_hi, b_lo = _hi_lo(b)
    return _dot_nt(a_hi, b_hi) + (_dot_nt(a_hi, b_lo) + _dot_nt(a_lo, b_hi))


def _dot_x2(a, g_bf16):
    a_hi, a_lo = _hi_lo(a)
    return _dot(a_hi, g_bf16) + _dot(a_lo, g_bf16)


def _sigmoid(x):
    return 1.0 / (1.0 + jnp.exp(-x))


def _group_matrices(width, group):
    g = np.zeros((width, LANES), np.float32)
    g[np.arange(width), np.arange(width) // group] = 1.0
    return jnp.asarray(g, dtype=BF16), jnp.asarray(g.T.copy(), dtype=BF16)


def _norm_matmul_kernel(x_ref, gain_ref, w_ref, *out_refs, widths, tn):
    x = x_ref[...]
    ms = jnp.mean(x * x, axis=-1, keepdims=True)
    h = (x * lax.rsqrt(ms + RMS_EPS) * gain_ref[...]).astype(BF16)
    col = 0
    for o_ref, width in zip(out_refs, widths):
        for c0 in range(0, width, tn):
            sz = min(tn, width - c0)
            o_ref[:, c0:c0 + sz] = _dot(h, w_ref[:, col + c0:col + c0 + sz])
        col += width


def norm_matmul(x, gain, w_bf16, widths, tm):
    m, d = x.shape
    n = w_bf16.shape[1]
    assert sum(widths) == n and m % tm == 0
    kern = functools.partial(_norm_matmul_kernel, widths=tuple(widths), tn=512)
    return pl.pallas_call(
        kern,
        grid=(m // tm,),
        in_specs=[pl.BlockSpec((tm, d), lambda i: (i, 0)),
                  pl.BlockSpec((1, d), lambda i: (0, 0)),
                  pl.BlockSpec((d, n), lambda i: (0, 0))],
        out_specs=[pl.BlockSpec((tm, wd), lambda i: (i, 0)) for wd in widths],
        out_shape=[jax.ShapeDtypeStruct((m, wd), F32) for wd in widths],
        compiler_params=_cparams("parallel"),
    )(x, gain.reshape(1, d), w_bf16)


def _mix_ffn_kernel(*refs, n_act):
    x_ref = refs[0]
    a_refs = refs[1:1 + n_act]
    wo_refs = refs[1 + n_act:1 + 2 * n_act]
    gain_ref, wg_ref, wu_ref, wd_ref, y_ref, x1_sc, h_sc, acc_sc = refs[1 + 2 * n_act:]
    j = pl.program_id(1)

    @pl.when(j == 0)
    def _():
        x1 = x_ref[...]
        for a_ref, wo_ref in zip(a_refs, wo_refs):
            x1 = x1 + _dot(a_ref[...].astype(BF16), wo_ref[...])
        x1_sc[...] = x1
        ms = jnp.mean(x1 * x1, axis=-1, keepdims=True)
        h_sc[...] = (x1 * lax.rsqrt(ms + RMS_EPS) * gain_ref[...]).astype(BF16)
        acc_sc[...] = jnp.zeros_like(acc_sc)

    h = h_sc[...]
    g = _dot(h, wg_ref[...])
    u = _dot(h, wu_ref[...])
    act = (g * _sigmoid(g) * u).astype(BF16)
    acc_sc[...] += _dot(act, wd_ref[...])

    @pl.when(j == pl.num_programs(1) - 1)
    def _():
        y_ref[...] = x1_sc[...] + acc_sc[...]


def mix_ffn(x, acts, wos_bf16, gain, wg, wu, wd, tm, tf):
    m, d = x.shape
    f = wg.shape[1]
    assert m % tm == 0 and f % tf == 0
    n_act = len(acts)
    in_specs = [pl.BlockSpec((tm, d), lambda i, j: (i, 0))]
    in_specs += [pl.BlockSpec((tm, a.shape[1]), lambda i, j: (i, 0)) for a in acts]
    in_specs += [pl.BlockSpec(w.shape, lambda i, j: (0, 0)) for w in wos_bf16]
    in_specs += [pl.BlockSpec((1, d), lambda i, j: (0, 0)),
                 pl.BlockSpec((d, tf), lambda i, j: (0, j)),
                 pl.BlockSpec((d, tf), lambda i, j: (0, j)),
                 pl.BlockSpec((tf, d), lambda i, j: (j, 0))]
    return pl.pallas_call(
        functools.partial(_mix_ffn_kernel, n_act=n_act),
        grid=(m // tm, f // tf),
        in_specs=in_specs,
        out_specs=pl.BlockSpec((tm, d), lambda i, j: (i, 0)),
        out_shape=jax.ShapeDtypeStruct((m, d), F32),
        scratch_shapes=[pltpu.VMEM((tm, d), F32), pltpu.VMEM((tm, d), BF16), pltpu.VMEM((tm, d), F32)],
        compiler_params=_cparams("parallel", "arbitrary"),
    )(x, *acts, *wos_bf16, gain.reshape(1, d), wg, wu, wd)


def _rwkv_prep_kernel(*refs, shift_mode, has_vres, tiles_per_seq):
    it = iter(refs)
    p_ref = next(it)
    prev_ref = next(it)
    mu_ref, w0_ref, a0_ref, kk_ref, ka_ref, w2_ref, a2_ref, g2_ref, grp_ref, grpt_ref = (
        next(it) for _ in range(10))
    if has_vres:
        vf_ref, v0_ref, v1_ref, v2_ref = (next(it) for _ in range(4))
    r_o, lw_o, k_o, v_o, kkn_o, b_o, g_o = (next(it) for _ in range(7))
    if not has_vres:
        vf_o = next(it)

    pf = p_ref[...]
    if shift_mode:
        first = (pl.program_id(0) % tiles_per_seq) == 0
        prev_row = jnp.where(first, 0.0, prev_ref[7:8, :])
        rows = lax.broadcasted_iota(jnp.int32, pf.shape, 0)
        prev = jnp.where(rows == 0, prev_row, pltpu.roll(pf, 1, axis=0))
    else:
        prev = prev_ref[...]
    xs = pf + (prev - pf) * mu_ref[...]
    r = xs[:, 0:RW_WIDTH]
    k = xs[:, RW_WIDTH:2 * RW_WIDTH]
    v = xs[:, 2 * RW_WIDTH:3 * RW_WIDTH]
    wa_lo = xs[:, 3 * RW_WIDTH:3 * RW_WIDTH + LANES]
    g_lo = xs[:, 3 * RW_WIDTH + LANES:RW_IN]
    z = w0_ref[...] + _dot_x3(jnp.tanh(wa_lo), w2_ref[...])
    sp = jnp.maximum(-z, 0.0) + jnp.log(1.0 + jnp.exp(-jnp.abs(z)))
    logw = -jnp.exp(-sp - 0.5)
    a = _sigmoid(a0_ref[...] + _dot_x3(wa_lo, a2_ref[...]))
    g = _dot_x3(_sigmoid(g_lo), g2_ref[...])
    if has_vres:
        gate = _sigmoid(v0_ref[...] + _dot_x3(_dot_x3(v, v1_ref[...]), v2_ref[...]))
        v = v + (vf_ref[...] - v) * gate
    else:
        vf_o[...] = v
    kk = k * kk_ref[...]
    ss = _dot_x2(kk * kk, grp_ref[...])
    inv = _dot_x2(lax.rsqrt(jnp.maximum(ss, 1e-24)), grpt_ref[...])
    kk = kk * inv
    k = k * (1.0 + (a - 1.0) * ka_ref[...])
    r_o[...] = r
    lw_o[...] = logw
    k_o[...] = k
    v_o[...] = v
    kkn_o[...] = kk
    b_o[...] = kk * a
    g_o[...] = g


def rwkv_prep(p_rw, prev, params, vres, shift_mode, seq_len, tm):
    t = p_rw.shape[0]
    assert t % tm == 0
    mu, w0, w2p, a0, a2p, g2, k_k, k_a = params
    grp, grpt = _group_matrices(RW_WIDTH, RW_HEAD)
    row = lambda a: a.reshape(1, -1)
    tile = lambda wd: pl.BlockSpec((tm, wd), lambda i: (i, 0))
    full = lambda a: pl.BlockSpec(a.shape, lambda i: (0,) * a.ndim)
    args = [p_rw]
    specs = [tile(RW_IN)]
    if shift_mode:
        r8 = tm // 8
        args.append(p_rw)
        specs.append(pl.BlockSpec((8, RW_IN), lambda i: (jnp.maximum(i * r8 - 1, 0), 0)))
    else:
        args.append(prev)
        specs.append(tile(RW_IN))
    small = [row(mu), row(w0), row(a0), row(k_k), row(k_a), w2p, a2p, g2, grp, grpt]
    args += small
    specs += [full(a) for a in small]
    has_vres = vres is not None
    if has_vres:
        vf, v0, v1, v2 = vres
        extra = [row(v0), v1, v2]
        args += [vf] + extra
        specs += [tile(RW_WIDTH)] + [full(a) for a in extra]
    n_out = 7 if has_vres else 8
    kern = functools.partial(_rwkv_prep_kernel, shift_mode=shift_mode, has_vres=has_vres,
                             tiles_per_seq=max(seq_len // tm, 1))
    outs = pl.pallas_call(
        kern,
        grid=(t // tm,),
        in_specs=specs,
        out_specs=[tile(RW_WIDTH)] * n_out,
        out_shape=[jax.ShapeDtypeStruct((t, RW_WIDTH), F32)] * n_out,
        compiler_params=_cparams("parallel"),
    )(*args)
    return outs


def _unit_lower_inverse(n, c):
    ri = lax.broadcasted_iota(jnp.int32, (c, c), 0)
    ci = lax.broadcasted_iota(jnp.int32, (c, c), 1)
    eye = (ri == ci).astype(F32)[None]
    blk = min(16, c)
    same = ((ri // blk) == (ci // blk))[None]
    nd = jnp.where(same, n, 0.0)
    x = eye - nd
    pw = nd
    span = 2
    while span < blk:
        pw = _bmm(pw, pw)
        x = _bmm(x, eye + pw)
        span *= 2
    if c > blk:
        p = _bmm(x, jnp.where(same, 0.0, n))
        y = eye - p
        pw = p
        span = 2
        while span < c // blk:
            pw = _bmm(pw, pw)
            y = _bmm(y, eye + pw)
            span *= 2
        x = _bmm(y, x)
    return x


def _rwkv_scan_kernel(r_ref, lw_ref, k_ref, v_ref, kk_ref, b_ref, g_ref, lnw_ref, lnb_ref, rk_ref,
                      s0_ref, tri_ref, o_ref, sout_ref, s_sc, *, chunk):
    l = pl.program_id(2)
    tl = r_ref.shape[1]
    c = chunk
    nc = tl // c
    hd = RW_HEAD
    n_heads = LANES // hd

    @pl.when(l == 0)
    def _():
        s_sc[...] = jnp.zeros_like(s_sc)
        for h in range(n_heads):
            s_sc[h * hd:(h + 1) * hd, h * hd:(h + 1) * hd] = s0_ref[0, h]

    ri = lax.broadcasted_iota(jnp.int32, (c, c), 0)
    ci = lax.broadcasted_iota(jnp.int32, (c, c), 1)
    tri_incl = (ri >= ci)[None]
    tri_strict = (ri > ci)[None]
    kr = lax.broadcasted_iota(jnp.int32, (LANES, LANES), 0)
    kc = lax.broadcasted_iota(jnp.int32, (LANES, LANES), 1)
    same_head = (kr // hd) == (kc // hd)
    head_of_lane = lax.broadcasted_iota(jnp.int32, (1, 1, LANES), 2) // hd
    to3 = lambda x: x.reshape(nc, c, LANES)

    lw = to3(lw_ref[0])
    lc = to3(_split_dot(tri_ref[...], lw_ref[0]))
    tot = lc[:, c - 1:c, :]
    r, k, v, kk, b = to3(r_ref[0]), to3(k_ref[0]), to3(v_ref[0]), to3(kk_ref[0]), to3(b_ref[0])
    e_neg = jnp.exp(-lc)
    e_end = jnp.exp(tot - lc)
    kk_t = kk * jnp.exp(lc - lw)
    r_t = r * jnp.exp(lc)
    b_t = b * e_neg
    k_t = k * e_neg
    wm = u0 = q_eff = o0 = None
    for h in range(n_heads):
        mine = head_of_lane == h
        lhs = jnp.concatenate([jnp.where(mine, kk_t, 0.0), jnp.where(mine, r_t, 0.0)], axis=1)
        g_b = _bmm_nt(lhs, b_t)
        g_k = _bmm_nt(lhs, k_t)
        a_ab = jnp.where(tri_strict, g_b[:, :c], 0.0)
        a_rb = jnp.where(tri_incl, g_b[:, c:], 0.0)
        a_ak = jnp.where(tri_strict, g_k[:, :c], 0.0)
        a_rk = jnp.where(tri_incl, g_k[:, c:], 0.0)
        tinv = _unit_lower_inverse(a_ab, c)
        wu = _bmm(tinv, jnp.concatenate([kk_t, _bmm(a_ak, v)], axis=2))
        wm_h, u0_h = wu[:, :, :LANES], -wu[:, :, LANES:]
        ru = _bmm(a_rb, jnp.concatenate([wm_h, u0_h], axis=2))
        qe_h = r_t - ru[:, :, :LANES]
        o0_h = _bmm(a_rk, v) + ru[:, :, LANES:]
        if h == 0:
            wm, u0, q_eff, o0 = wm_h, u0_h, qe_h, o0_h
        else:
            wm, u0 = jnp.where(mine, wm_h, wm), jnp.where(mine, u0_h, u0)
            q_eff, o0 = jnp.where(mine, qe_h, q_eff), jnp.where(mine, o0_h, o0)
    eye = (kr == kc).astype(F32)[None]
    m_mat = jnp.where(same_head[None], eye * jnp.exp(tot) - _bmm_tn(wm, b * e_end), 0.0)
    d_mat = jnp.where(same_head[None],
                      _bmm_tn(jnp.concatenate([u0, v], axis=1),
                              jnp.concatenate([b * e_end, k * e_end], axis=1)), 0.0)
    s = s_sc[...]
    states = []
    for i in range(nc):
        states.append(s)
        s = _dot(_bf(s), _bf(m_mat[i])) + d_mat[i]
    s_sc[...] = s
    o = (o0 + _bmm_nt(q_eff, jnp.stack(states))).reshape(tl, LANES)
    head_sum = _bf(jnp.where(same_head, 1.0, 0.0))
    mean = _dot_x2(o, head_sum) * (1.0 / hd)
    var = _dot_x2(jnp.square(o - mean), head_sum) * (1.0 / hd)
    o = (o - mean) * lax.rsqrt(var + GN_EPS) * lnw_ref[...] + lnb_ref[...]
    bonus = _dot_x2(r_ref[0] * k_ref[0] * rk_ref[...], head_sum) * v_ref[0]
    o_ref[0] = (o + bonus) * g_ref[0]

    @pl.when(l == pl.num_programs(2) - 1)
    def _():
        for h in range(n_heads):
            sout_ref[0, h] = s[h * hd:(h + 1) * hd, h * hd:(h + 1) * hd]


def rwkv_scan(r, lw, k, v, kk, b, g, ln_w, ln_b, r_k, s0, tl, chunk):
    bsz, seq, _ = r.shape
    assert seq % tl == 0 and tl % chunk == 0
    hp = RW_WIDTH // LANES
    per = LANES // RW_HEAD
    seq_spec = pl.BlockSpec((1, tl, LANES), lambda bi, hi, li: (bi, li, hi))
    par_spec = pl.BlockSpec((1, LANES), lambda bi, hi, li: (0, hi))
    st_spec = pl.BlockSpec((1, per, RW_HEAD, RW_HEAD), lambda bi, hi, li: (bi, hi, 0, 0))
    row = lambda a: a.reshape(1, RW_WIDTH)
    idx = np.arange(tl)
    tri = jnp.asarray(((idx[:, None] >= idx[None, :]) & (idx[:, None] // chunk == idx[None, :] // chunk)),
                      dtype=BF16)
    return pl.pallas_call(
        functools.partial(_rwkv_scan_kernel, chunk=chunk),
        grid=(bsz, hp, seq // tl),
        in_specs=[seq_spec] * 7 + [par_spec] * 3 + [st_spec, pl.BlockSpec((tl, tl), lambda bi, hi, li: (0, 0))],
        out_specs=[seq_spec, st_spec],
        out_shape=[jax.ShapeDtypeStruct((bsz, seq, RW_WIDTH), F32),
                   jax.ShapeDtypeStruct((bsz, RW_HEADS, RW_HEAD, RW_HEAD), F32)],
        scratch_shapes=[pltpu.VMEM((LANES, LANES), F32)],
        compiler_params=_cparams("parallel", "parallel", "arbitrary"),
    )(r, lw, k, v, kk, b, g, row(ln_w), row(ln_b), row(r_k), s0, tri)


def _moba_prep_kernel(p_ref, qg_ref, kg_ref, grp_ref, grpt_ref, *refs):
    q_o, k_o, v_o, kb_o, vt_o, ks_o = refs[-6:]
    p = p_ref[...]
    q = p[:, 0:MB_WIDTH]
    k = p[:, MB_WIDTH:2 * MB_WIDTH]
    v = p[:, 2 * MB_WIDTH:3 * MB_WIDTH]

    def head_norm(x, gain_ref):
        ms = _dot_x2(x * x, grp_ref[...]) * (1.0 / MB_HEAD)
        inv = _dot_x2(lax.rsqrt(ms + RMS_EPS), grpt_ref[...])
        return x * inv * gain_ref[...]

    qn = head_norm(q, qg_ref)
    kn = head_norm(k, kg_ref)
    q_o[...] = qn
    k_o[0] = kn
    v_o[0] = v
    kb_o[...] = kn.astype(BF16)
    vt_o[...] = v.T.astype(BF16)
    ks_o[0] = jnp.sum(kn, axis=0, keepdims=True)


def moba_prep(p_mb, q_gain, k_gain, tm, layer=0, n_layers=1, stacks=None):
    t = p_mb.shape[0]
    assert t % tm == 0
    grp, grpt = _group_matrices(MB_WIDTH, MB_HEAD)
    tile = lambda i: (i, 0)
    full = lambda a: pl.BlockSpec(a.shape, lambda i: (0,) * a.ndim)
    qg = jnp.tile(q_gain, MB_HEADS).reshape(1, MB_WIDTH)
    kg = jnp.tile(k_gain, MB_HEADS).reshape(1, MB_WIDTH)
    args = [p_mb, qg, kg, grp, grpt]
    in_specs = [pl.BlockSpec((tm, 3 * MB_WIDTH), tile), full(qg), full(kg), full(grp), full(grpt)]
    aliases = {}
    if stacks is not None:
        aliases = {len(args): 1, len(args) + 1: 2}
        args += list(stacks)
        in_specs += [pl.BlockSpec(memory_space=pl.ANY)] * 2
    slab = pl.BlockSpec((1, tm, MB_WIDTH), lambda i: (layer, i, 0))
    stack_shape = jax.ShapeDtypeStruct((n_layers, t, MB_WIDTH), F32)
    return pl.pallas_call(
        _moba_prep_kernel,
        grid=(t // tm,),
        in_specs=in_specs,
        out_specs=[pl.BlockSpec((tm, MB_WIDTH), tile), slab, slab, pl.BlockSpec((tm, MB_WIDTH), tile),
                   pl.BlockSpec((MB_WIDTH, tm), lambda i: (0, i)),
                   pl.BlockSpec((1, 1, MB_WIDTH), lambda i: (i, 0, 0))],
        out_shape=[jax.ShapeDtypeStruct((t, MB_WIDTH), F32), stack_shape, stack_shape,
                   jax.ShapeDtypeStruct((t, MB_WIDTH), BF16), jax.ShapeDtypeStruct((MB_WIDTH, t), BF16),
                   jax.ShapeDtypeStruct((t // tm, 1, MB_WIDTH), F32)],
        input_output_aliases=aliases,
        compiler_params=_cparams("parallel"),
    )(*args)


def _top_blocks(gate, idx, axis):
    sel = jnp.zeros_like(gate)
    rem = gate
    for _ in range(MOBA_TOPK):
        mx = jnp.max(rem, axis=axis, keepdims=True)
        live = jnp.logical_and(rem == mx, rem > 0.5 * MASK_NEG)
        first = jnp.min(jnp.where(live, idx, 1e9), axis=axis, keepdims=True)
        pick = idx == first
        sel = jnp.where(pick, 1.0, sel)
        rem = jnp.where(pick, MASK_NEG, rem)
    return sel


def _moba_attn_kernel(q_ref, k_ref, vt_ref, ks_ref, o_ref, s_a, s_b, p_a, p_b):
    qi = pl.program_id(2)
    tq = q_ref.shape[1]
    blk = MOBA_BLOCK
    assert tq == 2 * blk
    nb = ks_ref.shape[1]
    n_heads = LANES // MB_HEAD
    ncol = n_heads * tq
    q = q_ref[0]
    lane = lax.broadcasted_iota(jnp.int32, (1, LANES), 1)
    bmean = ks_ref[0] * (1.0 / MOBA_BLOCK)
    blk_f = lax.broadcasted_iota(jnp.int32, (nb, tq), 0).astype(F32)
    second = lax.broadcasted_iota(jnp.int32, (1, tq), 1) >= blk
    own_f = (2 * qi).astype(F32) + jnp.where(second, 1.0, 0.0)
    q_cols, bias_cols, own_bias = [], [], []
    pad_rows = jnp.full((LANES - nb, tq), MASK_NEG, F32)
    first_blk = lax.broadcasted_iota(jnp.int32, (nb, tq), 0) == 2 * qi
    for h in range(n_heads):
        q_h = jnp.where((lane // MB_HEAD) == h, q, 0.0)
        gate_t = _dot_x3_nt(bmean, q_h)
        gate_t = jnp.where(blk_f < own_f, gate_t, MASK_NEG)
        bias = (1.0 - _top_blocks(gate_t, blk_f, 0)) * MASK_NEG
        own_bias.append(jnp.max(jnp.where(first_blk, bias, MASK_NEG), axis=0, keepdims=True))
        bias_cols.append(jnp.concatenate([bias, pad_rows], axis=0).T)
        q_cols.append(q_h * (LOG2E * MB_HEAD ** -0.5))
    q_all = _bf(jnp.concatenate(q_cols, axis=0))
    q_ext = jnp.concatenate([q_all, _bf(jnp.concatenate(bias_cols, axis=0))], axis=1)

    def ones_rows(n):
        return _bf(jnp.where(lax.broadcasted_iota(jnp.int32, (16, n), 0) == 0, 1.0, 0.0))

    def values(r0, p):
        out = []
        for h in range(n_heads):
            vt_h = jnp.concatenate([vt_ref[h * MB_HEAD:(h + 1) * MB_HEAD, pl.ds(r0, tq)], ones_rows(tq)], axis=0)
            out.append(_dot(vt_h, p[:, h * tq:(h + 1) * tq]))
        return jnp.concatenate(out, axis=1)

    row0 = pl.multiple_of(qi * tq, tq)
    ri = lax.broadcasted_iota(jnp.int32, (tq, ncol), 0)
    cq = lax.broadcasted_iota(jnp.int32, (tq, ncol), 1) % tq
    gated = jnp.logical_and(ri < blk, cq >= blk)
    s = _dot_nt(k_ref[0, pl.ds(row0, tq), :], q_all) + jnp.where(gated, jnp.concatenate(own_bias, axis=1), 0.0)
    s = jnp.where(ri <= cq, s, MASK_NEG)
    m = jnp.max(s, axis=0, keepdims=True)
    acc = values(row0, _bf(jnp.exp2(s - m)))

    lane_i = lax.broadcasted_iota(jnp.int32, (1, LANES), 1)

    def pair_rows(i):
        return pl.multiple_of(jnp.minimum(i, nb // 2 - 1) * tq, tq)

    def scores(i, s_ref):
        k2 = k_ref[0, pl.ds(pair_rows(i), tq), :]
        onehot = [jnp.broadcast_to(_bf(jnp.where(lane_i == jnp.where(i < qi, 2 * i + u, LANES - 1), 1.0, 0.0)),
                                   (blk, LANES)) for u in range(2)]
        k_ext = jnp.concatenate([k2, jnp.concatenate(onehot, axis=0)], axis=1)
        s_ref[...] = _dot_nt(k_ext, q_ext)

    def softmax(s_ref, p_ref, m):
        s = s_ref[...]
        m_new = jnp.maximum(m, jnp.max(s, axis=0, keepdims=True))
        p_ref[...] = _bf(jnp.exp2(s - m_new))
        return m_new, jnp.exp2(m - m_new)

    scores(0, s_a)
    p_b[...] = jnp.zeros_like(p_b)

    def body(t, carry):
        m, acc = carry
        scores(2 * t + 1, s_b)
        pv = values(pair_rows(jnp.maximum(2 * t - 1, 0)), p_b[...])
        m, alpha = softmax(s_a, p_a, m)
        acc = (acc + pv) * alpha
        scores(2 * t + 2, s_a)
        pv = values(pair_rows(2 * t), p_a[...])
        m, alpha = softmax(s_b, p_b, m)
        acc = (acc + pv) * alpha
        return m, acc

    trips = (qi + 1) // 2
    m, acc = lax.fori_loop(0, trips, body, (m, acc))
    acc = acc + values(pair_rows(jnp.maximum(2 * trips - 1, 0)), p_b[...])
    o = acc[:MB_HEAD] / acc[MB_HEAD:MB_HEAD + 1]
    o_ref[0] = jnp.concatenate([o[:, h * tq:(h + 1) * tq] for h in range(n_heads)], axis=0).T


def moba_attn(qn, kb, vt, ksum):
    bsz, seq, _ = qn.shape
    tq = 2 * MOBA_BLOCK
    nb = seq // MOBA_BLOCK
    assert seq % tq == 0 and nb <= LANES
    hp = MB_WIDTH // LANES
    ncol = (LANES // MB_HEAD) * tq
    return pl.pallas_call(
        _moba_attn_kernel,
        grid=(bsz, hp, seq // tq),
        in_specs=[pl.BlockSpec((1, tq, LANES), lambda b, h, i: (b, i, h)),
                  pl.BlockSpec((1, seq, LANES), lambda b, h, i: (b, 0, h)),
                  pl.BlockSpec((LANES, seq), lambda b, h, i: (h, b)),
                  pl.BlockSpec((1, nb, LANES), lambda b, h, i: (b, 0, h))],
        out_specs=pl.BlockSpec((1, tq, LANES), lambda b, h, i: (b, i, h)),
        out_shape=jax.ShapeDtypeStruct((bsz, seq, MB_WIDTH), F32),
        scratch_shapes=[pltpu.VMEM((tq, ncol), F32), pltpu.VMEM((tq, ncol), F32),
                        pltpu.VMEM((tq, ncol), BF16), pltpu.VMEM((tq, ncol), BF16)],
        compiler_params=_cparams("parallel", "parallel", "arbitrary"),
    )(qn, kb, vt, ksum)


def _moba_scores_kernel(pt_ref, q_ref, *refs, n_pg):
    k_refs = refs[:n_pg]
    s_o = refs[n_pg]
    q = q_ref[0]
    for j in range(n_pg):
        s_o[0, j] = jnp.sum(k_refs[j][0, 0] * q, axis=1)


def _moba_select_kernel(s_ref, q_ref, kn_ref, p_o, w_o, id_o, *, ppb):
    scale = MB_HEAD ** -0.5
    s = s_ref[0]
    n_lp = s.shape[0]
    nblk = n_lp // ppb
    page_sum = jnp.sum(s, axis=-1, keepdims=True)
    gate = jnp.sum(page_sum.reshape(nblk, ppb, MB_HEADS, 1), axis=1) * (1.0 / MOBA_BLOCK)
    idx = lax.broadcasted_iota(jnp.int32, gate.shape, 0).astype(F32)
    lane = lax.broadcasted_iota(jnp.int32, (MB_HEADS, LANES), 1)
    sel = jnp.zeros_like(gate)
    rem = gate
    ids = jnp.zeros((MB_HEADS, LANES), F32)
    for t in range(MOBA_TOPK):
        mx = jnp.max(rem, axis=0, keepdims=True)
        first = jnp.min(jnp.where(rem == mx, idx, 1e9), axis=0, keepdims=True)
        pick = idx == first
        sel = jnp.where(pick, 1.0, sel)
        rem = jnp.where(pick, MASK_NEG, rem)
        for u in range(ppb):
            ids = jnp.where(lane == t * ppb + u, first[0] * ppb + u, ids)
    id_o[0] = ids.astype(jnp.int32)
    sel_pg = jnp.broadcast_to(sel[:, None], (nblk, ppb, MB_HEADS, 1)).reshape(n_lp, MB_HEADS, 1) > 0.5
    s_own = jnp.sum(kn_ref[0] * q_ref[0], axis=-1, keepdims=True) * scale
    sm = jnp.where(sel_pg, s * scale, MASK_NEG)
    m_all = jnp.maximum(jnp.max(jnp.max(sm, axis=0), axis=-1, keepdims=True), s_own)
    p = jnp.where(sel_pg, jnp.exp(sm - m_all[None]), 0.0)
    w_own = jnp.exp(s_own - m_all)
    inv = 1.0 / (jnp.sum(jnp.sum(p, axis=0), axis=-1, keepdims=True) + w_own)
    p_o[0] = p * inv[None]
    w_o[0] = jnp.broadcast_to(w_own * inv, (MB_HEADS, LANES))


def _moba_values_kernel(pt_ref, id_ref, p_ref, w_ref, vn_ref, *refs, n_sel, hps):
    v_refs = refs[:hps * n_sel]
    o_ref = refs[hps * n_sel]
    b = pl.program_id(0)
    for hh in range(hps):
        h = pl.program_id(1) * hps + hh
        acc = w_ref[0, pl.ds(h, 1), 0:1] * vn_ref[0, hh]
        for j in range(n_sel):
            page = id_ref[(b * MB_HEADS + h) * n_sel + j]
            prob = p_ref[0, page, pl.ds(h, 1), :]
            acc = acc + _dot_nt(prob, v_refs[hh * n_sel + j][0, 0, 0], HI)
        o_ref[0, hh] = acc


def moba_decode(qn, kn, vn, k_cache_t, v_cache_t, page_table, layer):
    bsz = qn.shape[0]
    n_lp = page_table.shape[1]
    ppb = MOBA_BLOCK // PAGE_SIZE
    assert n_lp % ppb == 0 and n_lp // ppb >= MOBA_TOPK
    n_pg = 16
    assert n_lp % n_pg == 0
    n_sel = MOBA_TOPK * ppb
    hps = MB_HEADS
    heads = lambda a: a.reshape(bsz, MB_HEADS, MB_HEAD)
    page_blk = (1, 1, MB_HEADS, MB_HEAD, PAGE_SIZE)

    scores = pl.pallas_call(
        functools.partial(_moba_scores_kernel, n_pg=n_pg),
        grid_spec=pltpu.PrefetchScalarGridSpec(
            num_scalar_prefetch=1,
            grid=(bsz, n_lp // n_pg),
            in_specs=[pl.BlockSpec((1, MB_HEADS, MB_HEAD, 1), lambda b, g, pt: (b, 0, 0, 0))]
                     + [pl.BlockSpec(page_blk, functools.partial(
                         lambda b, g, pt, j: (layer, pt[b, g * n_pg + j], 0, 0, 0), j=j)) for j in range(n_pg)],
            out_specs=pl.BlockSpec((1, n_pg, MB_HEADS, PAGE_SIZE), lambda b, g, pt: (b, g, 0, 0))),
        out_shape=jax.ShapeDtypeStruct((bsz, n_lp, MB_HEADS, PAGE_SIZE), F32),
        compiler_params=_cparams("parallel", "arbitrary"),
    )(page_table, qn.reshape(bsz, MB_HEADS, MB_HEAD, 1), *([k_cache_t] * n_pg))

    vec = pl.BlockSpec((1, MB_HEADS, MB_HEAD), lambda b: (b, 0, 0))
    probs, w_own, ids = pl.pallas_call(
        functools.partial(_moba_select_kernel, ppb=ppb),
        grid=(bsz,),
        in_specs=[pl.BlockSpec((1, n_lp, MB_HEADS, PAGE_SIZE), lambda b: (b, 0, 0, 0)), vec, vec],
        out_specs=[pl.BlockSpec((1, n_lp, MB_HEADS, PAGE_SIZE), lambda b: (b, 0, 0, 0)),
                   pl.BlockSpec((1, MB_HEADS, LANES), lambda b: (b, 0, 0)),
                   pl.BlockSpec((1, MB_HEADS, LANES), lambda b: (b, 0, 0))],
        out_shape=[jax.ShapeDtypeStruct((bsz, n_lp, MB_HEADS, PAGE_SIZE), F32),
                   jax.ShapeDtypeStruct((bsz, MB_HEADS, LANES), F32),
                   jax.ShapeDtypeStruct((bsz, MB_HEADS, LANES), jnp.int32)],
        compiler_params=_cparams("parallel"),
    )(scores, heads(qn), heads(kn))

    sel_ids = ids[:, :, :n_sel].reshape(-1)
    out = pl.pallas_call(
        functools.partial(_moba_values_kernel, n_sel=n_sel, hps=hps),
        grid_spec=pltpu.PrefetchScalarGridSpec(
            num_scalar_prefetch=2,
            grid=(bsz, MB_HEADS // hps),
            in_specs=[pl.BlockSpec((1, n_lp, MB_HEADS, PAGE_SIZE), lambda b, g, pt, sid: (b, 0, 0, 0)),
                      pl.BlockSpec((1, MB_HEADS, LANES), lambda b, g, pt, sid: (b, 0, 0)),
                      pl.BlockSpec((1, hps, 1, MB_HEAD), lambda b, g, pt, sid: (b, g, 0, 0))]
                     + [pl.BlockSpec((1, 1, 1, MB_HEAD, PAGE_SIZE), functools.partial(
                         lambda b, g, pt, sid, hh, j: (
                             layer, pt[b, sid[(b * MB_HEADS + g * hps + hh) * n_sel + j]], g * hps + hh, 0, 0),
                         hh=hh, j=j)) for hh in range(hps) for j in range(n_sel)],
            out_specs=pl.BlockSpec((1, hps, 1, MB_HEAD), lambda b, g, pt, sid: (b, g, 0, 0))),
        out_shape=jax.ShapeDtypeStruct((bsz, MB_HEADS, 1, MB_HEAD), F32),
        compiler_params=_cparams("parallel", "arbitrary"),
    )(page_table, sel_ids, probs, w_own, vn.reshape(bsz, MB_HEADS, 1, MB_HEAD), *([v_cache_t] * (hps * n_sel)))
    return out.reshape(bsz, MB_WIDTH)


def _hgrn_constants(c):
    rows = np.arange(c)
    dmats, ups, los, masks = [], [], [], []
    z = c
    while z >= 2:
        seg = rows // z
        mid = seg * z + z // 2
        up = rows >= mid
        d = np.zeros((c, c), np.float32)
        for i in rows:
            if up[i]:
                d[i, mid[i]:i + 1] = 1.0
            else:
                d[i, i + 1:mid[i]] = 1.0
        dmats.append(d)
        ups.append(np.repeat(up[:, None], HG_D, 1).astype(np.float32))
        los.append(np.repeat(~up[:, None], HG_D, 1).astype(np.float32))
        masks.append(((seg[:, None] == seg[None, :]) & up[:, None] & (~up)[None, :]).astype(np.float32))
        z //= 2
    tri = np.tril(np.ones((c, c), np.float32))
    n_wide = sum(1 for lev in range(len(dmats)) if (c >> lev) >= HG_WIDE_SEG)
    return (jnp.asarray(np.concatenate(dmats[n_wide:] + [tri], 0), dtype=BF16), jnp.asarray(np.stack(ups)),
            jnp.asarray(np.stack(los)), jnp.asarray(np.stack(masks)))


def _hgrn_kernel(pq_ref, pf_ref, pi_ref, pg_ref, lbl_ref, gain_ref, s0_ref, dd_ref, up_ref, lo_ref,
                 mk_ref, o_ref, sout_ref, st_sc, *, chunk, layer):
    l = pl.program_id(2)
    tl = pq_ref.shape[1]
    c = chunk
    nc = tl // c
    nlev = up_ref.shape[0]
    er = lax.broadcasted_iota(jnp.int32, (HG_D, HG_D), 0)
    ec = lax.broadcasted_iota(jnp.int32, (HG_D, HG_D), 1)
    eye_d = (er == ec).astype(F32)

    @pl.when(l == 0)
    def _():
        st_sc[...] = _dot_nt(eye_d, s0_ref[0, 0], HI)

    logits = lbl_ref[...]
    ex = jnp.exp(logits - jnp.max(logits, axis=0, keepdims=True))
    prob = ex / jnp.sum(ex, axis=0, keepdims=True)
    lb = jnp.sum(prob[0:layer + 1], axis=0, keepdims=True) - prob[0:1]

    qr = pq_ref[0]
    q = qr * _sigmoid(qr)
    forget = lb + (1.0 - lb) * _sigmoid(pf_ref[0])
    logf = jnp.log(jnp.maximum(forget, 1e-30))
    kg = 1.0 - forget
    v = pi_ref[0]

    ri = lax.broadcasted_iota(jnp.int32, (c, c), 0)
    ci = lax.broadcasted_iota(jnp.int32, (c, c), 1)
    diag = ri == ci
    rows = [slice(i * c, (i + 1) * c) for i in range(nc)]
    x_all = _split_dot(dd_ref[...], jnp.concatenate([logf[rs] for rs in rows], axis=1))
    n_wide = sum(1 for lev in range(nlev) if (c >> lev) >= HG_WIDE_SEG)
    n_nar = nlev - n_wide
    cum = jnp.concatenate([x_all[n_nar * c:(n_nar + 1) * c, i * HG_D:(i + 1) * HG_D] for i in range(nc)], axis=0)
    e_wide = []
    for lev in range(n_wide):
        z = c >> lev
        cum_z = cum.reshape(tl // z, z, HG_D)
        e_wide.append(jnp.exp(-jnp.abs(cum_z - cum_z[:, z // 2 - 1:z // 2, :])).reshape(tl, HG_D))
    cum_c = cum.reshape(nc, c, HG_D)
    e_in_all = jnp.exp(cum)
    e_out_all = jnp.exp(cum_c[:, c - 1:c, :] - cum_c).reshape(tl, HG_D)
    e_nar = jnp.exp(x_all[:n_nar * c])
    intra, kv, qe, dec = [], [], [], []
    for i, rs in enumerate(rows):
        qc, kc, vc = q[rs], kg[rs], v[rs]
        a = jnp.where(diag, jnp.sum(qc * kc, axis=1, keepdims=True), 0.0)
        for lev in range(nlev):
            if lev < n_wide:
                el = e_wide[lev][rs]
            else:
                el = e_nar[(lev - n_wide) * c:(lev - n_wide + 1) * c, i * HG_D:(i + 1) * HG_D]
            a = a + mk_ref[lev] * _dot_nt(_bf(qc * el * up_ref[lev]), _bf(kc * el * lo_ref[lev]))
        e_in = e_in_all[rs]
        e_out = e_out_all[rs]
        intra.append(_dot(_bf(a), _bf(vc)))
        kv.append(_dot_tn(_bf(vc), _bf(kc * e_out)))
        qe.append(_bf(qc * e_in))
        dec.append(e_in[c - 1:c, :])
    st = st_sc[...]
    outs = []
    for i in range(nc):
        outs.append(_dot_nt(qe[i], _bf(st)) + intra[i])
        st = st * dec[i] + kv[i]
    st_sc[...] = st
    o = jnp.concatenate(outs, axis=0) if nc > 1 else outs[0]
    ms = jnp.mean(o * o, axis=-1, keepdims=True)
    gr = pg_ref[0]
    o_ref[0] = o * lax.rsqrt(ms + RMS_EPS) * gain_ref[...] * (gr * _sigmoid(gr))

    @pl.when(l == pl.num_programs(2) - 1)
    def _():
        sout_ref[0, 0] = _dot_nt(eye_d, st, HI)


def hgrn_scan(p, lb_logits, out_gain, s0, layer, tl, chunk):
    bsz, seq, _ = p.shape
    assert seq % tl == 0 and tl % chunk == 0
    dd, up, lo, mk = _hgrn_constants(chunk)
    col = lambda off: pl.BlockSpec((1, tl, HG_D), lambda b, h, li: (b, li, off + h))
    full = lambda a: pl.BlockSpec(a.shape, lambda b, h, li: (0,) * a.ndim)
    st_spec = pl.BlockSpec((1, 1, HG_D, HG_D), lambda b, h, li: (b, h, 0, 0))
    return pl.pallas_call(
        functools.partial(_hgrn_kernel, chunk=chunk, layer=layer),
        grid=(bsz, HG_HEADS, seq // tl),
        in_specs=[col(0), col(HG_HEADS), col(2 * HG_HEADS), col(3 * HG_HEADS),
                  pl.BlockSpec((lb_logits.shape[0], HG_D), lambda b, h, li: (0, h)),
                  pl.BlockSpec((1, HG_D), lambda b, h, li: (0, 0)),
                  st_spec, full(dd), full(up), full(lo), full(mk)],
        out_specs=[pl.BlockSpec((1, tl, HG_D), lambda b, h, li: (b, li, h)), st_spec],
        out_shape=[jax.ShapeDtypeStruct((bsz, seq, D_MODEL), F32),
                   jax.ShapeDtypeStruct((bsz, HG_HEADS, HG_D, HG_D), F32)],
        scratch_shapes=[pltpu.VMEM((HG_D, HG_D), F32)],
        compiler_params=_cparams("parallel", "parallel", "arbitrary"),
    )(p, p, p, p, lb_logits, out_gain.reshape(1, HG_D), s0, dd, up, lo, mk)


def _row_to_col(row, eye):
    return jnp.sum(jnp.where(eye, row, 0.0), axis=1, keepdims=True)


def _col_to_row(col, eye):
    return jnp.sum(jnp.where(eye, col, 0.0), axis=0, keepdims=True)


def _rwkv_step_kernel(r_ref, lw_ref, k_ref, v_ref, kk_ref, b_ref, g_ref, lnw_ref, lnb_ref, rk_ref,
                      s0_ref, o_ref, sout_ref):
    hd = RW_HEAD
    ri = lax.broadcasted_iota(jnp.int32, (hd, hd), 0)
    ci = lax.broadcasted_iota(jnp.int32, (hd, hd), 1)
    eye = ri == ci
    for h in range(RW_HEADS):
        sl = slice(h * hd, (h + 1) * hd)
        row = lambda ref: ref[0][:, sl]
        r, k, v, kk, b = row(r_ref), row(k_ref), row(v_ref), row(kk_ref), row(b_ref)
        s = s0_ref[0, h]
        s_kk = jnp.sum(s * kk, axis=1, keepdims=True)
        s = s * jnp.exp(row(lw_ref)) - s_kk * b + _row_to_col(v, eye) * k
        sout_ref[0, h] = s
        o = _col_to_row(jnp.sum(s * r, axis=1, keepdims=True), eye)
        mean = jnp.mean(o, axis=-1, keepdims=True)
        var = jnp.mean(jnp.square(o - mean), axis=-1, keepdims=True)
        o = (o - mean) * lax.rsqrt(var + GN_EPS) * lnw_ref[:, sl] + lnb_ref[:, sl]
        bonus = jnp.sum(r * k * rk_ref[:, sl], axis=-1, keepdims=True) * v
        o_ref[0, :, sl] = (o + bonus) * row(g_ref)


def rwkv_step(r, lw, k, v, kk, b, g, ln_w, ln_b, r_k, s0):
    bsz = r.shape[0]
    seq_spec = pl.BlockSpec((1, 1, RW_WIDTH), lambda i: (i, 0, 0))
    par_spec = pl.BlockSpec((1, RW_WIDTH), lambda i: (0, 0))
    st_spec = pl.BlockSpec((1, RW_HEADS, RW_HEAD, RW_HEAD), lambda i: (i, 0, 0, 0))
    row = lambda a: a.reshape(1, RW_WIDTH)
    tok = lambda a: a.reshape(bsz, 1, RW_WIDTH)
    o, s = pl.pallas_call(
        _rwkv_step_kernel,
        grid=(bsz,),
        in_specs=[seq_spec] * 7 + [par_spec] * 3 + [st_spec],
        out_specs=[seq_spec, st_spec],
        out_shape=[jax.ShapeDtypeStruct((bsz, 1, RW_WIDTH), F32),
                   jax.ShapeDtypeStruct((bsz, RW_HEADS, RW_HEAD, RW_HEAD), F32)],
        compiler_params=_cparams("parallel"),
    )(*[tok(a) for a in (r, lw, k, v, kk, b, g)], row(ln_w), row(ln_b), row(r_k), s0)
    return o.reshape(bsz, RW_WIDTH), s


def _hgrn_step_kernel(p_ref, lbl_ref, gain_ref, s0_ref, o_ref, sout_ref, *, layer):
    d = HG_D
    ri = lax.broadcasted_iota(jnp.int32, (d, d), 0)
    ci = lax.broadcasted_iota(jnp.int32, (d, d), 1)
    eye = ri == ci
    logits = lbl_ref[...]
    ex = jnp.exp(logits - jnp.max(logits, axis=0, keepdims=True))
    prob = ex / jnp.sum(ex, axis=0, keepdims=True)
    lb_all = jnp.sum(prob[0:layer + 1], axis=0, keepdims=True) - prob[0:1]
    width = HG_HEADS * d
    for h in range(HG_HEADS):
        col = lambda part: p_ref[0][:, part * width + h * d:part * width + (h + 1) * d]
        qr, fr, v, gr = col(0), col(1), col(2), col(3)
        lb = lb_all[:, h * d:(h + 1) * d]
        q = qr * _sigmoid(qr)
        forget = lb + (1.0 - lb) * _sigmoid(fr)
        decay = jnp.maximum(forget, 1e-30)
        s = _row_to_col(decay, eye) * s0_ref[0, h] + _row_to_col(1.0 - forget, eye) * v
        sout_ref[0, h] = s
        o = jnp.sum(_row_to_col(q, eye) * s, axis=0, keepdims=True)
        ms = jnp.mean(o * o, axis=-1, keepdims=True)
        o_ref[0, :, h * d:(h + 1) * d] = o * lax.rsqrt(ms + RMS_EPS) * gain_ref[...] * (gr * _sigmoid(gr))


def hgrn_step(p, lb_logits, out_gain, s0, layer):
    bsz = p.shape[0]
    st_spec = pl.BlockSpec((1, HG_HEADS, HG_D, HG_D), lambda i: (i, 0, 0, 0))
    o, s = pl.pallas_call(
        functools.partial(_hgrn_step_kernel, layer=layer),
        grid=(bsz,),
        in_specs=[pl.BlockSpec((1, 1, p.shape[1]), lambda i: (i, 0, 0)),
                  pl.BlockSpec(lb_logits.shape, lambda i: (0, 0)),
                  pl.BlockSpec((1, HG_D), lambda i: (0, 0)), st_spec],
        out_specs=[pl.BlockSpec((1, 1, D_MODEL), lambda i: (i, 0, 0)), st_spec],
        out_shape=[jax.ShapeDtypeStruct((bsz, 1, D_MODEL), F32),
                   jax.ShapeDtypeStruct((bsz, HG_HEADS, HG_D, HG_D), F32)],
        compiler_params=_cparams("parallel"),
    )(p.reshape(bsz, 1, -1), lb_logits, out_gain.reshape(1, HG_D), s0)
    return o.reshape(bsz, D_MODEL), s


def _even_layer(xp, xs, il, n_even, vf_p, vf_s, kv_stacks, w, cache_k, cache_v, page_table, state_rwkv,
                state_shift):
    nb_p, seq, d = xp.shape
    nb_s = xs.shape[0]
    tp = nb_p * seq
    rw_params = (w['rw_mu'][il], w['rw_w0'][il], w['w2p'][il], w['rw_a0'][il], w['a2p'][il],
                 w['rw_g2'][il], w['rw_k_k'][il], w['rw_k_a'][il])
    has_vres = il > 0
    vres_w = (w['rw_v0'][il - 1], w['rw_v1'][il - 1], w['rw_v2'][il - 1]) if has_vres else None
    ln = (w['rw_ln_w'][il], w['rw_ln_b'][il], w['rw_r_k'][il].reshape(-1))

    p_rw, p_mb = norm_matmul(xp.reshape(tp, d), w['norm_mix'][2 * il], w['w_in_even'][il],
                             (RW_IN, 3 * MB_WIDTH), 512)
    vres = (vf_p,) + vres_w if has_vres else None
    prep = rwkv_prep(p_rw, None, rw_params, vres, True, seq, 512)
    if not has_vres:
        vf_p = prep[7]
    seq3 = lambda a: a.reshape(nb_p, seq, -1)
    o_rw, wkv_p = rwkv_scan(*[seq3(a) for a in prep[:7]], *ln,
                            jnp.zeros((nb_p, RW_HEADS, RW_HEAD, RW_HEAD), F32), 1024, RW_CHUNK)
    qn, k_stack, v_stack, kb, vt, ksum = moba_prep(p_mb, w['mb_q_norm'][il], w['mb_k_norm'][il], MOBA_BLOCK,
                                                   il, n_even, kv_stacks)
    o_mb = moba_attn(seq3(qn), seq3(kb), vt, ksum.reshape(nb_p, seq // MOBA_BLOCK, MB_WIDTH))
    lyr = 2 * il
    yp = mix_ffn(xp.reshape(tp, d), [o_rw.reshape(tp, RW_WIDTH), o_mb.reshape(tp, MB_WIDTH)],
                 [w['w_out_even'][il][:RW_WIDTH], w['w_out_even'][il][RW_WIDTH:]],
                 w['norm_ffn'][lyr], w['ffn_w_gate'][lyr], w['ffn_w_up'][lyr], w['ffn_w_down'][lyr],
                 512, D_FF // 2).reshape(nb_p, seq, d)
    sh_p = p_rw.reshape(nb_p, seq, RW_IN)[:, -1]

    ps_rw, ps_mb = norm_matmul(xs.reshape(nb_s, d), w['norm_mix'][2 * il], w['w_in_even'][il],
                               (RW_IN, 3 * MB_WIDTH), nb_s)
    vres = (vf_s,) + vres_w if has_vres else None
    prep = rwkv_prep(ps_rw, state_shift[il], rw_params, vres, False, 1, nb_s)
    if not has_vres:
        vf_s = prep[7]
    os_rw, wkv_s = rwkv_step(*prep[:7], *ln, state_rwkv[il])
    qs, ks, vs, _, _, _ = moba_prep(ps_mb, w['mb_q_norm'][il], w['mb_k_norm'][il], nb_s)
    ks, vs = ks[0], vs[0]
    os_mb = moba_decode(qs, ks, vs, cache_k, cache_v, page_table, il)
    ys = mix_ffn(xs.reshape(nb_s, d), [os_rw, os_mb],
                 [w['w_out_even'][il][:RW_WIDTH], w['w_out_even'][il][RW_WIDTH:]],
                 w['norm_ffn'][lyr], w['ffn_w_gate'][lyr], w['ffn_w_up'][lyr], w['ffn_w_down'][lyr],
                 nb_s, D_FF // 2).reshape(nb_s, 1, d)
    mk_s = ks.reshape(nb_s, 1, MB_HEADS, MB_HEAD)
    mv_s = vs.reshape(nb_s, 1, MB_HEADS, MB_HEAD)
    return yp, ys, vf_p, vf_s, (k_stack, v_stack), (mk_s, mv_s, wkv_p, wkv_s, sh_p, ps_rw)


def _odd_layer(xp, xs, il, w, state_hgrn):
    nb_p, seq, d = xp.shape
    nb_s = xs.shape[0]
    tp = nb_p * seq
    lyr = 2 * il + 1
    (pp,) = norm_matmul(xp.reshape(tp, d), w['norm_mix'][lyr], w['w_in_odd'][il], (4 * d,), 512)
    o_p, hg_p = hgrn_scan(pp.reshape(nb_p, seq, 4 * d), w['hg_lb_logits'], w['hg_out_norm'][il],
                          jnp.zeros((nb_p, HG_HEADS, HG_D, HG_D), F32), il, 2048, HG_CHUNK)
    yp = mix_ffn(xp.reshape(tp, d), [o_p.reshape(tp, d)], [w['w_out_odd'][il]],
                 w['norm_ffn'][lyr], w['ffn_w_gate'][lyr], w['ffn_w_up'][lyr], w['ffn_w_down'][lyr],
                 512, D_FF // 2).reshape(nb_p, seq, d)
    (ps,) = norm_matmul(xs.reshape(nb_s, d), w['norm_mix'][lyr], w['w_in_odd'][il], (4 * d,), nb_s)
    o_s, hg_s = hgrn_step(ps, w['hg_lb_logits'], w['hg_out_norm'][il], state_hgrn[il], il)
    ys = mix_ffn(xs.reshape(nb_s, d), [o_s], [w['w_out_odd'][il]],
                 w['norm_ffn'][lyr], w['ffn_w_gate'][lyr], w['ffn_w_up'][lyr], w['ffn_w_down'][lyr],
                 nb_s, D_FF // 2).reshape(nb_s, 1, d)
    return yp, ys, hg_p, hg_s


def kernel(x_prompt, x_sample, cache_moba_k, cache_moba_v, page_table, state_rwkv, state_rwkv_shift, state_hgrn, norm_mix, norm_ffn, w_in_even, w_out_even, rw_mu, rw_w0, rw_w2, rw_a0, rw_a2, rw_g2, rw_k_k, rw_k_a, rw_r_k, rw_ln_w, rw_ln_b, rw_v0, rw_v1, rw_v2, mb_q_norm, mb_k_norm, w_in_odd, w_out_odd, hg_lb_logits, hg_out_norm, ffn_w_gate, ffn_w_up, ffn_w_down):
    depth = norm_mix.shape[0]
    zeros_lora = jnp.zeros_like(rw_w2)
    w = dict(
        norm_mix=norm_mix, norm_ffn=norm_ffn,
        w_in_even=w_in_even.astype(BF16), w_out_even=w_out_even.astype(BF16),
        w_in_odd=w_in_odd.astype(BF16), w_out_odd=w_out_odd.astype(BF16),
        ffn_w_gate=ffn_w_gate.astype(BF16), ffn_w_up=ffn_w_up.astype(BF16),
        ffn_w_down=ffn_w_down.astype(BF16),
        rw_mu=rw_mu, rw_w0=rw_w0, rw_a0=rw_a0, rw_g2=rw_g2, rw_k_k=rw_k_k, rw_k_a=rw_k_a,
        rw_r_k=rw_r_k, rw_ln_w=rw_ln_w, rw_ln_b=rw_ln_b, rw_v0=rw_v0, rw_v1=rw_v1, rw_v2=rw_v2,
        w2p=jnp.concatenate([rw_w2, zeros_lora], axis=1), a2p=jnp.concatenate([zeros_lora, rw_a2], axis=1),
        mb_q_norm=mb_q_norm, mb_k_norm=mb_k_norm, hg_lb_logits=hg_lb_logits, hg_out_norm=hg_out_norm)
    page_table = page_table.astype(jnp.int32)
    cache_k = jnp.transpose(cache_moba_k, (0, 1, 3, 4, 2))
    cache_v = jnp.transpose(cache_moba_v, (0, 1, 3, 4, 2))

    xp, xs = x_prompt, x_sample
    vf_p = vf_s = kv_stacks = None
    n_even = w_in_even.shape[0]
    even_out, hg_out = [], []
    for layer in range(depth):
        il = layer // 2
        if layer % 2 == 0:
            xp, xs, vf_p, vf_s, kv_stacks, outs = _even_layer(xp, xs, il, n_even, vf_p, vf_s, kv_stacks, w, cache_k,
                                                              cache_v, page_table, state_rwkv, state_rwkv_shift)
            even_out.append(outs)
        else:
            xp, xs, hg_p, hg_s = _odd_layer(xp, xs, il, w, state_hgrn)
            hg_out.append((hg_p, hg_s))
    stack = lambda i: jnp.stack([o[i] for o in even_out])
    nb_p, seq, _ = x_prompt.shape
    mk_p, mv_p = (a.reshape(n_even, nb_p, seq, MB_HEADS, MB_HEAD) for a in kv_stacks)
    return (xp, xs, mk_p, mv_p, stack(0), stack(1), stack(2), stack(3), stack(4), stack(5),
            jnp.stack([o[0] for o in hg_out]), jnp.stack([o[1] for o in hg_out]))
```

```python
import functools

import numpy as np
import jax
import jax.numpy as jnp
from jax import lax
from jax.experimental import pallas as pl
from jax.experimental.pallas import tpu as pltpu

F32 = jnp.float32
BF16 = jnp.bfloat16
HI = lax.Precision.HIGHEST

D_MODEL = 1024
PAGE_SIZE = 128
RW_HEAD = 64
RW_WIDTH = 512
RW_HEADS = 8
RW_IN = 1792
MB_HEAD = 64
MB_WIDTH = 512
MB_HEADS = 8
MOBA_BLOCK = 256
MOBA_TOPK = 3
HG_HEADS = 8
HG_D = 128
D_FF = 2816
RMS_EPS = 1e-6
GN_EPS = 64e-5
MASK_NEG = -1e30
LOG2E = 1.4426950408889634

LANES = 128
VMEM_LIMIT = 56 * 1024 * 1024
RW_CHUNK = 64
HG_CHUNK = 64
HG_WIDE_SEG = 16


def _cparams(*sem):
    return pltpu.CompilerParams(dimension_semantics=sem, vmem_limit_bytes=VMEM_LIMIT)


def _bf(x):
    return x.astype(BF16)


def _dot(a, b, precision=None):
    return jnp.dot(a, b, precision=precision, preferred_element_type=F32)


def _dot_nt(a, b, precision=None):
    return lax.dot_general(a, b, (((1,), (1,)), ((), ())), precision=precision,
                           preferred_element_type=F32)


def _dot_tn(a, b, precision=None):
    return lax.dot_general(a, b, (((0,), (0,)), ((), ())), precision=precision,
                           preferred_element_type=F32)


def _bmm(a, b):
    return lax.dot_general(_bf(a), _bf(b), (((2,), (1,)), ((0,), (0,))), preferred_element_type=F32)


def _bmm_nt(a, b):
    return lax.dot_general(_bf(a), _bf(b), (((2,), (2,)), ((0,), (0,))), preferred_element_type=F32)


def _bmm_tn(a, b):
    return lax.dot_general(_bf(a), _bf(b), (((1,), (1,)), ((0,), (0,))), preferred_element_type=F32)


def _split_dot(a_bf16, b):
    n = b.shape[1]
    hi = b.astype(BF16)
    lo = (b - hi.astype(F32)).astype(BF16)
    out = _dot(a_bf16, jnp.concatenate([hi, lo], axis=1))
    return out[:, :n] + out[:, n:]


def _hi_lo(x):
    hi = x.astype(BF16)
    return hi, (x - hi.astype(F32)).astype(BF16)


def _dot_x3(a, b):
    a_hi, a_lo = _hi_lo(a)
    b_hi, b_lo = _hi_lo(b)
    return _dot(a_hi, b_hi) + (_dot(a_hi, b_lo) + _dot(a_lo, b_hi))


def _dot_x3_nt(a, b):
    a_hi, a_lo = _hi_lo(a)
    b_hi, b_lo = _hi_lo(b)
    return _dot_nt(a_hi, b_hi) + (_dot_nt(a_hi, b_lo) + _dot_nt(a_lo, b_hi))


def _dot_x2(a, g_bf16):
    a_hi, a_lo = _hi_lo(a)
    return _dot(a_hi, g_bf16) + _dot(a_lo, g_bf16)


def _sigmoid(x):
    return 1.0 / (1.0 + jnp.exp(-x))


def _group_matrices(width, group):
    g = np.zeros((width, LANES), np.float32)
    g[np.arange(width), np.arange(width) // group] = 1.0
    return jnp.asarray(g, dtype=BF16), jnp.asarray(g.T.copy(), dtype=BF16)


def _norm_matmul_kernel(x_ref, gain_ref, w_ref, *out_refs, widths, tn):
    x = x_ref[...]
    ms = jnp.mean(x * x, axis=-1, keepdims=True)
    h = (x * lax.rsqrt(ms + RMS_EPS) * gain_ref[...]).astype(BF16)
    col = 0
    for o_ref, width in zip(out_refs, widths):
        for c0 in range(0, width, tn):
            sz = min(tn, width - c0)
            o_ref[:, c0:c0 + sz] = _dot(h, w_ref[:, col + c0:col + c0 + sz])
        col += width


def norm_matmul(x, gain, w_bf16, widths, tm):
    m, d = x.shape
    n = w_bf16.shape[1]
    assert sum(widths) == n and m % tm == 0
    kern = functools.partial(_norm_matmul_kernel, widths=tuple(widths), tn=512)
    return pl.pallas_call(
        kern,
        grid=(m // tm,),
        in_specs=[pl.BlockSpec((tm, d), lambda i: (i, 0)),
                  pl.BlockSpec((1, d), lambda i: (0, 0)),
                  pl.BlockSpec((d, n), lambda i: (0, 0))],
        out_specs=[pl.BlockSpec((tm, wd), lambda i: (i, 0)) for wd in widths],
        out_shape=[jax.ShapeDtypeStruct((m, wd), F32) for wd in widths],
        compiler_params=_cparams("parallel"),
    )(x, gain.reshape(1, d), w_bf16)


def _mix_ffn_kernel(*refs, n_act):
    x_ref = refs[0]
    a_refs = refs[1:1 + n_act]
    wo_refs = refs[1 + n_act:1 + 2 * n_act]
    gain_ref, wg_ref, wu_ref, wd_ref, y_ref, x1_sc, h_sc, acc_sc = refs[1 + 2 * n_act:]
    j = pl.program_id(1)

    @pl.when(j == 0)
    def _():
        x1 = x_ref[...]
        for a_ref, wo_ref in zip(a_refs, wo_refs):
            x1 = x1 + _dot(a_ref[...].astype(BF16), wo_ref[...])
        x1_sc[...] = x1
        ms = jnp.mean(x1 * x1, axis=-1, keepdims=True)
        h_sc[...] = (x1 * lax.rsqrt(ms + RMS_EPS) * gain_ref[...]).astype(BF16)
        acc_sc[...] = jnp.zeros_like(acc_sc)

    h = h_sc[...]
    g = _dot(h, wg_ref[...])
    u = _dot(h, wu_ref[...])
    act = (g * _sigmoid(g) * u).astype(BF16)
    acc_sc[...] += _dot(act, wd_ref[...])

    @pl.when(j == pl.num_programs(1) - 1)
    def _():
        y_ref[...] = x1_sc[...] + acc_sc[...]


def mix_ffn(x, acts, wos_bf16, gain, wg, wu, wd, tm, tf):
    m, d = x.shape
    f = wg.shape[1]
    assert m % tm == 0 and f % tf == 0
    n_act = len(acts)
    in_specs = [pl.BlockSpec((tm, d), lambda i, j: (i, 0))]
    in_specs += [pl.BlockSpec((tm, a.shape[1]), lambda i, j: (i, 0)) for a in acts]
    in_specs += [pl.BlockSpec(w.shape, lambda i, j: (0, 0)) for w in wos_bf16]
    in_specs += [pl.BlockSpec((1, d), lambda i, j: (0, 0)),
                 pl.BlockSpec((d, tf), lambda i, j: (0, j)),
                 pl.BlockSpec((d, tf), lambda i, j: (0, j)),
                 pl.BlockSpec((tf, d), lambda i, j: (j, 0))]
    return pl.pallas_call(
        functools.partial(_mix_ffn_kernel, n_act=n_act),
        grid=(m // tm, f // tf),
        in_specs=in_specs,
        out_specs=pl.BlockSpec((tm, d), lambda i, j: (i, 0)),
        out_shape=jax.ShapeDtypeStruct((m, d), F32),
        scratch_shapes=[pltpu.VMEM((tm, d), F32), pltpu.VMEM((tm, d), BF16), pltpu.VMEM((tm, d), F32)],
        compiler_params=_cparams("parallel", "arbitrary"),
    )(x, *acts, *wos_bf16, gain.reshape(1, d), wg, wu, wd)


def _rwkv_prep_kernel(*refs, shift_mode, has_vres, tiles_per_seq):
    it = iter(refs)
    p_ref = next(it)
    prev_ref = next(it)
    mu_ref, w0_ref, a0_ref, kk_ref, ka_ref, w2_ref, a2_ref, g2_ref, grp_ref, grpt_ref = (
        next(it) for _ in range(10))
    if has_vres:
        vf_ref, v0_ref, v1_ref, v2_ref = (next(it) for _ in range(4))
    r_o, lw_o, k_o, v_o, kkn_o, b_o, g_o = (next(it) for _ in range(7))
    if not has_vres:
        vf_o = next(it)

    pf = p_ref[...]
    if shift_mode:
        first = (pl.program_id(0) % tiles_per_seq) == 0
        prev_row = jnp.where(first, 0.0, prev_ref[7:8, :])
        rows = lax.broadcasted_iota(jnp.int32, pf.shape, 0)
        prev = jnp.where(rows == 0, prev_row, pltpu.roll(pf, 1, axis=0))
    else:
        prev = prev_ref[...]
    xs = pf + (prev - pf) * mu_ref[...]
    r = xs[:, 0:RW_WIDTH]
    k = xs[:, RW_WIDTH:2 * RW_WIDTH]
    v = xs[:, 2 * RW_WIDTH:3 * RW_WIDTH]
    wa_lo = xs[:, 3 * RW_WIDTH:3 * RW_WIDTH + LANES]
    g_lo = xs[:, 3 * RW_WIDTH + LANES:RW_IN]
    z = w0_ref[...] + _dot_x3(jnp.tanh(wa_lo), w2_ref[...])
    sp = jnp.maximum(-z, 0.0) + jnp.log(1.0 + jnp.exp(-jnp.abs(z)))
    logw = -jnp.exp(-sp - 0.5)
    a = _sigmoid(a0_ref[...] + _dot_x3(wa_lo, a2_ref[...]))
    g = _dot_x3(_sigmoid(g_lo), g2_ref[...])
    if has_vres:
        gate = _sigmoid(v0_ref[...] + _dot_x3(_dot_x3(v, v1_ref[...]), v2_ref[...]))
        v = v + (vf_ref[...] - v) * gate
    else:
        vf_o[...] = v
    kk = k * kk_ref[...]
    ss = _dot_x2(kk * kk, grp_ref[...])
    inv = _dot_x2(lax.rsqrt(jnp.maximum(ss, 1e-24)), grpt_ref[...])
    kk = kk * inv
    k = k * (1.0 + (a - 1.0) * ka_ref[...])
    r_o[...] = r
    lw_o[...] = logw
    k_o[...] = k
    v_o[...] = v
    kkn_o[...] = kk
    b_o[...] = kk * a
    g_o[...] = g


def rwkv_prep(p_rw, prev, params, vres, shift_mode, seq_len, tm):
    t = p_rw.shape[0]
    assert t % tm == 0
    mu, w0, w2p, a0, a2p, g2, k_k, k_a = params
    grp, grpt = _group_matrices(RW_WIDTH, RW_HEAD)
    row = lambda a: a.reshape(1, -1)
    tile = lambda wd: pl.BlockSpec((tm, wd), lambda i: (i, 0))
    full = lambda a: pl.BlockSpec(a.shape, lambda i: (0,) * a.ndim)
    args = [p_rw]
    specs = [tile(RW_IN)]
    if shift_mode:
        r8 = tm // 8
        args.append(p_rw)
        specs.append(pl.BlockSpec((8, RW_IN), lambda i: (jnp.maximum(i * r8 - 1, 0), 0)))
    else:
        args.append(prev)
        specs.append(tile(RW_IN))
    small = [row(mu), row(w0), row(a0), row(k_k), row(k_a), w2p, a2p, g2, grp, grpt]
    args += small
    specs += [full(a) for a in small]
    has_vres = vres is not None
    if has_vres:
        vf, v0, v1, v2 = vres
        extra = [row(v0), v1, v2]
        args += [vf] + extra
        specs += [tile(RW_WIDTH)] + [full(a) for a in extra]
    n_out = 7 if has_vres else 8
    kern = functools.partial(_rwkv_prep_kernel, shift_mode=shift_mode, has_vres=has_vres,
                             tiles_per_seq=max(seq_len // tm, 1))
    outs = pl.pallas_call(
        kern,
        grid=(t // tm,),
        in_specs=specs,
        out_specs=[tile(RW_WIDTH)] * n_out,
        out_shape=[jax.ShapeDtypeStruct((t, RW_WIDTH), F32)] * n_out,
        compiler_params=_cparams("parallel"),
    )(*args)
    return outs


def _unit_lower_inverse(n, c):
    ri = lax.broadcasted_iota(jnp.int32, (c, c), 0)
    ci = lax.broadcasted_iota(jnp.int32, (c, c), 1)
    eye = (ri == ci).astype(F32)[None]
    blk = min(16, c)
    same = ((ri // blk) == (ci // blk))[None]
    nd = jnp.where(same, n, 0.0)
    x = eye - nd
    pw = nd
    span = 2
    while span < blk:
        pw = _bmm(pw, pw)
        x = _bmm(x, eye + pw)
        span *= 2
    if c > blk:
        p = _bmm(x, jnp.where(same, 0.0, n))
        y = eye - p
        pw = p
        span = 2
        while span < c // blk:
            pw = _bmm(pw, pw)
            y = _bmm(y, eye + pw)
            span *= 2
        x = _bmm(y, x)
    return x


def _rwkv_scan_kernel(r_ref, lw_ref, k_ref, v_ref, kk_ref, b_ref, g_ref, lnw_ref, lnb_ref, rk_ref,
                      s0_ref, tri_ref, o_ref, sout_ref, s_sc, *, chunk):
    l = pl.program_id(2)
    tl = r_ref.shape[1]
    c = chunk
    nc = tl // c
    hd = RW_HEAD
    n_heads = LANES // hd

    @pl.when(l == 0)
    def _():
        s_sc[...] = jnp.zeros_like(s_sc)
        for h in range(n_heads):
            s_sc[h * hd:(h + 1) * hd, h * hd:(h + 1) * hd] = s0_ref[0, h]

    ri = lax.broadcasted_iota(jnp.int32, (c, c), 0)
    ci = lax.broadcasted_iota(jnp.int32, (c, c), 1)
    tri_incl = (ri >= ci)[None]
    tri_strict = (ri > ci)[None]
    kr = lax.broadcasted_iota(jnp.int32, (LANES, LANES), 0)
    kc = lax.broadcasted_iota(jnp.int32, (LANES, LANES), 1)
    same_head = (kr // hd) == (kc // hd)
    head_of_lane = lax.broadcasted_iota(jnp.int32, (1, 1, LANES), 2) // hd
    to3 = lambda x: x.reshape(nc, c, LANES)

    lw = to3(lw_ref[0])
    lc = to3(_split_dot(tri_ref[...], lw_ref[0]))
    tot = lc[:, c - 1:c, :]
    r, k, v, kk, b = to3(r_ref[0]), to3(k_ref[0]), to3(v_ref[0]), to3(kk_ref[0]), to3(b_ref[0])
    e_neg = jnp.exp(-lc)
    e_end = jnp.exp(tot - lc)
    kk_t = kk * jnp.exp(lc - lw)
    r_t = r * jnp.exp(lc)
    b_t = b * e_neg
    k_t = k * e_neg
    wm = u0 = q_eff = o0 = None
    for h in range(n_heads):
        mine = head_of_lane == h
        lhs = jnp.concatenate([jnp.where(mine, kk_t, 0.0), jnp.where(mine, r_t, 0.0)], axis=1)
        g_b = _bmm_nt(lhs, b_t)
        g_k = _bmm_nt(lhs, k_t)
        a_ab = jnp.where(tri_strict, g_b[:, :c], 0.0)
        a_rb = jnp.where(tri_incl, g_b[:, c:], 0.0)
        a_ak = jnp.where(tri_strict, g_k[:, :c], 0.0)
        a_rk = jnp.where(tri_incl, g_k[:, c:], 0.0)
        tinv = _unit_lower_inverse(a_ab, c)
        wu = _bmm(tinv, jnp.concatenate([kk_t, _bmm(a_ak, v)], axis=2))
        wm_h, u0_h = wu[:, :, :LANES], -wu[:, :, LANES:]
        ru = _bmm(a_rb, jnp.concatenate([wm_h, u0_h], axis=2))
        qe_h = r_t - ru[:, :, :LANES]
        o0_h = _bmm(a_rk, v) + ru[:, :, LANES:]
        if h == 0:
            wm, u0, q_eff, o0 = wm_h, u0_h, qe_h, o0_h
        else:
            wm, u0 = jnp.where(mine, wm_h, wm), jnp.where(mine, u0_h, u0)
            q_eff, o0 = jnp.where(mine, qe_h, q_eff), jnp.where(mine, o0_h, o0)
    eye = (kr == kc).astype(F32)[None]
    m_mat = jnp.where(same_head[None], eye * jnp.exp(tot) - _bmm_tn(wm, b * e_end), 0.0)
    d_mat = jnp.where(same_head[None],
                      _bmm_tn(jnp.concatenate([u0, v], axis=1),
                              jnp.concatenate([b * e_end, k * e_end], axis=1)), 0.0)
    s = s_sc[...]
    states = []
    for i in range(nc):
        states.append(s)
        s = _dot(_bf(s), _bf(m_mat[i])) + d_mat[i]
    s_sc[...] = s
    o = (o0 + _bmm_nt(q_eff, jnp.stack(states))).reshape(tl, LANES)
    head_sum = _bf(jnp.where(same_head, 1.0, 0.0))
    mean = _dot_x2(o, head_sum) * (1.0 / hd)
    var = _dot_x2(jnp.square(o - mean), head_sum) * (1.0 / hd)
    o = (o - mean) * lax.rsqrt(var + GN_EPS) * lnw_ref[...] + lnb_ref[...]
    bonus = _dot_x2(r_ref[0] * k_ref[0] * rk_ref[...], head_sum) * v_ref[0]
    o_ref[0] = (o + bonus) * g_ref[0]

    @pl.when(l == pl.num_programs(2) - 1)
    def _():
        for h in range(n_heads):
            sout_ref[0, h] = s[h * hd:(h + 1) * hd, h * hd:(h + 1) * hd]


def rwkv_scan(r, lw, k, v, kk, b, g, ln_w, ln_b, r_k, s0, tl, chunk):
    bsz, seq, _ = r.shape
    assert seq % tl == 0 and tl % chunk == 0
    hp = RW_WIDTH // LANES
    per = LANES // RW_HEAD
    seq_spec = pl.BlockSpec((1, tl, LANES), lambda bi, hi, li: (bi, li, hi))
    par_spec = pl.BlockSpec((1, LANES), lambda bi, hi, li: (0, hi))
    st_spec = pl.BlockSpec((1, per, RW_HEAD, RW_HEAD), lambda bi, hi, li: (bi, hi, 0, 0))
    row = lambda a: a.reshape(1, RW_WIDTH)
    idx = np.arange(tl)
    tri = jnp.asarray(((idx[:, None] >= idx[None, :]) & (idx[:, None] // chunk == idx[None, :] // chunk)),
                      dtype=BF16)
    return pl.pallas_call(
        functools.partial(_rwkv_scan_kernel, chunk=chunk),
        grid=(bsz, hp, seq // tl),
        in_specs=[seq_spec] * 7 + [par_spec] * 3 + [st_spec, pl.BlockSpec((tl, tl), lambda bi, hi, li: (0, 0))],
        out_specs=[seq_spec, st_spec],
        out_shape=[jax.ShapeDtypeStruct((bsz, seq, RW_WIDTH), F32),
                   jax.ShapeDtypeStruct((bsz, RW_HEADS, RW_HEAD, RW_HEAD), F32)],
        scratch_shapes=[pltpu.VMEM((LANES, LANES), F32)],
        compiler_params=_cparams("parallel", "parallel", "arbitrary"),
    )(r, lw, k, v, kk, b, g, row(ln_w), row(ln_b), row(r_k), s0, tri)


def _moba_prep_kernel(p_ref, qg_ref, kg_ref, grp_ref, grpt_ref, *refs):
    q_o, k_o, v_o, kb_o, vt_o, ks_o = refs[-6:]
    p = p_ref[...]
    q = p[:, 0:MB_WIDTH]
    k = p[:, MB_WIDTH:2 * MB_WIDTH]
    v = p[:, 2 * MB_WIDTH:3 * MB_WIDTH]

    def head_norm(x, gain_ref):
        ms = _dot_x2(x * x, grp_ref[...]) * (1.0 / MB_HEAD)
        inv = _dot_x2(lax.rsqrt(ms + RMS_EPS), grpt_ref[...])
        return x * inv * gain_ref[...]

    qn = head_norm(q, qg_ref)
    kn = head_norm(k, kg_ref)
    q_o[...] = qn
    v_t = v.T
    k_o[0, 0] = kn.T
    v_o[0, 0] = v_t
    kb_o[...] = kn.astype(BF16)
    vt_o[...] = v_t.astype(BF16)
    ks_o[0] = jnp.sum(kn, axis=0, keepdims=True)


def moba_prep(p_mb, q_gain, k_gain, tm, seq, layer=0, n_layers=1, stacks=None):
    t = p_mb.shape[0]
    assert t % seq == 0 and seq % tm == 0
    tps = seq // tm
    grp, grpt = _group_matrices(MB_WIDTH, MB_HEAD)
    tile = lambda i: (i, 0)
    full = lambda a: pl.BlockSpec(a.shape, lambda i: (0,) * a.ndim)
    qg = jnp.tile(q_gain, MB_HEADS).reshape(1, MB_WIDTH)
    kg = jnp.tile(k_gain, MB_HEADS).reshape(1, MB_WIDTH)
    args = [p_mb, qg, kg, grp, grpt]
    in_specs = [pl.BlockSpec((tm, 3 * MB_WIDTH), tile), full(qg), full(kg), full(grp), full(grpt)]
    aliases = {}
    if stacks is not None:
        aliases = {len(args): 1, len(args) + 1: 2}
        args += list(stacks)
        in_specs += [pl.BlockSpec(memory_space=pl.ANY)] * 2
    slab = pl.BlockSpec((1, 1, MB_WIDTH, tm), lambda i: (layer, i // tps, 0, i % tps))
    stack_shape = jax.ShapeDtypeStruct((n_layers, t // seq, MB_WIDTH, seq), F32)
    return pl.pallas_call(
        _moba_prep_kernel,
        grid=(t // tm,),
        in_specs=in_specs,
        out_specs=[pl.BlockSpec((tm, MB_WIDTH), tile), slab, slab, pl.BlockSpec((tm, MB_WIDTH), tile),
                   pl.BlockSpec((MB_WIDTH, tm), lambda i: (0, i)),
                   pl.BlockSpec((1, 1, MB_WIDTH), lambda i: (i, 0, 0))],
        out_shape=[jax.ShapeDtypeStruct((t, MB_WIDTH), F32), stack_shape, stack_shape,
                   jax.ShapeDtypeStruct((t, MB_WIDTH), BF16), jax.ShapeDtypeStruct((MB_WIDTH, t), BF16),
                   jax.ShapeDtypeStruct((t // tm, 1, MB_WIDTH), F32)],
        input_output_aliases=aliases,
        compiler_params=_cparams("parallel"),
    )(*args)


def _top_blocks(gate, idx, axis):
    sel = jnp.zeros_like(gate)
    rem = gate
    for _ in range(MOBA_TOPK):
        mx = jnp.max(rem, axis=axis, keepdims=True)
        live = jnp.logical_and(rem == mx, rem > 0.5 * MASK_NEG)
        first = jnp.min(jnp.where(live, idx, 1e9), axis=axis, keepdims=True)
        pick = idx == first
        sel = jnp.where(pick, 1.0, sel)
        rem = jnp.where(pick, MASK_NEG, rem)
    return sel


def _moba_attn_kernel(q_ref, k_ref, vt_ref, ks_ref, o_ref, s_a, s_b, p_a, p_b):
    qi = pl.program_id(2)
    tq = q_ref.shape[1]
    blk = MOBA_BLOCK
    assert tq == 2 * blk
    nb = ks_ref.shape[1]
    n_heads = LANES // MB_HEAD
    ncol = n_heads * tq
    q = q_ref[0]
    lane = lax.broadcasted_iota(jnp.int32, (1, LANES), 1)
    bmean = ks_ref[0] * (1.0 / MOBA_BLOCK)
    blk_f = lax.broadcasted_iota(jnp.int32, (nb, tq), 0).astype(F32)
    second = lax.broadcasted_iota(jnp.int32, (1, tq), 1) >= blk
    own_f = (2 * qi).astype(F32) + jnp.where(second, 1.0, 0.0)
    q_cols, bias_cols, own_bias = [], [], []
    pad_rows = jnp.full((LANES - nb, tq), MASK_NEG, F32)
    first_blk = lax.broadcasted_iota(jnp.int32, (nb, tq), 0) == 2 * qi
    for h in range(n_heads):
        q_h = jnp.where((lane // MB_HEAD) == h, q, 0.0)
        gate_t = _dot_x3_nt(bmean, q_h)
        gate_t = jnp.where(blk_f < own_f, gate_t, MASK_NEG)
        bias = (1.0 - _top_blocks(gate_t, blk_f, 0)) * MASK_NEG
        own_bias.append(jnp.max(jnp.where(first_blk, bias, MASK_NEG), axis=0, keepdims=True))
        bias_cols.append(jnp.concatenate([bias, pad_rows], axis=0).T)
        q_cols.append(q_h * (LOG2E * MB_HEAD ** -0.5))
    q_all = _bf(jnp.concatenate(q_cols, axis=0))
    q_ext = jnp.concatenate([q_all, _bf(jnp.concatenate(bias_cols, axis=0))], axis=1)

    def ones_rows(n):
        return _bf(jnp.where(lax.broadcasted_iota(jnp.int32, (16, n), 0) == 0, 1.0, 0.0))

    def values(r0, p):
        out = []
        for h in range(n_heads):
            vt_h = jnp.concatenate([vt_ref[h * MB_HEAD:(h + 1) * MB_HEAD, pl.ds(r0, tq)], ones_rows(tq)], axis=0)
            out.append(_dot(vt_h, p[:, h * tq:(h + 1) * tq]))
        return jnp.concatenate(out, axis=1)

    row0 = pl.multiple_of(qi * tq, tq)
    ri = lax.broadcasted_iota(jnp.int32, (tq, ncol), 0)
    cq = lax.broadcasted_iota(jnp.int32, (tq, ncol), 1) % tq
    gated = jnp.logical_and(ri < blk, cq >= blk)
    s = _dot_nt(k_ref[0, pl.ds(row0, tq), :], q_all) + jnp.where(gated, jnp.concatenate(own_bias, axis=1), 0.0)
    s = jnp.where(ri <= cq, s, MASK_NEG)
    m = jnp.max(s, axis=0, keepdims=True)
    acc = values(row0, _bf(jnp.exp2(s - m)))

    lane_i = lax.broadcasted_iota(jnp.int32, (1, LANES), 1)

    def pair_rows(i):
        return pl.multiple_of(jnp.minimum(i, nb // 2 - 1) * tq, tq)

    def scores(i, s_ref):
        k2 = k_ref[0, pl.ds(pair_rows(i), tq), :]
        onehot = [jnp.broadcast_to(_bf(jnp.where(lane_i == jnp.where(i < qi, 2 * i + u, LANES - 1), 1.0, 0.0)),
                                   (blk, LANES)) for u in range(2)]
        k_ext = jnp.concatenate([k2, jnp.concatenate(onehot, axis=0)], axis=1)
        s_ref[...] = _dot_nt(k_ext, q_ext)

    def softmax(s_ref, p_ref, m):
        s = s_ref[...]
        m_new = jnp.maximum(m, jnp.max(s, axis=0, keepdims=True))
        p_ref[...] = _bf(jnp.exp2(s - m_new))
        return m_new, jnp.exp2(m - m_new)

    scores(0, s_a)
    p_b[...] = jnp.zeros_like(p_b)

    def body(t, carry):
        m, acc = carry
        scores(2 * t + 1, s_b)
        pv = values(pair_rows(jnp.maximum(2 * t - 1, 0)), p_b[...])
        m, alpha = softmax(s_a, p_a, m)
        acc = (acc + pv) * alpha
        scores(2 * t + 2, s_a)
        pv = values(pair_rows(2 * t), p_a[...])
        m, alpha = softmax(s_b, p_b, m)
        acc = (acc + pv) * alpha
        return m, acc

    trips = (qi + 1) // 2
    m, acc = lax.fori_loop(0, trips, body, (m, acc))
    acc = acc + values(pair_rows(jnp.maximum(2 * trips - 1, 0)), p_b[...])
    o = acc[:MB_HEAD] / acc[MB_HEAD:MB_HEAD + 1]
    o_ref[0] = jnp.concatenate([o[:, h * tq:(h + 1) * tq] for h in range(n_heads)], axis=0).T


def moba_attn(qn, kb, vt, ksum):
    bsz, seq, _ = qn.shape
    tq = 2 * MOBA_BLOCK
    nb = seq // MOBA_BLOCK
    assert seq % tq == 0 and nb <= LANES
    hp = MB_WIDTH // LANES
    ncol = (LANES // MB_HEAD) * tq
    return pl.pallas_call(
        _moba_attn_kernel,
        grid=(bsz, hp, seq // tq),
        in_specs=[pl.BlockSpec((1, tq, LANES), lambda b, h, i: (b, i, h)),
                  pl.BlockSpec((1, seq, LANES), lambda b, h, i: (b, 0, h)),
                  pl.BlockSpec((LANES, seq), lambda b, h, i: (h, b)),
                  pl.BlockSpec((1, nb, LANES), lambda b, h, i: (b, 0, h))],
        out_specs=pl.BlockSpec((1, tq, LANES), lambda b, h, i: (b, i, h)),
        out_shape=jax.ShapeDtypeStruct((bsz, seq, MB_WIDTH), F32),
        scratch_shapes=[pltpu.VMEM((tq, ncol), F32), pltpu.VMEM((tq, ncol), F32),
                        pltpu.VMEM((tq, ncol), BF16), pltpu.VMEM((tq, ncol), BF16)],
        compiler_params=_cparams("parallel", "parallel", "arbitrary"),
    )(qn, kb, vt, ksum)


def _moba_scores_kernel(pt_ref, q_ref, *refs, n_pg):
    k_refs = refs[:n_pg]
    s_o = refs[n_pg]
    q = q_ref[0]
    for j in range(n_pg):
        s_o[0, j] = jnp.sum(k_refs[j][0, 0] * q, axis=1)


def _moba_select_kernel(s_ref, q_ref, kn_ref, p_o, w_o, id_o, *, ppb):
    scale = MB_HEAD ** -0.5
    s = s_ref[0]
    n_lp = s.shape[0]
    nblk = n_lp // ppb
    page_sum = jnp.sum(s, axis=-1, keepdims=True)
    gate = jnp.sum(page_sum.reshape(nblk, ppb, MB_HEADS, 1), axis=1) * (1.0 / MOBA_BLOCK)
    idx = lax.broadcasted_iota(jnp.int32, gate.shape, 0).astype(F32)
    lane = lax.broadcasted_iota(jnp.int32, (MB_HEADS, LANES), 1)
    sel = jnp.zeros_like(gate)
    rem = gate
    ids = jnp.zeros((MB_HEADS, LANES), F32)
    for t in range(MOBA_TOPK):
        mx = jnp.max(rem, axis=0, keepdims=True)
        first = jnp.min(jnp.where(rem == mx, idx, 1e9), axis=0, keepdims=True)
        pick = idx == first
        sel = jnp.where(pick, 1.0, sel)
        rem = jnp.where(pick, MASK_NEG, rem)
        for u in range(ppb):
            ids = jnp.where(lane == t * ppb + u, first[0] * ppb + u, ids)
    id_o[0] = ids.astype(jnp.int32)
    sel_pg = jnp.broadcast_to(sel[:, None], (nblk, ppb, MB_HEADS, 1)).reshape(n_lp, MB_HEADS, 1) > 0.5
    s_own = jnp.sum(kn_ref[0] * q_ref[0], axis=-1, keepdims=True) * scale
    sm = jnp.where(sel_pg, s * scale, MASK_NEG)
    m_all = jnp.maximum(jnp.max(jnp.max(sm, axis=0), axis=-1, keepdims=True), s_own)
    p = jnp.where(sel_pg, jnp.exp(sm - m_all[None]), 0.0)
    w_own = jnp.exp(s_own - m_all)
    inv = 1.0 / (jnp.sum(jnp.sum(p, axis=0), axis=-1, keepdims=True) + w_own)
    p_o[0] = p * inv[None]
    w_o[0] = jnp.broadcast_to(w_own * inv, (MB_HEADS, LANES))


def _moba_values_kernel(pt_ref, id_ref, p_ref, w_ref, vn_ref, *refs, n_sel, hps):
    v_refs = refs[:hps * n_sel]
    o_ref = refs[hps * n_sel]
    b = pl.program_id(0)
    for hh in range(hps):
        h = pl.program_id(1) * hps + hh
        acc = w_ref[0, pl.ds(h, 1), 0:1] * vn_ref[0, hh]
        for j in range(n_sel):
            page = id_ref[(b * MB_HEADS + h) * n_sel + j]
            prob = p_ref[0, page, pl.ds(h, 1), :]
            acc = acc + _dot_nt(prob, v_refs[hh * n_sel + j][0, 0, 0], HI)
        o_ref[0, hh] = acc


def moba_decode(qn, kn, vn, k_cache_t, v_cache_t, page_table, layer):
    bsz = qn.shape[0]
    n_lp = page_table.shape[1]
    ppb = MOBA_BLOCK // PAGE_SIZE
    assert n_lp % ppb == 0 and n_lp // ppb >= MOBA_TOPK
    n_pg = 16
    assert n_lp % n_pg == 0
    n_sel = MOBA_TOPK * ppb
    hps = MB_HEADS
    heads = lambda a: a.reshape(bsz, MB_HEADS, MB_HEAD)
    page_blk = (1, 1, MB_HEADS, MB_HEAD, PAGE_SIZE)

    scores = pl.pallas_call(
        functools.partial(_moba_scores_kernel, n_pg=n_pg),
        grid_spec=pltpu.PrefetchScalarGridSpec(
            num_scalar_prefetch=1,
            grid=(bsz, n_lp // n_pg),
            in_specs=[pl.BlockSpec((1, MB_HEADS, MB_HEAD, 1), lambda b, g, pt: (b, 0, 0, 0))]
                     + [pl.BlockSpec(page_blk, functools.partial(
                         lambda b, g, pt, j: (layer, pt[b, g * n_pg + j], 0, 0, 0), j=j)) for j in range(n_pg)],
            out_specs=pl.BlockSpec((1, n_pg, MB_HEADS, PAGE_SIZE), lambda b, g, pt: (b, g, 0, 0))),
        out_shape=jax.ShapeDtypeStruct((bsz, n_lp, MB_HEADS, PAGE_SIZE), F32),
        compiler_params=_cparams("parallel", "arbitrary"),
    )(page_table, qn.reshape(bsz, MB_HEADS, MB_HEAD, 1), *([k_cache_t] * n_pg))

    vec = pl.BlockSpec((1, MB_HEADS, MB_HEAD), lambda b: (b, 0, 0))
    probs, w_own, ids = pl.pallas_call(
        functools.partial(_moba_select_kernel, ppb=ppb),
        grid=(bsz,),
        in_specs=[pl.BlockSpec((1, n_lp, MB_HEADS, PAGE_SIZE), lambda b: (b, 0, 0, 0)), vec, vec],
        out_specs=[pl.BlockSpec((1, n_lp, MB_HEADS, PAGE_SIZE), lambda b: (b, 0, 0, 0)),
                   pl.BlockSpec((1, MB_HEADS, LANES), lambda b: (b, 0, 0)),
                   pl.BlockSpec((1, MB_HEADS, LANES), lambda b: (b, 0, 0))],
        out_shape=[jax.ShapeDtypeStruct((bsz, n_lp, MB_HEADS, PAGE_SIZE), F32),
                   jax.ShapeDtypeStruct((bsz, MB_HEADS, LANES), F32),
                   jax.ShapeDtypeStruct((bsz, MB_HEADS, LANES), jnp.int32)],
        compiler_params=_cparams("parallel"),
    )(scores, heads(qn), heads(kn))

    sel_ids = ids[:, :, :n_sel].reshape(-1)
    out = pl.pallas_call(
        functools.partial(_moba_values_kernel, n_sel=n_sel, hps=hps),
        grid_spec=pltpu.PrefetchScalarGridSpec(
            num_scalar_prefetch=2,
            grid=(bsz, MB_HEADS // hps),
            in_specs=[pl.BlockSpec((1, n_lp, MB_HEADS, PAGE_SIZE), lambda b, g, pt, sid: (b, 0, 0, 0)),
                      pl.BlockSpec((1, MB_HEADS, LANES), lambda b, g, pt, sid: (b, 0, 0)),
                      pl.BlockSpec((1, hps, 1, MB_HEAD), lambda b, g, pt, sid: (b, g, 0, 0))]
                     + [pl.BlockSpec((1, 1, 1, MB_HEAD, PAGE_SIZE), functools.partial(
                         lambda b, g, pt, sid, hh, j: (
                             layer, pt[b, sid[(b * MB_HEADS + g * hps + hh) * n_sel + j]], g * hps + hh, 0, 0),
                         hh=hh, j=j)) for hh in range(hps) for j in range(n_sel)],
            out_specs=pl.BlockSpec((1, hps, 1, MB_HEAD), lambda b, g, pt, sid: (b, g, 0, 0))),
        out_shape=jax.ShapeDtypeStruct((bsz, MB_HEADS, 1, MB_HEAD), F32),
        compiler_params=_cparams("parallel", "arbitrary"),
    )(page_table, sel_ids, probs, w_own, vn.reshape(bsz, MB_HEADS, 1, MB_HEAD), *([v_cache_t] * (hps * n_sel)))
    return out.reshape(bsz, MB_WIDTH)


def _hgrn_constants(c):
    rows = np.arange(c)
    dmats, ups, los, masks = [], [], [], []
    z = c
    while z >= 2:
        seg = rows // z
        mid = seg * z + z // 2
        up = rows >= mid
        d = np.zeros((c, c), np.float32)
        for i in rows:
            if up[i]:
                d[i, mid[i]:i + 1] = 1.0
            else:
                d[i, i + 1:mid[i]] = 1.0
        dmats.append(d)
        ups.append(np.repeat(up[:, None], HG_D, 1).astype(np.float32))
        los.append(np.repeat(~up[:, None], HG_D, 1).astype(np.float32))
        masks.append(((seg[:, None] == seg[None, :]) & up[:, None] & (~up)[None, :]).astype(np.float32))
        z //= 2
    tri = np.tril(np.ones((c, c), np.float32))
    n_wide = sum(1 for lev in range(len(dmats)) if (c >> lev) >= HG_WIDE_SEG)
    return (jnp.asarray(np.concatenate(dmats[n_wide:] + [tri], 0), dtype=BF16), jnp.asarray(np.stack(ups)),
            jnp.asarray(np.stack(los)), jnp.asarray(np.stack(masks)))


def _hgrn_kernel(pq_ref, pf_ref, pi_ref, pg_ref, lbl_ref, gain_ref, s0_ref, dd_ref, up_ref, lo_ref,
                 mk_ref, o_ref, sout_ref, st_sc, *, chunk, layer):
    l = pl.program_id(2)
    tl = pq_ref.shape[1]
    c = chunk
    nc = tl // c
    nlev = up_ref.shape[0]
    er = lax.broadcasted_iota(jnp.int32, (HG_D, HG_D), 0)
    ec = lax.broadcasted_iota(jnp.int32, (HG_D, HG_D), 1)
    eye_d = (er == ec).astype(F32)

    @pl.when(l == 0)
    def _():
        st_sc[...] = _dot_nt(eye_d, s0_ref[0, 0], HI)

    logits = lbl_ref[...]
    ex = jnp.exp(logits - jnp.max(logits, axis=0, keepdims=True))
    prob = ex / jnp.sum(ex, axis=0, keepdims=True)
    lb = jnp.sum(prob[0:layer + 1], axis=0, keepdims=True) - prob[0:1]

    qr = pq_ref[0]
    q = qr * _sigmoid(qr)
    forget = lb + (1.0 - lb) * _sigmoid(pf_ref[0])
    logf = jnp.log(jnp.maximum(forget, 1e-30))
    kg = 1.0 - forget
    v = pi_ref[0]

    ri = lax.broadcasted_iota(jnp.int32, (c, c), 0)
    ci = lax.broadcasted_iota(jnp.int32, (c, c), 1)
    diag = ri == ci
    rows = [slice(i * c, (i + 1) * c) for i in range(nc)]
    x_all = _split_dot(dd_ref[...], jnp.concatenate([logf[rs] for rs in rows], axis=1))
    n_wide = sum(1 for lev in range(nlev) if (c >> lev) >= HG_WIDE_SEG)
    n_nar = nlev - n_wide
    cum = jnp.concatenate([x_all[n_nar * c:(n_nar + 1) * c, i * HG_D:(i + 1) * HG_D] for i in range(nc)], axis=0)
    e_wide = []
    for lev in range(n_wide):
        z = c >> lev
        cum_z = cum.reshape(tl // z, z, HG_D)
        e_wide.append(jnp.exp(-jnp.abs(cum_z - cum_z[:, z // 2 - 1:z // 2, :])).reshape(tl, HG_D))
    cum_c = cum.reshape(nc, c, HG_D)
    e_in_all = jnp.exp(cum)
    e_out_all = jnp.exp(cum_c[:, c - 1:c, :] - cum_c).reshape(tl, HG_D)
    e_nar = jnp.exp(x_all[:n_nar * c])
    intra, kv, qe, dec = [], [], [], []
    for i, rs in enumerate(rows):
        qc, kc, vc = q[rs], kg[rs], v[rs]
        a = jnp.where(diag, jnp.sum(qc * kc, axis=1, keepdims=True), 0.0)
        for lev in range(nlev):
            if lev < n_wide:
                el = e_wide[lev][rs]
            else:
                el = e_nar[(lev - n_wide) * c:(lev - n_wide + 1) * c, i * HG_D:(i + 1) * HG_D]
            a = a + mk_ref[lev] * _dot_nt(_bf(qc * el * up_ref[lev]), _bf(kc * el * lo_ref[lev]))
        e_in = e_in_all[rs]
        e_out = e_out_all[rs]
        intra.append(_dot(_bf(a), _bf(vc)))
        kv.append(_dot_tn(_bf(vc), _bf(kc * e_out)))
        qe.append(_bf(qc * e_in))
        dec.append(e_in[c - 1:c, :])
    st = st_sc[...]
    outs = []
    for i in range(nc):
        outs.append(_dot_nt(qe[i], _bf(st)) + intra[i])
        st = st * dec[i] + kv[i]
    st_sc[...] = st
    o = jnp.concatenate(outs, axis=0) if nc > 1 else outs[0]
    ms = jnp.mean(o * o, axis=-1, keepdims=True)
    gr = pg_ref[0]
    o_ref[0] = o * lax.rsqrt(ms + RMS_EPS) * gain_ref[...] * (gr * _sigmoid(gr))

    @pl.when(l == pl.num_programs(2) - 1)
    def _():
        sout_ref[0, 0] = _dot_nt(eye_d, st, HI)


def hgrn_scan(p, lb_logits, out_gain, s0, layer, tl, chunk):
    bsz, seq, _ = p.shape
    assert seq % tl == 0 and tl % chunk == 0
    dd, up, lo, mk = _hgrn_constants(chunk)
    col = lambda off: pl.BlockSpec((1, tl, HG_D), lambda b, h, li: (b, li, off + h))
    full = lambda a: pl.BlockSpec(a.shape, lambda b, h, li: (0,) * a.ndim)
    st_spec = pl.BlockSpec((1, 1, HG_D, HG_D), lambda b, h, li: (b, h, 0, 0))
    return pl.pallas_call(
        functools.partial(_hgrn_kernel, chunk=chunk, layer=layer),
        grid=(bsz, HG_HEADS, seq // tl),
        in_specs=[col(0), col(HG_HEADS), col(2 * HG_HEADS), col(3 * HG_HEADS),
                  pl.BlockSpec((lb_logits.shape[0], HG_D), lambda b, h, li: (0, h)),
                  pl.BlockSpec((1, HG_D), lambda b, h, li: (0, 0)),
                  st_spec, full(dd), full(up), full(lo), full(mk)],
        out_specs=[pl.BlockSpec((1, tl, HG_D), lambda b, h, li: (b, li, h)), st_spec],
        out_shape=[jax.ShapeDtypeStruct((bsz, seq, D_MODEL), F32),
                   jax.ShapeDtypeStruct((bsz, HG_HEADS, HG_D, HG_D), F32)],
        scratch_shapes=[pltpu.VMEM((HG_D, HG_D), F32)],
        compiler_params=_cparams("parallel", "parallel", "arbitrary"),
    )(p, p, p, p, lb_logits, out_gain.reshape(1, HG_D), s0, dd, up, lo, mk)


def _row_to_col(row, eye):
    return jnp.sum(jnp.where(eye, row, 0.0), axis=1, keepdims=True)


def _col_to_row(col, eye):
    return jnp.sum(jnp.where(eye, col, 0.0), axis=0, keepdims=True)


def _rwkv_step_kernel(r_ref, lw_ref, k_ref, v_ref, kk_ref, b_ref, g_ref, lnw_ref, lnb_ref, rk_ref,
                      s0_ref, o_ref, sout_ref):
    hd = RW_HEAD
    ri = lax.broadcasted_iota(jnp.int32, (hd, hd), 0)
    ci = lax.broadcasted_iota(jnp.int32, (hd, hd), 1)
    eye = ri == ci
    for h in range(RW_HEADS):
        sl = slice(h * hd, (h + 1) * hd)
        row = lambda ref: ref[0][:, sl]
        r, k, v, kk, b = row(r_ref), row(k_ref), row(v_ref), row(kk_ref), row(b_ref)
        s = s0_ref[0, h]
        s_kk = jnp.sum(s * kk, axis=1, keepdims=True)
        s = s * jnp.exp(row(lw_ref)) - s_kk * b + _row_to_col(v, eye) * k
        sout_ref[0, h] = s
        o = _col_to_row(jnp.sum(s * r, axis=1, keepdims=True), eye)
        mean = jnp.mean(o, axis=-1, keepdims=True)
        var = jnp.mean(jnp.square(o - mean), axis=-1, keepdims=True)
        o = (o - mean) * lax.rsqrt(var + GN_EPS) * lnw_ref[:, sl] + lnb_ref[:, sl]
        bonus = jnp.sum(r * k * rk_ref[:, sl], axis=-1, keepdims=True) * v
        o_ref[0, :, sl] = (o + bonus) * row(g_ref)


def rwkv_step(r, lw, k, v, kk, b, g, ln_w, ln_b, r_k, s0):
    bsz = r.shape[0]
    seq_spec = pl.BlockSpec((1, 1, RW_WIDTH), lambda i: (i, 0, 0))
    par_spec = pl.BlockSpec((1, RW_WIDTH), lambda i: (0, 0))
    st_spec = pl.BlockSpec((1, RW_HEADS, RW_HEAD, RW_HEAD), lambda i: (i, 0, 0, 0))
    row = lambda a: a.reshape(1, RW_WIDTH)
    tok = lambda a: a.reshape(bsz, 1, RW_WIDTH)
    o, s = pl.pallas_call(
        _rwkv_step_kernel,
        grid=(bsz,),
        in_specs=[seq_spec] * 7 + [par_spec] * 3 + [st_spec],
        out_specs=[seq_spec, st_spec],
        out_shape=[jax.ShapeDtypeStruct((bsz, 1, RW_WIDTH), F32),
                   jax.ShapeDtypeStruct((bsz, RW_HEADS, RW_HEAD, RW_HEAD), F32)],
        compiler_params=_cparams("parallel"),
    )(*[tok(a) for a in (r, lw, k, v, kk, b, g)], row(ln_w), row(ln_b), row(r_k), s0)
    return o.reshape(bsz, RW_WIDTH), s


def _hgrn_step_kernel(p_ref, lbl_ref, gain_ref, s0_ref, o_ref, sout_ref, *, layer):
    d = HG_D
    ri = lax.broadcasted_iota(jnp.int32, (d, d), 0)
    ci = lax.broadcasted_iota(jnp.int32, (d, d), 1)
    eye = ri == ci
    logits = lbl_ref[...]
    ex = jnp.exp(logits - jnp.max(logits, axis=0, keepdims=True))
    prob = ex / jnp.sum(ex, axis=0, keepdims=True)
    lb_all = jnp.sum(prob[0:layer + 1], axis=0, keepdims=True) - prob[0:1]
    width = HG_HEADS * d
    for h in range(HG_HEADS):
        col = lambda part: p_ref[0][:, part * width + h * d:part * width + (h + 1) * d]
        qr, fr, v, gr = col(0), col(1), col(2), col(3)
        lb = lb_all[:, h * d:(h + 1) * d]
        q = qr * _sigmoid(qr)
        forget = lb + (1.0 - lb) * _sigmoid(fr)
        decay = jnp.maximum(forget, 1e-30)
        s = _row_to_col(decay, eye) * s0_ref[0, h] + _row_to_col(1.0 - forget, eye) * v
        sout_ref[0, h] = s
        o = jnp.sum(_row_to_col(q, eye) * s, axis=0, keepdims=True)
        ms = jnp.mean(o * o, axis=-1, keepdims=True)
        o_ref[0, :, h * d:(h + 1) * d] = o * lax.rsqrt(ms + RMS_EPS) * gain_ref[...] * (gr * _sigmoid(gr))


def hgrn_step(p, lb_logits, out_gain, s0, layer):
    bsz = p.shape[0]
    st_spec = pl.BlockSpec((1, HG_HEADS, HG_D, HG_D), lambda i: (i, 0, 0, 0))
    o, s = pl.pallas_call(
        functools.partial(_hgrn_step_kernel, layer=layer),
        grid=(bsz,),
        in_specs=[pl.BlockSpec((1, 1, p.shape[1]), lambda i: (i, 0, 0)),
                  pl.BlockSpec(lb_logits.shape, lambda i: (0, 0)),
                  pl.BlockSpec((1, HG_D), lambda i: (0, 0)), st_spec],
        out_specs=[pl.BlockSpec((1, 1, D_MODEL), lambda i: (i, 0, 0)), st_spec],
        out_shape=[jax.ShapeDtypeStruct((bsz, 1, D_MODEL), F32),
                   jax.ShapeDtypeStruct((bsz, HG_HEADS, HG_D, HG_D), F32)],
        compiler_params=_cparams("parallel"),
    )(p.reshape(bsz, 1, -1), lb_logits, out_gain.reshape(1, HG_D), s0)
    return o.reshape(bsz, D_MODEL), s


def _even_layer(xp, xs, il, n_even, vf_p, vf_s, kv_stacks, w, cache_k, cache_v, page_table, state_rwkv,
                state_shift):
    nb_p, seq, d = xp.shape
    nb_s = xs.shape[0]
    tp = nb_p * seq
    rw_params = (w['rw_mu'][il], w['rw_w0'][il], w['w2p'][il], w['rw_a0'][il], w['a2p'][il],
                 w['rw_g2'][il], w['rw_k_k'][il], w['rw_k_a'][il])
    has_vres = il > 0
    vres_w = (w['rw_v0'][il - 1], w['rw_v1'][il - 1], w['rw_v2'][il - 1]) if has_vres else None
    ln = (w['rw_ln_w'][il], w['rw_ln_b'][il], w['rw_r_k'][il].reshape(-1))

    p_rw, p_mb = norm_matmul(xp.reshape(tp, d), w['norm_mix'][2 * il], w['w_in_even'][il],
                             (RW_IN, 3 * MB_WIDTH), 512)
    vres = (vf_p,) + vres_w if has_vres else None
    prep = rwkv_prep(p_rw, None, rw_params, vres, True, seq, 512)
    if not has_vres:
        vf_p = prep[7]
    seq3 = lambda a: a.reshape(nb_p, seq, -1)
    o_rw, wkv_p = rwkv_scan(*[seq3(a) for a in prep[:7]], *ln,
                            jnp.zeros((nb_p, RW_HEADS, RW_HEAD, RW_HEAD), F32), 1024, RW_CHUNK)
    qn, k_stack, v_stack, kb, vt, ksum = moba_prep(p_mb, w['mb_q_norm'][il], w['mb_k_norm'][il], MOBA_BLOCK,
                                                   seq, il, n_even, kv_stacks)
    o_mb = moba_attn(seq3(qn), seq3(kb), vt, ksum.reshape(nb_p, seq // MOBA_BLOCK, MB_WIDTH))
    lyr = 2 * il
    yp = mix_ffn(xp.reshape(tp, d), [o_rw.reshape(tp, RW_WIDTH), o_mb.reshape(tp, MB_WIDTH)],
                 [w['w_out_even'][il][:RW_WIDTH], w['w_out_even'][il][RW_WIDTH:]],
                 w['norm_ffn'][lyr], w['ffn_w_gate'][lyr], w['ffn_w_up'][lyr], w['ffn_w_down'][lyr],
                 512, D_FF // 2).reshape(nb_p, seq, d)
    sh_p = p_rw.reshape(nb_p, seq, RW_IN)[:, -1]

    ps_rw, ps_mb = norm_matmul(xs.reshape(nb_s, d), w['norm_mix'][2 * il], w['w_in_even'][il],
                               (RW_IN, 3 * MB_WIDTH), nb_s)
    vres = (vf_s,) + vres_w if has_vres else None
    prep = rwkv_prep(ps_rw, state_shift[il], rw_params, vres, False, 1, nb_s)
    if not has_vres:
        vf_s = prep[7]
    os_rw, wkv_s = rwkv_step(*prep[:7], *ln, state_rwkv[il])
    qs, ks, vs, _, _, _ = moba_prep(ps_mb, w['mb_q_norm'][il], w['mb_k_norm'][il], nb_s, nb_s)
    ks, vs = ks[0, 0].T, vs[0, 0].T
    os_mb = moba_decode(qs, ks, vs, cache_k, cache_v, page_table, il)
    ys = mix_ffn(xs.reshape(nb_s, d), [os_rw, os_mb],
                 [w['w_out_even'][il][:RW_WIDTH], w['w_out_even'][il][RW_WIDTH:]],
                 w['norm_ffn'][lyr], w['ffn_w_gate'][lyr], w['ffn_w_up'][lyr], w['ffn_w_down'][lyr],
                 nb_s, D_FF // 2).reshape(nb_s, 1, d)
    mk_s = ks.reshape(nb_s, 1, MB_HEADS, MB_HEAD)
    mv_s = vs.reshape(nb_s, 1, MB_HEADS, MB_HEAD)
    return yp, ys, vf_p, vf_s, (k_stack, v_stack), (mk_s, mv_s, wkv_p, wkv_s, sh_p, ps_rw)


def _odd_layer(xp, xs, il, w, state_hgrn):
    nb_p, seq, d = xp.shape
    nb_s = xs.shape[0]
    tp = nb_p * seq
    lyr = 2 * il + 1
    (pp,) = norm_matmul(xp.reshape(tp, d), w['norm_mix'][lyr], w['w_in_odd'][il], (4 * d,), 512)
    o_p, hg_p = hgrn_scan(pp.reshape(nb_p, seq, 4 * d), w['hg_lb_logits'], w['hg_out_norm'][il],
                          jnp.zeros((nb_p, HG_HEADS, HG_D, HG_D), F32), il, 2048, HG_CHUNK)
    yp = mix_ffn(xp.reshape(tp, d), [o_p.reshape(tp, d)], [w['w_out_odd'][il]],
                 w['norm_ffn'][lyr], w['ffn_w_gate'][lyr], w['ffn_w_up'][lyr], w['ffn_w_down'][lyr],
                 512, D_FF // 2).reshape(nb_p, seq, d)
    (ps,) = norm_matmul(xs.reshape(nb_s, d), w['norm_mix'][lyr], w['w_in_odd'][il], (4 * d,), nb_s)
    o_s, hg_s = hgrn_step(ps, w['hg_lb_logits'], w['hg_out_norm'][il], state_hgrn[il], il)
    ys = mix_ffn(xs.reshape(nb_s, d), [o_s], [w['w_out_odd'][il]],
                 w['norm_ffn'][lyr], w['ffn_w_gate'][lyr], w['ffn_w_up'][lyr], w['ffn_w_down'][lyr],
                 nb_s, D_FF // 2).reshape(nb_s, 1, d)
    return yp, ys, hg_p, hg_s


def kernel(x_prompt, x_sample, cache_moba_k, cache_moba_v, page_table, state_rwkv, state_rwkv_shift, state_hgrn, norm_mix, norm_ffn, w_in_even, w_out_even, rw_mu, rw_w0, rw_w2, rw_a0, rw_a2, rw_g2, rw_k_k, rw_k_a, rw_r_k, rw_ln_w, rw_ln_b, rw_v0, rw_v1, rw_v2, mb_q_norm, mb_k_norm, w_in_odd, w_out_odd, hg_lb_logits, hg_out_norm, ffn_w_gate, ffn_w_up, ffn_w_down):
    depth = norm_mix.shape[0]
    zeros_lora = jnp.zeros_like(rw_w2)
    w = dict(
        norm_mix=norm_mix, norm_ffn=norm_ffn,
        w_in_even=w_in_even.astype(BF16), w_out_even=w_out_even.astype(BF16),
        w_in_odd=w_in_odd.astype(BF16), w_out_odd=w_out_odd.astype(BF16),
        ffn_w_gate=ffn_w_gate.astype(BF16), ffn_w_up=ffn_w_up.astype(BF16),
        ffn_w_down=ffn_w_down.astype(BF16),
        rw_mu=rw_mu, rw_w0=rw_w0, rw_a0=rw_a0, rw_g2=rw_g2, rw_k_k=rw_k_k, rw_k_a=rw_k_a,
        rw_r_k=rw_r_k, rw_ln_w=rw_ln_w, rw_ln_b=rw_ln_b, rw_v0=rw_v0, rw_v1=rw_v1, rw_v2=rw_v2,
        w2p=jnp.concatenate([rw_w2, zeros_lora], axis=1), a2p=jnp.concatenate([zeros_lora, rw_a2], axis=1),
        mb_q_norm=mb_q_norm, mb_k_norm=mb_k_norm, hg_lb_logits=hg_lb_logits, hg_out_norm=hg_out_norm)
    page_table = page_table.astype(jnp.int32)
    cache_k = jnp.transpose(cache_moba_k, (0, 1, 3, 4, 2))
    cache_v = jnp.transpose(cache_moba_v, (0, 1, 3, 4, 2))

    xp, xs = x_prompt, x_sample
    vf_p = vf_s = kv_stacks = None
    n_even = w_in_even.shape[0]
    even_out, hg_out = [], []
    for layer in range(depth):
        il = layer // 2
        if layer % 2 == 0:
            xp, xs, vf_p, vf_s, kv_stacks, outs = _even_layer(xp, xs, il, n_even, vf_p, vf_s, kv_stacks, w, cache_k,
                                                              cache_v, page_table, state_rwkv, state_rwkv_shift)
            even_out.append(outs)
        else:
            xp, xs, hg_p, hg_s = _odd_layer(xp, xs, il, w, state_hgrn)
            hg_out.append((hg_p, hg_s))
    stack = lambda i: jnp.stack([o[i] for o in even_out])
    nb_p, seq, _ = x_prompt.shape
    mk_p, mv_p = (jnp.transpose(a.reshape(n_even, nb_p, MB_HEADS, MB_HEAD, seq), (0, 1, 4, 2, 3))
                  for a in kv_stacks)
    return (xp, xs, mk_p, mv_p, stack(0), stack(1), stack(2), stack(3), stack(4), stack(5),
            jnp.stack([o[0] for o in hg_out]), jnp.stack([o[1] for o in hg_out]))
```

```python
import functools

import numpy as np
import jax
import jax.numpy as jnp
from jax import lax
from jax.experimental import pallas as pl
from jax.experimental.pallas import tpu as pltpu

F32 = jnp.float32
BF16 = jnp.bfloat16
HI = lax.Precision.HIGHEST

D_MODEL = 1024
PAGE_SIZE = 128
RW_HEAD = 64
RW_WIDTH = 512
RW_HEADS = 8
RW_IN = 1792
MB_HEAD = 64
MB_WIDTH = 512
MB_HEADS = 8
MOBA_BLOCK = 256
MOBA_TOPK = 3
HG_HEADS = 8
HG_D = 128
D_FF = 2816
RMS_EPS = 1e-6
GN_EPS = 64e-5
MASK_NEG = -1e30
LOG2E = 1.4426950408889634

LANES = 128
VMEM_LIMIT = 56 * 1024 * 1024
RW_CHUNK = 64
HG_CHUNK = 64
HG_WIDE_SEG = 16


def _cparams(*sem):
    return pltpu.CompilerParams(dimension_semantics=sem, vmem_limit_bytes=VMEM_LIMIT)


def _bf(x):
    return x.astype(BF16)


def _dot(a, b, precision=None):
    return jnp.dot(a, b, precision=precision, preferred_element_type=F32)


def _dot_nt(a, b, precision=None):
    return lax.dot_general(a, b, (((1,), (1,)), ((), ())), precision=precision,
                           preferred_element_type=F32)


def _dot_tn(a, b, precision=None):
    return lax.dot_general(a, b, (((0,), (0,)), ((), ())), precision=precision,
                           preferred_element_type=F32)


def _bmm(a, b):
    return lax.dot_general(_bf(a), _bf(b), (((2,), (1,)), ((0,), (0,))), preferred_element_type=F32)


def _bmm_nt(a, b):
    return lax.dot_general(_bf(a), _bf(b), (((2,), (2,)), ((0,), (0,))), preferred_element_type=F32)


def _bmm_tn(a, b):
    return lax.dot_general(_bf(a), _bf(b), (((1,), (1,)), ((0,), (0,))), preferred_element_type=F32)


def _split_dot(a_bf16, b):
    n = b.shape[1]
    hi = b.astype(BF16)
    lo = (b - hi.astype(F32)).astype(BF16)
    out = _dot(a_bf16, jnp.concatenate([hi, lo], axis=1))
    return out[:, :n] + out[:, n:]


def _hi_lo(x):
    hi = x.astype(BF16)
    return hi, (x - hi.astype(F32)).astype(BF16)


def _dot_x3(a, b):
    a_hi, a_lo = _hi_lo(a)
    b_hi, b_lo = _hi_lo(b)
    return _dot(a_hi, b_hi) + (_dot(a_hi, b_lo) + _dot(a_lo, b_hi))


def _dot_x3_nt(a, b):
    a_hi, a_lo = _hi_lo(a)
    b_hi, b_lo = _hi_lo(b)
    return _dot_nt(a_hi, b_hi) + (_dot_nt(a_hi, b_lo) + _dot_nt(a_lo, b_hi))


def _dot_x2(a, g_bf16):
    a_hi, a_lo = _hi_lo(a)
    return _dot(a_hi, g_bf16) + _dot(a_lo, g_bf16)


def _sigmoid(x):
    return 1.0 / (1.0 + jnp.exp(-x))


def _group_matrices(width, group):
    g = np.zeros((width, LANES), np.float32)
    g[np.arange(width), np.arange(width) // group] = 1.0
    return jnp.asarray(g, dtype=BF16), jnp.asarray(g.T.copy(), dtype=BF16)


def _norm_matmul_kernel(x_ref, gain_ref, w_ref, *out_refs, widths, tn):
    x = x_ref[...]
    ms = jnp.mean(x * x, axis=-1, keepdims=True)
    h = (x * lax.rsqrt(ms + RMS_EPS) * gain_ref[...]).astype(BF16)
    col = 0
    for o_ref, width in zip(out_refs, widths):
        for c0 in range(0, width, tn):
            sz = min(tn, width - c0)
            o_ref[:, c0:c0 + sz] = _dot(h, w_ref[:, col + c0:col + c0 + sz])
        col += width


def norm_matmul(x, gain, w_bf16, widths, tm):
    m, d = x.shape
    n = w_bf16.shape[1]
    assert sum(widths) == n and m % tm == 0
    kern = functools.partial(_norm_matmul_kernel, widths=tuple(widths), tn=512)
    return pl.pallas_call(
        kern,
        grid=(m // tm,),
        in_specs=[pl.BlockSpec((tm, d), lambda i: (i, 0)),
                  pl.BlockSpec((1, d), lambda i: (0, 0)),
                  pl.BlockSpec((d, n), lambda i: (0, 0))],
        out_specs=[pl.BlockSpec((tm, wd), lambda i: (i, 0)) for wd in widths],
        out_shape=[jax.ShapeDtypeStruct((m, wd), F32) for wd in widths],
        compiler_params=_cparams("parallel"),
    )(x, gain.reshape(1, d), w_bf16)


def _mix_ffn_kernel(*refs, n_act):
    x_ref = refs[0]
    a_refs = refs[1:1 + n_act]
    wo_refs = refs[1 + n_act:1 + 2 * n_act]
    gain_ref, wg_ref, wu_ref, wd_ref, y_ref, x1_sc, h_sc, acc_sc = refs[1 + 2 * n_act:]
    j = pl.program_id(1)

    @pl.when(j == 0)
    def _():
        x1 = x_ref[...]
        for a_ref, wo_ref in zip(a_refs, wo_refs):
            x1 = x1 + _dot(a_ref[...].astype(BF16), wo_ref[...])
        x1_sc[...] = x1
        ms = jnp.mean(x1 * x1, axis=-1, keepdims=True)
        h_sc[...] = (x1 * lax.rsqrt(ms + RMS_EPS) * gain_ref[...]).astype(BF16)
        acc_sc[...] = jnp.zeros_like(acc_sc)

    h = h_sc[...]
    g = _dot(h, wg_ref[...])
    u = _dot(h, wu_ref[...])
    act = (g * _sigmoid(g) * u).astype(BF16)
    acc_sc[...] += _dot(act, wd_ref[...])

    @pl.when(j == pl.num_programs(1) - 1)
    def _():
        y_ref[...] = x1_sc[...] + acc_sc[...]


def mix_ffn(x, acts, wos_bf16, gain, wg, wu, wd, tm, tf):
    m, d = x.shape
    f = wg.shape[1]
    assert m % tm == 0 and f % tf == 0
    n_act = len(acts)
    in_specs = [pl.BlockSpec((tm, d), lambda i, j: (i, 0))]
    in_specs += [pl.BlockSpec((tm, a.shape[1]), lambda i, j: (i, 0)) for a in acts]
    in_specs += [pl.BlockSpec(w.shape, lambda i, j: (0, 0)) for w in wos_bf16]
    in_specs += [pl.BlockSpec((1, d), lambda i, j: (0, 0)),
                 pl.BlockSpec((d, tf), lambda i, j: (0, j)),
                 pl.BlockSpec((d, tf), lambda i, j: (0, j)),
                 pl.BlockSpec((tf, d), lambda i, j: (j, 0))]
    return pl.pallas_call(
        functools.partial(_mix_ffn_kernel, n_act=n_act),
        grid=(m // tm, f // tf),
        in_specs=in_specs,
        out_specs=pl.BlockSpec((tm, d), lambda i, j: (i, 0)),
        out_shape=jax.ShapeDtypeStruct((m, d), F32),
        scratch_shapes=[pltpu.VMEM((tm, d), F32), pltpu.VMEM((tm, d), BF16), pltpu.VMEM((tm, d), F32)],
        compiler_params=_cparams("parallel", "arbitrary"),
    )(x, *acts, *wos_bf16, gain.reshape(1, d), wg, wu, wd)


def _rwkv_prep_kernel(*refs, shift_mode, has_vres, tiles_per_seq):
    it = iter(refs)
    p_ref = next(it)
    prev_ref = next(it)
    mu_ref, w0_ref, a0_ref, kk_ref, ka_ref, w2_ref, a2_ref, g2_ref, grp_ref, grpt_ref = (
        next(it) for _ in range(10))
    if has_vres:
        vf_ref, v0_ref, v1_ref, v2_ref = (next(it) for _ in range(4))
    r_o, lw_o, k_o, v_o, kkn_o, b_o, g_o = (next(it) for _ in range(7))
    if not has_vres:
        vf_o = next(it)

    pf = p_ref[...]
    if shift_mode:
        first = (pl.program_id(0) % tiles_per_seq) == 0
        prev_row = jnp.where(first, 0.0, prev_ref[7:8, :])
        rows = lax.broadcasted_iota(jnp.int32, pf.shape, 0)
        prev = jnp.where(rows == 0, prev_row, pltpu.roll(pf, 1, axis=0))
    else:
        prev = prev_ref[...]
    xs = pf + (prev - pf) * mu_ref[...]
    r = xs[:, 0:RW_WIDTH]
    k = xs[:, RW_WIDTH:2 * RW_WIDTH]
    v = xs[:, 2 * RW_WIDTH:3 * RW_WIDTH]
    wa_lo = xs[:, 3 * RW_WIDTH:3 * RW_WIDTH + LANES]
    g_lo = xs[:, 3 * RW_WIDTH + LANES:RW_IN]
    z = w0_ref[...] + _dot_x3(jnp.tanh(wa_lo), w2_ref[...])
    sp = jnp.maximum(-z, 0.0) + jnp.log(1.0 + jnp.exp(-jnp.abs(z)))
    logw = -jnp.exp(-sp - 0.5)
    a = _sigmoid(a0_ref[...] + _dot_x3(wa_lo, a2_ref[...]))
    g = _dot_x3(_sigmoid(g_lo), g2_ref[...])
    if has_vres:
        gate = _sigmoid(v0_ref[...] + _dot_x3(_dot_x3(v, v1_ref[...]), v2_ref[...]))
        v = v + (vf_ref[...] - v) * gate
    else:
        vf_o[...] = v
    kk = k * kk_ref[...]
    ss = _dot_x2(kk * kk, grp_ref[...])
    inv = _dot_x2(lax.rsqrt(jnp.maximum(ss, 1e-24)), grpt_ref[...])
    kk = kk * inv
    k = k * (1.0 + (a - 1.0) * ka_ref[...])
    r_o[...] = r
    lw_o[...] = logw
    k_o[...] = k
    v_o[...] = v
    kkn_o[...] = kk
    b_o[...] = kk * a
    g_o[...] = g


def rwkv_prep(p_rw, prev, params, vres, shift_mode, seq_len, tm):
    t = p_rw.shape[0]
    assert t % tm == 0
    mu, w0, w2p, a0, a2p, g2, k_k, k_a = params
    grp, grpt = _group_matrices(RW_WIDTH, RW_HEAD)
    row = lambda a: a.reshape(1, -1)
    tile = lambda wd: pl.BlockSpec((tm, wd), lambda i: (i, 0))
    full = lambda a: pl.BlockSpec(a.shape, lambda i: (0,) * a.ndim)
    args = [p_rw]
    specs = [tile(RW_IN)]
    if shift_mode:
        r8 = tm // 8
        args.append(p_rw)
        specs.append(pl.BlockSpec((8, RW_IN), lambda i: (jnp.maximum(i * r8 - 1, 0), 0)))
    else:
        args.append(prev)
        specs.append(tile(RW_IN))
    small = [row(mu), row(w0), row(a0), row(k_k), row(k_a), w2p, a2p, g2, grp, grpt]
    args += small
    specs += [full(a) for a in small]
    has_vres = vres is not None
    if has_vres:
        vf, v0, v1, v2 = vres
        extra = [row(v0), v1, v2]
        args += [vf] + extra
        specs += [tile(RW_WIDTH)] + [full(a) for a in extra]
    n_out = 7 if has_vres else 8
    kern = functools.partial(_rwkv_prep_kernel, shift_mode=shift_mode, has_vres=has_vres,
                             tiles_per_seq=max(seq_len // tm, 1))
    outs = pl.pallas_call(
        kern,
        grid=(t // tm,),
        in_specs=specs,
        out_specs=[tile(RW_WIDTH)] * n_out,
        out_shape=[jax.ShapeDtypeStruct((t, RW_WIDTH), F32)] * n_out,
        compiler_params=_cparams("parallel"),
    )(*args)
    return outs


def _unit_lower_inverse(n, c):
    ri = lax.broadcasted_iota(jnp.int32, (c, c), 0)
    ci = lax.broadcasted_iota(jnp.int32, (c, c), 1)
    eye = (ri == ci).astype(F32)[None]
    blk = min(16, c)
    same = ((ri // blk) == (ci // blk))[None]
    nd = jnp.where(same, n, 0.0)
    x = eye - nd
    pw = nd
    span = 2
    while span < blk:
        pw = _bmm(pw, pw)
        x = _bmm(x, eye + pw)
        span *= 2
    if c > blk:
        p = _bmm(x, jnp.where(same, 0.0, n))
        y = eye - p
        pw = p
        span = 2
        while span < c // blk:
            pw = _bmm(pw, pw)
            y = _bmm(y, eye + pw)
            span *= 2
        x = _bmm(y, x)
    return x


def _rwkv_scan_kernel(r_ref, lw_ref, k_ref, v_ref, kk_ref, b_ref, g_ref, lnw_ref, lnb_ref, rk_ref,
                      s0_ref, tri_ref, o_ref, sout_ref, s_sc, *, chunk):
    l = pl.program_id(2)
    tl = r_ref.shape[1]
    c = chunk
    nc = tl // c
    hd = RW_HEAD
    n_heads = LANES // hd

    @pl.when(l == 0)
    def _():
        s_sc[...] = jnp.zeros_like(s_sc)
        for h in range(n_heads):
            s_sc[h * hd:(h + 1) * hd, h * hd:(h + 1) * hd] = s0_ref[0, h]

    ri = lax.broadcasted_iota(jnp.int32, (c, c), 0)
    ci = lax.broadcasted_iota(jnp.int32, (c, c), 1)
    tri_incl = (ri >= ci)[None]
    tri_strict = (ri > ci)[None]
    kr = lax.broadcasted_iota(jnp.int32, (LANES, LANES), 0)
    kc = lax.broadcasted_iota(jnp.int32, (LANES, LANES), 1)
    same_head = (kr // hd) == (kc // hd)
    head_of_lane = lax.broadcasted_iota(jnp.int32, (1, 1, LANES), 2) // hd
    to3 = lambda x: x.reshape(nc, c, LANES)

    lw = to3(lw_ref[0])
    lc = to3(_split_dot(tri_ref[...], lw_ref[0]))
    tot = lc[:, c - 1:c, :]
    r, k, v, kk, b = to3(r_ref[0]), to3(k_ref[0]), to3(v_ref[0]), to3(kk_ref[0]), to3(b_ref[0])
    e_neg = jnp.exp(-lc)
    e_end = jnp.exp(tot - lc)
    kk_t = kk * jnp.exp(lc - lw)
    r_t = r * jnp.exp(lc)
    b_t = b * e_neg
    k_t = k * e_neg
    wm = u0 = q_eff = o0 = None
    for h in range(n_heads):
        mine = head_of_lane == h
        lhs = jnp.concatenate([jnp.where(mine, kk_t, 0.0), jnp.where(mine, r_t, 0.0)], axis=1)
        g_b = _bmm_nt(lhs, b_t)
        g_k = _bmm_nt(lhs, k_t)
        a_ab = jnp.where(tri_strict, g_b[:, :c], 0.0)
        a_rb = jnp.where(tri_incl, g_b[:, c:], 0.0)
        a_ak = jnp.where(tri_strict, g_k[:, :c], 0.0)
        a_rk = jnp.where(tri_incl, g_k[:, c:], 0.0)
        tinv = _unit_lower_inverse(a_ab, c)
        wu = _bmm(tinv, jnp.concatenate([kk_t, _bmm(a_ak, v)], axis=2))
        wm_h, u0_h = wu[:, :, :LANES], -wu[:, :, LANES:]
        ru = _bmm(a_rb, jnp.concatenate([wm_h, u0_h], axis=2))
        qe_h = r_t - ru[:, :, :LANES]
        o0_h = _bmm(a_rk, v) + ru[:, :, LANES:]
        if h == 0:
            wm, u0, q_eff, o0 = wm_h, u0_h, qe_h, o0_h
        else:
            wm, u0 = jnp.where(mine, wm_h, wm), jnp.where(mine, u0_h, u0)
            q_eff, o0 = jnp.where(mine, qe_h, q_eff), jnp.where(mine, o0_h, o0)
    eye = (kr == kc).astype(F32)[None]
    m_mat = jnp.where(same_head[None], eye * jnp.exp(tot) - _bmm_tn(wm, b * e_end), 0.0)
    d_mat = jnp.where(same_head[None],
                      _bmm_tn(jnp.concatenate([u0, v], axis=1),
                              jnp.concatenate([b * e_end, k * e_end], axis=1)), 0.0)
    s = s_sc[...]
    states = []
    for i in range(nc):
        states.append(s)
        s = _dot(_bf(s), _bf(m_mat[i])) + d_mat[i]
    s_sc[...] = s
    o = (o0 + _bmm_nt(q_eff, jnp.stack(states))).reshape(tl, LANES)
    head_sum = _bf(jnp.where(same_head, 1.0, 0.0))
    mean = _dot_x2(o, head_sum) * (1.0 / hd)
    var = _dot_x2(jnp.square(o - mean), head_sum) * (1.0 / hd)
    o = (o - mean) * lax.rsqrt(var + GN_EPS) * lnw_ref[...] + lnb_ref[...]
    bonus = _dot_x2(r_ref[0] * k_ref[0] * rk_ref[...], head_sum) * v_ref[0]
    o_ref[0] = (o + bonus) * g_ref[0]

    @pl.when(l == pl.num_programs(2) - 1)
    def _():
        for h in range(n_heads):
            sout_ref[0, h] = s[h * hd:(h + 1) * hd, h * hd:(h + 1) * hd]


def rwkv_scan(r, lw, k, v, kk, b, g, ln_w, ln_b, r_k, s0, tl, chunk):
    bsz, seq, _ = r.shape
    assert seq % tl == 0 and tl % chunk == 0
    hp = RW_WIDTH // LANES
    per = LANES // RW_HEAD
    seq_spec = pl.BlockSpec((1, tl, LANES), lambda bi, hi, li: (bi, li, hi))
    par_spec = pl.BlockSpec((1, LANES), lambda bi, hi, li: (0, hi))
    st_spec = pl.BlockSpec((1, per, RW_HEAD, RW_HEAD), lambda bi, hi, li: (bi, hi, 0, 0))
    row = lambda a: a.reshape(1, RW_WIDTH)
    idx = np.arange(tl)
    tri = jnp.asarray(((idx[:, None] >= idx[None, :]) & (idx[:, None] // chunk == idx[None, :] // chunk)),
                      dtype=BF16)
    return pl.pallas_call(
        functools.partial(_rwkv_scan_kernel, chunk=chunk),
        grid=(bsz, hp, seq // tl),
        in_specs=[seq_spec] * 7 + [par_spec] * 3 + [st_spec, pl.BlockSpec((tl, tl), lambda bi, hi, li: (0, 0))],
        out_specs=[seq_spec, st_spec],
        out_shape=[jax.ShapeDtypeStruct((bsz, seq, RW_WIDTH), F32),
                   jax.ShapeDtypeStruct((bsz, RW_HEADS, RW_HEAD, RW_HEAD), F32)],
        scratch_shapes=[pltpu.VMEM((LANES, LANES), F32)],
        compiler_params=_cparams("parallel", "parallel", "arbitrary"),
    )(r, lw, k, v, kk, b, g, row(ln_w), row(ln_b), row(r_k), s0, tri)


def _moba_prep_kernel(p_ref, qg_ref, kg_ref, grp_ref, grpt_ref, *refs):
    q_o, k_o, v_o, kb_o, vt_o, ks_o = refs[-6:]
    p = p_ref[...]
    q = p[:, 0:MB_WIDTH]
    k = p[:, MB_WIDTH:2 * MB_WIDTH]
    v = p[:, 2 * MB_WIDTH:3 * MB_WIDTH]

    def head_norm(x, gain_ref):
        ms = _dot_x2(x * x, grp_ref[...]) * (1.0 / MB_HEAD)
        inv = _dot_x2(lax.rsqrt(ms + RMS_EPS), grpt_ref[...])
        return x * inv * gain_ref[...]

    qn = head_norm(q, qg_ref)
    kn = head_norm(k, kg_ref)
    q_o[...] = qn
    v_t = v.T
    k_o[0, 0] = kn.T
    v_o[0, 0] = v_t
    kb_o[...] = kn.astype(BF16)
    vt_o[...] = v_t.astype(BF16)
    ks_o[0] = jnp.sum(kn, axis=0, keepdims=True)


def moba_prep(p_mb, q_gain, k_gain, tm, seq, layer=0, n_layers=1, stacks=None):
    t = p_mb.shape[0]
    assert t % seq == 0 and seq % tm == 0
    tps = seq // tm
    grp, grpt = _group_matrices(MB_WIDTH, MB_HEAD)
    tile = lambda i: (i, 0)
    full = lambda a: pl.BlockSpec(a.shape, lambda i: (0,) * a.ndim)
    qg = jnp.tile(q_gain, MB_HEADS).reshape(1, MB_WIDTH)
    kg = jnp.tile(k_gain, MB_HEADS).reshape(1, MB_WIDTH)
    args = [p_mb, qg, kg, grp, grpt]
    in_specs = [pl.BlockSpec((tm, 3 * MB_WIDTH), tile), full(qg), full(kg), full(grp), full(grpt)]
    aliases = {}
    if stacks is not None:
        aliases = {len(args): 1, len(args) + 1: 2}
        args += list(stacks)
        in_specs += [pl.BlockSpec(memory_space=pl.ANY)] * 2
    slab = pl.BlockSpec((1, 1, MB_WIDTH, tm), lambda i: (layer, i // tps, 0, i % tps))
    stack_shape = jax.ShapeDtypeStruct((n_layers, t // seq, MB_WIDTH, seq), F32)
    return pl.pallas_call(
        _moba_prep_kernel,
        grid=(t // tm,),
        in_specs=in_specs,
        out_specs=[pl.BlockSpec((tm, MB_WIDTH), tile), slab, slab, pl.BlockSpec((tm, MB_WIDTH), tile),
                   pl.BlockSpec((MB_WIDTH, tm), lambda i: (0, i)),
                   pl.BlockSpec((1, 1, MB_WIDTH), lambda i: (i, 0, 0))],
        out_shape=[jax.ShapeDtypeStruct((t, MB_WIDTH), F32), stack_shape, stack_shape,
                   jax.ShapeDtypeStruct((t, MB_WIDTH), BF16), jax.ShapeDtypeStruct((MB_WIDTH, t), BF16),
                   jax.ShapeDtypeStruct((t // tm, 1, MB_WIDTH), F32)],
        input_output_aliases=aliases,
        compiler_params=_cparams("parallel"),
    )(*args)


def _top_blocks(gate, idx, axis):
    sel = jnp.zeros_like(gate)
    rem = gate
    for _ in range(MOBA_TOPK):
        mx = jnp.max(rem, axis=axis, keepdims=True)
        live = jnp.logical_and(rem == mx, rem > 0.5 * MASK_NEG)
        first = jnp.min(jnp.where(live, idx, 1e9), axis=axis, keepdims=True)
        pick = idx == first
        sel = jnp.where(pick, 1.0, sel)
        rem = jnp.where(pick, MASK_NEG, rem)
    return sel


def _moba_attn_kernel(q_ref, k_ref, vt_ref, ks_ref, o_ref, s_a, s_b, p_a, p_b):
    qi = pl.program_id(2)
    tq = q_ref.shape[1]
    blk = MOBA_BLOCK
    assert tq == 2 * blk
    nb = ks_ref.shape[1]
    n_heads = LANES // MB_HEAD
    ncol = n_heads * tq
    q = q_ref[0]
    lane = lax.broadcasted_iota(jnp.int32, (1, LANES), 1)
    bmean = ks_ref[0] * (1.0 / MOBA_BLOCK)
    blk_f = lax.broadcasted_iota(jnp.int32, (nb, tq), 0).astype(F32)
    second = lax.broadcasted_iota(jnp.int32, (1, tq), 1) >= blk
    own_f = (2 * qi).astype(F32) + jnp.where(second, 1.0, 0.0)
    q_cols, bias_cols, own_bias = [], [], []
    pad_rows = jnp.full((LANES - nb, tq), MASK_NEG, F32)
    first_blk = lax.broadcasted_iota(jnp.int32, (nb, tq), 0) == 2 * qi
    for h in range(n_heads):
        q_h = jnp.where((lane // MB_HEAD) == h, q, 0.0)
        gate_t = _dot_x3_nt(bmean, q_h)
        gate_t = jnp.where(blk_f < own_f, gate_t, MASK_NEG)
        bias = (1.0 - _top_blocks(gate_t, blk_f, 0)) * MASK_NEG
        own_bias.append(jnp.max(jnp.where(first_blk, bias, MASK_NEG), axis=0, keepdims=True))
        bias_cols.append(jnp.concatenate([bias, pad_rows], axis=0).T)
        q_cols.append(q_h * (LOG2E * MB_HEAD ** -0.5))
    q_all = _bf(jnp.concatenate(q_cols, axis=0))
    q_ext = jnp.concatenate([q_all, _bf(jnp.concatenate(bias_cols, axis=0))], axis=1)

    def ones_rows(n):
        return _bf(jnp.where(lax.broadcasted_iota(jnp.int32, (16, n), 0) == 0, 1.0, 0.0))

    def values(r0, p):
        out = []
        for h in range(n_heads):
            vt_h = jnp.concatenate([vt_ref[h * MB_HEAD:(h + 1) * MB_HEAD, pl.ds(r0, tq)], ones_rows(tq)], axis=0)
            out.append(_dot(vt_h, p[:, h * tq:(h + 1) * tq]))
        return jnp.concatenate(out, axis=1)

    row0 = pl.multiple_of(qi * tq, tq)
    ri = lax.broadcasted_iota(jnp.int32, (tq, ncol), 0)
    cq = lax.broadcasted_iota(jnp.int32, (tq, ncol), 1) % tq
    gated = jnp.logical_and(ri < blk, cq >= blk)
    s = _dot_nt(k_ref[0, pl.ds(row0, tq), :], q_all) + jnp.where(gated, jnp.concatenate(own_bias, axis=1), 0.0)
    s = jnp.where(ri <= cq, s, MASK_NEG)
    m = jnp.max(s, axis=0, keepdims=True)
    acc = values(row0, _bf(jnp.exp2(s - m)))

    lane_i = lax.broadcasted_iota(jnp.int32, (1, LANES), 1)

    def pair_rows(i):
        return pl.multiple_of(jnp.minimum(i, nb // 2 - 1) * tq, tq)

    def scores(i, s_ref):
        k2 = k_ref[0, pl.ds(pair_rows(i), tq), :]
        onehot = [jnp.broadcast_to(_bf(jnp.where(lane_i == jnp.where(i < qi, 2 * i + u, LANES - 1), 1.0, 0.0)),
                                   (blk, LANES)) for u in range(2)]
        k_ext = jnp.concatenate([k2, jnp.concatenate(onehot, axis=0)], axis=1)
        s_ref[...] = _dot_nt(k_ext, q_ext)

    def softmax(s_ref, p_ref, m):
        s = s_ref[...]
        m_new = jnp.maximum(m, jnp.max(s, axis=0, keepdims=True))
        p_ref[...] = _bf(jnp.exp2(s - m_new))
        return m_new, jnp.exp2(m - m_new)

    scores(0, s_a)
    p_b[...] = jnp.zeros_like(p_b)

    def body(t, carry):
        m, acc = carry
        scores(2 * t + 1, s_b)
        pv = values(pair_rows(jnp.maximum(2 * t - 1, 0)), p_b[...])
        m, alpha = softmax(s_a, p_a, m)
        acc = (acc + pv) * alpha
        scores(2 * t + 2, s_a)
        pv = values(pair_rows(2 * t), p_a[...])
        m, alpha = softmax(s_b, p_b, m)
        acc = (acc + pv) * alpha
        return m, acc

    trips = (qi + 1) // 2
    m, acc = lax.fori_loop(0, trips, body, (m, acc))
    acc = acc + values(pair_rows(jnp.maximum(2 * trips - 1, 0)), p_b[...])
    o = acc[:MB_HEAD] / acc[MB_HEAD:MB_HEAD + 1]
    o_ref[0] = jnp.concatenate([o[:, h * tq:(h + 1) * tq] for h in range(n_heads)], axis=0).T


def moba_attn(qn, kb, vt, ksum):
    bsz, seq, _ = qn.shape
    tq = 2 * MOBA_BLOCK
    nb = seq // MOBA_BLOCK
    assert seq % tq == 0 and nb <= LANES
    hp = MB_WIDTH // LANES
    ncol = (LANES // MB_HEAD) * tq
    return pl.pallas_call(
        _moba_attn_kernel,
        grid=(bsz, hp, seq // tq),
        in_specs=[pl.BlockSpec((1, tq, LANES), lambda b, h, i: (b, i, h)),
                  pl.BlockSpec((1, seq, LANES), lambda b, h, i: (b, 0, h)),
                  pl.BlockSpec((LANES, seq), lambda b, h, i: (h, b)),
                  pl.BlockSpec((1, nb, LANES), lambda b, h, i: (b, 0, h))],
        out_specs=pl.BlockSpec((1, tq, LANES), lambda b, h, i: (b, i, h)),
        out_shape=jax.ShapeDtypeStruct((bsz, seq, MB_WIDTH), F32),
        scratch_shapes=[pltpu.VMEM((tq, ncol), F32), pltpu.VMEM((tq, ncol), F32),
                        pltpu.VMEM((tq, ncol), BF16), pltpu.VMEM((tq, ncol), BF16)],
        compiler_params=_cparams("parallel", "parallel", "arbitrary"),
    )(qn, kb, vt, ksum)


def _moba_scores_kernel(pt_ref, q_ref, *refs, n_pg):
    k_refs = refs[:n_pg]
    s_o = refs[n_pg]
    q = q_ref[0]
    for j in range(n_pg):
        s_o[0, j] = jnp.sum(k_refs[j][0, 0] * q, axis=1)


def _moba_select_kernel(s_ref, q_ref, kn_ref, p_o, w_o, id_o, *, ppb):
    scale = MB_HEAD ** -0.5
    s = s_ref[0]
    n_lp = s.shape[0]
    nblk = n_lp // ppb
    page_sum = jnp.sum(s, axis=-1, keepdims=True)
    gate = jnp.sum(page_sum.reshape(nblk, ppb, MB_HEADS, 1), axis=1) * (1.0 / MOBA_BLOCK)
    idx = lax.broadcasted_iota(jnp.int32, gate.shape, 0).astype(F32)
    lane = lax.broadcasted_iota(jnp.int32, (MB_HEADS, LANES), 1)
    sel = jnp.zeros_like(gate)
    rem = gate
    ids = jnp.zeros((MB_HEADS, LANES), F32)
    for t in range(MOBA_TOPK):
        mx = jnp.max(rem, axis=0, keepdims=True)
        first = jnp.min(jnp.where(rem == mx, idx, 1e9), axis=0, keepdims=True)
        pick = idx == first
        sel = jnp.where(pick, 1.0, sel)
        rem = jnp.where(pick, MASK_NEG, rem)
        for u in range(ppb):
            ids = jnp.where(lane == t * ppb + u, first[0] * ppb + u, ids)
    id_o[0] = ids.astype(jnp.int32)
    sel_pg = jnp.broadcast_to(sel[:, None], (nblk, ppb, MB_HEADS, 1)).reshape(n_lp, MB_HEADS, 1) > 0.5
    s_own = jnp.sum(kn_ref[0] * q_ref[0], axis=-1, keepdims=True) * scale
    sm = jnp.where(sel_pg, s * scale, MASK_NEG)
    m_all = jnp.maximum(jnp.max(jnp.max(sm, axis=0), axis=-1, keepdims=True), s_own)
    p = jnp.where(sel_pg, jnp.exp(sm - m_all[None]), 0.0)
    w_own = jnp.exp(s_own - m_all)
    inv = 1.0 / (jnp.sum(jnp.sum(p, axis=0), axis=-1, keepdims=True) + w_own)
    p_o[0] = p * inv[None]
    w_o[0] = jnp.broadcast_to(w_own * inv, (MB_HEADS, LANES))


def _moba_values_kernel(pt_ref, id_ref, p_ref, w_ref, vn_ref, *refs, n_sel, hps):
    v_refs = refs[:hps * n_sel]
    o_ref = refs[hps * n_sel]
    b = pl.program_id(0)
    for hh in range(hps):
        h = pl.program_id(1) * hps + hh
        acc = w_ref[0, pl.ds(h, 1), 0:1] * vn_ref[0, hh]
        for j in range(n_sel):
            page = id_ref[(b * MB_HEADS + h) * n_sel + j]
            prob = p_ref[0, page, pl.ds(h, 1), :]
            acc = acc + _dot_nt(prob, v_refs[hh * n_sel + j][0, 0, 0], HI)
        o_ref[0, hh] = acc


def moba_decode(qn, kn, vn, k_cache_t, v_cache_t, page_table, layer):
    bsz = qn.shape[0]
    n_lp = page_table.shape[1]
    ppb = MOBA_BLOCK // PAGE_SIZE
    assert n_lp % ppb == 0 and n_lp // ppb >= MOBA_TOPK
    n_pg = min(32, n_lp)
    assert n_lp % n_pg == 0
    n_sel = MOBA_TOPK * ppb
    hps = MB_HEADS
    heads = lambda a: a.reshape(bsz, MB_HEADS, MB_HEAD)
    page_blk = (1, 1, MB_HEADS, MB_HEAD, PAGE_SIZE)

    scores = pl.pallas_call(
        functools.partial(_moba_scores_kernel, n_pg=n_pg),
        grid_spec=pltpu.PrefetchScalarGridSpec(
            num_scalar_prefetch=1,
            grid=(bsz, n_lp // n_pg),
            in_specs=[pl.BlockSpec((1, MB_HEADS, MB_HEAD, 1), lambda b, g, pt: (b, 0, 0, 0))]
                     + [pl.BlockSpec(page_blk, functools.partial(
                         lambda b, g, pt, j: (layer, pt[b, g * n_pg + j], 0, 0, 0), j=j)) for j in range(n_pg)],
            out_specs=pl.BlockSpec((1, n_pg, MB_HEADS, PAGE_SIZE), lambda b, g, pt: (b, g, 0, 0))),
        out_shape=jax.ShapeDtypeStruct((bsz, n_lp, MB_HEADS, PAGE_SIZE), F32),
        compiler_params=_cparams("parallel", "arbitrary"),
    )(page_table, qn.reshape(bsz, MB_HEADS, MB_HEAD, 1), *([k_cache_t] * n_pg))

    vec = pl.BlockSpec((1, MB_HEADS, MB_HEAD), lambda b: (b, 0, 0))
    probs, w_own, ids = pl.pallas_call(
        functools.partial(_moba_select_kernel, ppb=ppb),
        grid=(bsz,),
        in_specs=[pl.BlockSpec((1, n_lp, MB_HEADS, PAGE_SIZE), lambda b: (b, 0, 0, 0)), vec, vec],
        out_specs=[pl.BlockSpec((1, n_lp, MB_HEADS, PAGE_SIZE), lambda b: (b, 0, 0, 0)),
                   pl.BlockSpec((1, MB_HEADS, LANES), lambda b: (b, 0, 0)),
                   pl.BlockSpec((1, MB_HEADS, LANES), lambda b: (b, 0, 0))],
        out_shape=[jax.ShapeDtypeStruct((bsz, n_lp, MB_HEADS, PAGE_SIZE), F32),
                   jax.ShapeDtypeStruct((bsz, MB_HEADS, LANES), F32),
                   jax.ShapeDtypeStruct((bsz, MB_HEADS, LANES), jnp.int32)],
        compiler_params=_cparams("parallel"),
    )(scores, heads(qn), heads(kn))

    sel_ids = ids[:, :, :n_sel].reshape(-1)
    out = pl.pallas_call(
        functools.partial(_moba_values_kernel, n_sel=n_sel, hps=hps),
        grid_spec=pltpu.PrefetchScalarGridSpec(
            num_scalar_prefetch=2,
            grid=(bsz, MB_HEADS // hps),
            in_specs=[pl.BlockSpec((1, n_lp, MB_HEADS, PAGE_SIZE), lambda b, g, pt, sid: (b, 0, 0, 0)),
                      pl.BlockSpec((1, MB_HEADS, LANES), lambda b, g, pt, sid: (b, 0, 0)),
                      pl.BlockSpec((1, hps, 1, MB_HEAD), lambda b, g, pt, sid: (b, g, 0, 0))]
                     + [pl.BlockSpec((1, 1, 1, MB_HEAD, PAGE_SIZE), functools.partial(
                         lambda b, g, pt, sid, hh, j: (
                             layer, pt[b, sid[(b * MB_HEADS + g * hps + hh) * n_sel + j]], g * hps + hh, 0, 0),
                         hh=hh, j=j)) for hh in range(hps) for j in range(n_sel)],
            out_specs=pl.BlockSpec((1, hps, 1, MB_HEAD), lambda b, g, pt, sid: (b, g, 0, 0))),
        out_shape=jax.ShapeDtypeStruct((bsz, MB_HEADS, 1, MB_HEAD), F32),
        compiler_params=_cparams("parallel", "arbitrary"),
    )(page_table, sel_ids, probs, w_own, vn.reshape(bsz, MB_HEADS, 1, MB_HEAD), *([v_cache_t] * (hps * n_sel)))
    return out.reshape(bsz, MB_WIDTH)


def _hgrn_constants(c):
    rows = np.arange(c)
    dmats, ups, los, masks = [], [], [], []
    z = c
    while z >= 2:
        seg = rows // z
        mid = seg * z + z // 2
        up = rows >= mid
        d = np.zeros((c, c), np.float32)
        for i in rows:
            if up[i]:
                d[i, mid[i]:i + 1] = 1.0
            else:
                d[i, i + 1:mid[i]] = 1.0
        dmats.append(d)
        ups.append(np.repeat(up[:, None], HG_D, 1).astype(np.float32))
        los.append(np.repeat(~up[:, None], HG_D, 1).astype(np.float32))
        masks.append(((seg[:, None] == seg[None, :]) & up[:, None] & (~up)[None, :]).astype(np.float32))
        z //= 2
    tri = np.tril(np.ones((c, c), np.float32))
    n_wide = sum(1 for lev in range(len(dmats)) if (c >> lev) >= HG_WIDE_SEG)
    return (jnp.asarray(np.concatenate(dmats[n_wide:] + [tri], 0), dtype=BF16), jnp.asarray(np.stack(ups)),
            jnp.asarray(np.stack(los)), jnp.asarray(np.stack(masks)))


def _hgrn_kernel(pq_ref, pf_ref, pi_ref, pg_ref, lbl_ref, gain_ref, s0_ref, dd_ref, up_ref, lo_ref,
                 mk_ref, o_ref, sout_ref, st_sc, *, chunk, layer):
    l = pl.program_id(2)
    tl = pq_ref.shape[1]
    c = chunk
    nc = tl // c
    nlev = up_ref.shape[0]
    er = lax.broadcasted_iota(jnp.int32, (HG_D, HG_D), 0)
    ec = lax.broadcasted_iota(jnp.int32, (HG_D, HG_D), 1)
    eye_d = (er == ec).astype(F32)

    @pl.when(l == 0)
    def _():
        st_sc[...] = _dot_nt(eye_d, s0_ref[0, 0], HI)

    logits = lbl_ref[...]
    ex = jnp.exp(logits - jnp.max(logits, axis=0, keepdims=True))
    prob = ex / jnp.sum(ex, axis=0, keepdims=True)
    lb = jnp.sum(prob[0:layer + 1], axis=0, keepdims=True) - prob[0:1]

    qr = pq_ref[0]
    q = qr * _sigmoid(qr)
    forget = lb + (1.0 - lb) * _sigmoid(pf_ref[0])
    logf = jnp.log(jnp.maximum(forget, 1e-30))
    kg = 1.0 - forget
    v = pi_ref[0]

    ri = lax.broadcasted_iota(jnp.int32, (c, c), 0)
    ci = lax.broadcasted_iota(jnp.int32, (c, c), 1)
    diag = ri == ci
    rows = [slice(i * c, (i + 1) * c) for i in range(nc)]
    x_all = _split_dot(dd_ref[...], jnp.concatenate([logf[rs] for rs in rows], axis=1))
    n_wide = sum(1 for lev in range(nlev) if (c >> lev) >= HG_WIDE_SEG)
    n_nar = nlev - n_wide
    cum = jnp.concatenate([x_all[n_nar * c:(n_nar + 1) * c, i * HG_D:(i + 1) * HG_D] for i in range(nc)], axis=0)
    e_wide = []
    for lev in range(n_wide):
        z = c >> lev
        cum_z = cum.reshape(tl // z, z, HG_D)
        e_wide.append(jnp.exp(-jnp.abs(cum_z - cum_z[:, z // 2 - 1:z // 2, :])).reshape(tl, HG_D))
    cum_c = cum.reshape(nc, c, HG_D)
    e_in_all = jnp.exp(cum)
    e_out_all = jnp.exp(cum_c[:, c - 1:c, :] - cum_c).reshape(tl, HG_D)
    e_nar = jnp.exp(x_all[:n_nar * c])
    intra, kv, qe, dec = [], [], [], []
    for i, rs in enumerate(rows):
        qc, kc, vc = q[rs], kg[rs], v[rs]
        a = jnp.where(diag, jnp.sum(qc * kc, axis=1, keepdims=True), 0.0)
        for lev in range(nlev):
            if lev < n_wide:
                el = e_wide[lev][rs]
            else:
                el = e_nar[(lev - n_wide) * c:(lev - n_wide + 1) * c, i * HG_D:(i + 1) * HG_D]
            a = a + mk_ref[lev] * _dot_nt(_bf(qc * el * up_ref[lev]), _bf(kc * el * lo_ref[lev]))
        e_in = e_in_all[rs]
        e_out = e_out_all[rs]
        intra.append(_dot(_bf(a), _bf(vc)))
        kv.append(_dot_tn(_bf(vc), _bf(kc * e_out)))
        qe.append(_bf(qc * e_in))
        dec.append(e_in[c - 1:c, :])
    st = st_sc[...]
    outs = []
    for i in range(nc):
        outs.append(_dot_nt(qe[i], _bf(st)) + intra[i])
        st = st * dec[i] + kv[i]
    st_sc[...] = st
    o = jnp.concatenate(outs, axis=0) if nc > 1 else outs[0]
    ms = jnp.mean(o * o, axis=-1, keepdims=True)
    gr = pg_ref[0]
    o_ref[0] = o * lax.rsqrt(ms + RMS_EPS) * gain_ref[...] * (gr * _sigmoid(gr))

    @pl.when(l == pl.num_programs(2) - 1)
    def _():
        sout_ref[0, 0] = _dot_nt(eye_d, st, HI)


def hgrn_scan(p, lb_logits, out_gain, s0, layer, tl, chunk):
    bsz, seq, _ = p.shape
    assert seq % tl == 0 and tl % chunk == 0
    dd, up, lo, mk = _hgrn_constants(chunk)
    col = lambda off: pl.BlockSpec((1, tl, HG_D), lambda b, h, li: (b, li, off + h))
    full = lambda a: pl.BlockSpec(a.shape, lambda b, h, li: (0,) * a.ndim)
    st_spec = pl.BlockSpec((1, 1, HG_D, HG_D), lambda b, h, li: (b, h, 0, 0))
    return pl.pallas_call(
        functools.partial(_hgrn_kernel, chunk=chunk, layer=layer),
        grid=(bsz, HG_HEADS, seq // tl),
        in_specs=[col(0), col(HG_HEADS), col(2 * HG_HEADS), col(3 * HG_HEADS),
                  pl.BlockSpec((lb_logits.shape[0], HG_D), lambda b, h, li: (0, h)),
                  pl.BlockSpec((1, HG_D), lambda b, h, li: (0, 0)),
                  st_spec, full(dd), full(up), full(lo), full(mk)],
        out_specs=[pl.BlockSpec((1, tl, HG_D), lambda b, h, li: (b, li, h)), st_spec],
        out_shape=[jax.ShapeDtypeStruct((bsz, seq, D_MODEL), F32),
                   jax.ShapeDtypeStruct((bsz, HG_HEADS, HG_D, HG_D), F32)],
        scratch_shapes=[pltpu.VMEM((HG_D, HG_D), F32)],
        compiler_params=_cparams("parallel", "parallel", "arbitrary"),
    )(p, p, p, p, lb_logits, out_gain.reshape(1, HG_D), s0, dd, up, lo, mk)


def _row_to_col(row, eye):
    return jnp.sum(jnp.where(eye, row, 0.0), axis=1, keepdims=True)


def _col_to_row(col, eye):
    return jnp.sum(jnp.where(eye, col, 0.0), axis=0, keepdims=True)


def _rwkv_step_kernel(r_ref, lw_ref, k_ref, v_ref, kk_ref, b_ref, g_ref, lnw_ref, lnb_ref, rk_ref,
                      s0_ref, o_ref, sout_ref):
    hd = RW_HEAD
    ri = lax.broadcasted_iota(jnp.int32, (hd, hd), 0)
    ci = lax.broadcasted_iota(jnp.int32, (hd, hd), 1)
    eye = ri == ci
    for h in range(RW_HEADS):
        sl = slice(h * hd, (h + 1) * hd)
        row = lambda ref: ref[0][:, sl]
        r, k, v, kk, b = row(r_ref), row(k_ref), row(v_ref), row(kk_ref), row(b_ref)
        s = s0_ref[0, h]
        s_kk = jnp.sum(s * kk, axis=1, keepdims=True)
        s = s * jnp.exp(row(lw_ref)) - s_kk * b + _row_to_col(v, eye) * k
        sout_ref[0, h] = s
        o = _col_to_row(jnp.sum(s * r, axis=1, keepdims=True), eye)
        mean = jnp.mean(o, axis=-1, keepdims=True)
        var = jnp.mean(jnp.square(o - mean), axis=-1, keepdims=True)
        o = (o - mean) * lax.rsqrt(var + GN_EPS) * lnw_ref[:, sl] + lnb_ref[:, sl]
        bonus = jnp.sum(r * k * rk_ref[:, sl], axis=-1, keepdims=True) * v
        o_ref[0, :, sl] = (o + bonus) * row(g_ref)


def rwkv_step(r, lw, k, v, kk, b, g, ln_w, ln_b, r_k, s0):
    bsz = r.shape[0]
    seq_spec = pl.BlockSpec((1, 1, RW_WIDTH), lambda i: (i, 0, 0))
    par_spec = pl.BlockSpec((1, RW_WIDTH), lambda i: (0, 0))
    st_spec = pl.BlockSpec((1, RW_HEADS, RW_HEAD, RW_HEAD), lambda i: (i, 0, 0, 0))
    row = lambda a: a.reshape(1, RW_WIDTH)
    tok = lambda a: a.reshape(bsz, 1, RW_WIDTH)
    o, s = pl.pallas_call(
        _rwkv_step_kernel,
        grid=(bsz,),
        in_specs=[seq_spec] * 7 + [par_spec] * 3 + [st_spec],
        out_specs=[seq_spec, st_spec],
        out_shape=[jax.ShapeDtypeStruct((bsz, 1, RW_WIDTH), F32),
                   jax.ShapeDtypeStruct((bsz, RW_HEADS, RW_HEAD, RW_HEAD), F32)],
        compiler_params=_cparams("parallel"),
    )(*[tok(a) for a in (r, lw, k, v, kk, b, g)], row(ln_w), row(ln_b), row(r_k), s0)
    return o.reshape(bsz, RW_WIDTH), s


def _hgrn_step_kernel(p_ref, lbl_ref, gain_ref, s0_ref, o_ref, sout_ref, *, layer):
    d = HG_D
    ri = lax.broadcasted_iota(jnp.int32, (d, d), 0)
    ci = lax.broadcasted_iota(jnp.int32, (d, d), 1)
    eye = ri == ci
    logits = lbl_ref[...]
    ex = jnp.exp(logits - jnp.max(logits, axis=0, keepdims=True))
    prob = ex / jnp.sum(ex, axis=0, keepdims=True)
    lb_all = jnp.sum(prob[0:layer + 1], axis=0, keepdims=True) - prob[0:1]
    width = HG_HEADS * d
    for h in range(HG_HEADS):
        col = lambda part: p_ref[0][:, part * width + h * d:part * width + (h + 1) * d]
        qr, fr, v, gr = col(0), col(1), col(2), col(3)
        lb = lb_all[:, h * d:(h + 1) * d]
        q = qr * _sigmoid(qr)
        forget = lb + (1.0 - lb) * _sigmoid(fr)
        decay = jnp.maximum(forget, 1e-30)
        s = _row_to_col(decay, eye) * s0_ref[0, h] + _row_to_col(1.0 - forget, eye) * v
        sout_ref[0, h] = s
        o = jnp.sum(_row_to_col(q, eye) * s, axis=0, keepdims=True)
        ms = jnp.mean(o * o, axis=-1, keepdims=True)
        o_ref[0, :, h * d:(h + 1) * d] = o * lax.rsqrt(ms + RMS_EPS) * gain_ref[...] * (gr * _sigmoid(gr))


def hgrn_step(p, lb_logits, out_gain, s0, layer):
    bsz = p.shape[0]
    st_spec = pl.BlockSpec((1, HG_HEADS, HG_D, HG_D), lambda i: (i, 0, 0, 0))
    o, s = pl.pallas_call(
        functools.partial(_hgrn_step_kernel, layer=layer),
        grid=(bsz,),
        in_specs=[pl.BlockSpec((1, 1, p.shape[1]), lambda i: (i, 0, 0)),
                  pl.BlockSpec(lb_logits.shape, lambda i: (0, 0)),
                  pl.BlockSpec((1, HG_D), lambda i: (0, 0)), st_spec],
        out_specs=[pl.BlockSpec((1, 1, D_MODEL), lambda i: (i, 0, 0)), st_spec],
        out_shape=[jax.ShapeDtypeStruct((bsz, 1, D_MODEL), F32),
                   jax.ShapeDtypeStruct((bsz, HG_HEADS, HG_D, HG_D), F32)],
        compiler_params=_cparams("parallel"),
    )(p.reshape(bsz, 1, -1), lb_logits, out_gain.reshape(1, HG_D), s0)
    return o.reshape(bsz, D_MODEL), s


def _even_layer(xp, xs, il, n_even, vf_p, vf_s, kv_stacks, w, cache_k, cache_v, page_table, state_rwkv,
                state_shift):
    nb_p, seq, d = xp.shape
    nb_s = xs.shape[0]
    tp = nb_p * seq
    rw_params = (w['rw_mu'][il], w['rw_w0'][il], w['w2p'][il], w['rw_a0'][il], w['a2p'][il],
                 w['rw_g2'][il], w['rw_k_k'][il], w['rw_k_a'][il])
    has_vres = il > 0
    vres_w = (w['rw_v0'][il - 1], w['rw_v1'][il - 1], w['rw_v2'][il - 1]) if has_vres else None
    ln = (w['rw_ln_w'][il], w['rw_ln_b'][il], w['rw_r_k'][il].reshape(-1))

    p_rw, p_mb = norm_matmul(xp.reshape(tp, d), w['norm_mix'][2 * il], w['w_in_even'][il],
                             (RW_IN, 3 * MB_WIDTH), 512)
    vres = (vf_p,) + vres_w if has_vres else None
    prep = rwkv_prep(p_rw, None, rw_params, vres, True, seq, 512)
    if not has_vres:
        vf_p = prep[7]
    seq3 = lambda a: a.reshape(nb_p, seq, -1)
    o_rw, wkv_p = rwkv_scan(*[seq3(a) for a in prep[:7]], *ln,
                            jnp.zeros((nb_p, RW_HEADS, RW_HEAD, RW_HEAD), F32), 1024, RW_CHUNK)
    qn, k_stack, v_stack, kb, vt, ksum = moba_prep(p_mb, w['mb_q_norm'][il], w['mb_k_norm'][il], MOBA_BLOCK,
                                                   seq, il, n_even, kv_stacks)
    o_mb = moba_attn(seq3(qn), seq3(kb), vt, ksum.reshape(nb_p, seq // MOBA_BLOCK, MB_WIDTH))
    lyr = 2 * il
    yp = mix_ffn(xp.reshape(tp, d), [o_rw.reshape(tp, RW_WIDTH), o_mb.reshape(tp, MB_WIDTH)],
                 [w['w_out_even'][il][:RW_WIDTH], w['w_out_even'][il][RW_WIDTH:]],
                 w['norm_ffn'][lyr], w['ffn_w_gate'][lyr], w['ffn_w_up'][lyr], w['ffn_w_down'][lyr],
                 512, D_FF // 2).reshape(nb_p, seq, d)
    sh_p = p_rw.reshape(nb_p, seq, RW_IN)[:, -1]

    ps_rw, ps_mb = norm_matmul(xs.reshape(nb_s, d), w['norm_mix'][2 * il], w['w_in_even'][il],
                               (RW_IN, 3 * MB_WIDTH), nb_s)
    vres = (vf_s,) + vres_w if has_vres else None
    prep = rwkv_prep(ps_rw, state_shift[il], rw_params, vres, False, 1, nb_s)
    if not has_vres:
        vf_s = prep[7]
    os_rw, wkv_s = rwkv_step(*prep[:7], *ln, state_rwkv[il])
    qs, ks, vs, _, _, _ = moba_prep(ps_mb, w['mb_q_norm'][il], w['mb_k_norm'][il], nb_s, nb_s)
    ks, vs = ks[0, 0].T, vs[0, 0].T
    os_mb = moba_decode(qs, ks, vs, cache_k, cache_v, page_table, il)
    ys = mix_ffn(xs.reshape(nb_s, d), [os_rw, os_mb],
                 [w['w_out_even'][il][:RW_WIDTH], w['w_out_even'][il][RW_WIDTH:]],
                 w['norm_ffn'][lyr], w['ffn_w_gate'][lyr], w['ffn_w_up'][lyr], w['ffn_w_down'][lyr],
                 nb_s, D_FF // 2).reshape(nb_s, 1, d)
    mk_s = ks.reshape(nb_s, 1, MB_HEADS, MB_HEAD)
    mv_s = vs.reshape(nb_s, 1, MB_HEADS, MB_HEAD)
    return yp, ys, vf_p, vf_s, (k_stack, v_stack), (mk_s, mv_s, wkv_p, wkv_s, sh_p, ps_rw)


def _odd_layer(xp, xs, il, w, state_hgrn):
    nb_p, seq, d = xp.shape
    nb_s = xs.shape[0]
    tp = nb_p * seq
    lyr = 2 * il + 1
    (pp,) = norm_matmul(xp.reshape(tp, d), w['norm_mix'][lyr], w['w_in_odd'][il], (4 * d,), 512)
    o_p, hg_p = hgrn_scan(pp.reshape(nb_p, seq, 4 * d), w['hg_lb_logits'], w['hg_out_norm'][il],
                          jnp.zeros((nb_p, HG_HEADS, HG_D, HG_D), F32), il, 2048, HG_CHUNK)
    yp = mix_ffn(xp.reshape(tp, d), [o_p.reshape(tp, d)], [w['w_out_odd'][il]],
                 w['norm_ffn'][lyr], w['ffn_w_gate'][lyr], w['ffn_w_up'][lyr], w['ffn_w_down'][lyr],
                 512, D_FF // 2).reshape(nb_p, seq, d)
    (ps,) = norm_matmul(xs.reshape(nb_s, d), w['norm_mix'][lyr], w['w_in_odd'][il], (4 * d,), nb_s)
    o_s, hg_s = hgrn_step(ps, w['hg_lb_logits'], w['hg_out_norm'][il], state_hgrn[il], il)
    ys = mix_ffn(xs.reshape(nb_s, d), [o_s], [w['w_out_odd'][il]],
                 w['norm_ffn'][lyr], w['ffn_w_gate'][lyr], w['ffn_w_up'][lyr], w['ffn_w_down'][lyr],
                 nb_s, D_FF // 2).reshape(nb_s, 1, d)
    return yp, ys, hg_p, hg_s


def kernel(x_prompt, x_sample, cache_moba_k, cache_moba_v, page_table, state_rwkv, state_rwkv_shift, state_hgrn, norm_mix, norm_ffn, w_in_even, w_out_even, rw_mu, rw_w0, rw_w2, rw_a0, rw_a2, rw_g2, rw_k_k, rw_k_a, rw_r_k, rw_ln_w, rw_ln_b, rw_v0, rw_v1, rw_v2, mb_q_norm, mb_k_norm, w_in_odd, w_out_odd, hg_lb_logits, hg_out_norm, ffn_w_gate, ffn_w_up, ffn_w_down):
    depth = norm_mix.shape[0]
    zeros_lora = jnp.zeros_like(rw_w2)
    w = dict(
        norm_mix=norm_mix, norm_ffn=norm_ffn,
        w_in_even=w_in_even.astype(BF16), w_out_even=w_out_even.astype(BF16),
        w_in_odd=w_in_odd.astype(BF16), w_out_odd=w_out_odd.astype(BF16),
        ffn_w_gate=ffn_w_gate.astype(BF16), ffn_w_up=ffn_w_up.astype(BF16),
        ffn_w_down=ffn_w_down.astype(BF16),
        rw_mu=rw_mu, rw_w0=rw_w0, rw_a0=rw_a0, rw_g2=rw_g2, rw_k_k=rw_k_k, rw_k_a=rw_k_a,
        rw_r_k=rw_r_k, rw_ln_w=rw_ln_w, rw_ln_b=rw_ln_b, rw_v0=rw_v0, rw_v1=rw_v1, rw_v2=rw_v2,
        w2p=jnp.concatenate([rw_w2, zeros_lora], axis=1), a2p=jnp.concatenate([zeros_lora, rw_a2], axis=1),
        mb_q_norm=mb_q_norm, mb_k_norm=mb_k_norm, hg_lb_logits=hg_lb_logits, hg_out_norm=hg_out_norm)
    page_table = page_table.astype(jnp.int32)
    cache_k = jnp.transpose(cache_moba_k, (0, 1, 3, 4, 2))
    cache_v = jnp.transpose(cache_moba_v, (0, 1, 3, 4, 2))

    xp, xs = x_prompt, x_sample
    vf_p = vf_s = kv_stacks = None
    n_even = w_in_even.shape[0]
    even_out, hg_out = [], []
    for layer in range(depth):
        il = layer // 2
        if layer % 2 == 0:
            xp, xs, vf_p, vf_s, kv_stacks, outs = _even_layer(xp, xs, il, n_even, vf_p, vf_s, kv_stacks, w, cache_k,
                                                              cache_v, page_table, state_rwkv, state_rwkv_shift)
            even_out.append(outs)
        else:
            xp, xs, hg_p, hg_s = _odd_layer(xp, xs, il, w, state_hgrn)
            hg_out.append((hg_p, hg_s))
    stack = lambda i: jnp.stack([o[i] for o in even_out])
    nb_p, seq, _ = x_prompt.shape
    mk_p, mv_p = (jnp.transpose(a.reshape(n_even, nb_p, MB_HEADS, MB_HEAD, seq), (0, 1, 4, 2, 3))
                  for a in kv_stacks)
    return (xp, xs, mk_p, mv_p, stack(0), stack(1), stack(2), stack(3), stack(4), stack(5),
            jnp.stack([o[0] for o in hg_out]), jnp.stack([o[1] for o in hg_out]))
```
